```python
import math
import jax
import jax.numpy as jnp
from jax import lax
import numpy as np

D_MODEL = 1024
BATCH = 8
SEQ = 4096
DEPTH = 1
DEC_BATCH = 4
DEC_SEQ = 8192
PAST_LEN = 128

GRID_W = 64
N_ATTN_HEADS = 8
HEAD_DIM = 64
ATTN_WIDTH = N_ATTN_HEADS * HEAD_DIM
WIN_ROWS_MAX = 8
WIN_COLS = 16
CONV_WIDTH = D_MODEL - ATTN_WIDTH
CONV_GROUPS = 8
CONV_K = 31
IN_COLS = 3 * ATTN_WIDTH + 2 * CONV_WIDTH
N_EXPERTS = 16
EC_FACTOR = 2
D_EXPERT = 2816
N_MOD = 6
EPS = 1e-6

kernel_name = "hymba_na_conformer_ec_encoder"


def rms_norm(x, g):
    xf = x.astype(jnp.float32)
    y = xf * lax.rsqrt(jnp.mean(xf * xf, axis=-1, keepdims=True) + EPS)
    return (y * g.astype(jnp.float32)).astype(x.dtype)


def neighbourhood_attention(q, k, v, rpb):
    B, L, H, dh = q.shape
    rows = L // GRID_W
    wr = min(WIN_ROWS_MAX, rows)
    qg = q.reshape(B, rows, GRID_W, H, dh)
    kg = k.reshape(B, rows, GRID_W, H, dh)
    vg = v.reshape(B, rows, GRID_W, H, dh)
    cols = jnp.arange(GRID_W)
    col_start = jnp.clip(cols - WIN_COLS // 2, 0, GRID_W - WIN_COLS)
    col_idx = col_start[:, None] + jnp.arange(WIN_COLS)[None, :]
    col_off = col_idx - cols[:, None] + (WIN_COLS - 1)
    scale = HEAD_DIM ** -0.5

    def one_row(args):
        q_r, r = args
        rs = jnp.clip(r - wr // 2, 0, rows - wr)
        k_blk = lax.dynamic_slice_in_dim(kg, rs, wr, axis=1)
        v_blk = lax.dynamic_slice_in_dim(vg, rs, wr, axis=1)
        k_win = k_blk[:, :, col_idx]
        v_win = v_blk[:, :, col_idx]
        row_off = rs + jnp.arange(wr) - r + (WIN_ROWS_MAX - 1)
        bias = rpb[:, row_off[:, None, None], col_off[None, :, :]]
        bias = jnp.transpose(bias, (0, 2, 1, 3)).astype(jnp.float32)
        s = jnp.einsum('bqhd,brqchd->bhqrc', q_r, k_win).astype(jnp.float32) * scale + bias[None]
        p = jax.nn.softmax(s.reshape(B, H, GRID_W, wr * WIN_COLS), axis=-1)
        p = p.reshape(B, H, GRID_W, wr, WIN_COLS).astype(v.dtype)
        return jnp.einsum('bhqrc,brqchd->bqhd', p, v_win)

    out = lax.map(one_row, (jnp.transpose(qg, (1, 0, 2, 3, 4)), jnp.arange(rows)))
    return jnp.transpose(out, (1, 0, 2, 3, 4)).reshape(B, L, H * dh)


def conformer_conv(a, gt, conv_w, conv_b, conv_ln_g, conv_ln_b):
    B, L, C = a.shape
    u = a * jax.nn.sigmoid(gt)
    u = lax.conv_general_dilated(
        u, conv_w[:, None, :].astype(u.dtype), window_strides=(1,),
        padding=[(CONV_K // 2, CONV_K // 2)], dimension_numbers=('NWC', 'WIO', 'NWC'),
        feature_group_count=C) + conv_b
    uf = u.astype(jnp.float32).reshape(B, L, CONV_GROUPS, C // CONV_GROUPS)
    mu = jnp.mean(uf, axis=-1, keepdims=True)
    var = jnp.mean(jnp.square(uf - mu), axis=-1, keepdims=True)
    un = ((uf - mu) * lax.rsqrt(var + EPS)).reshape(B, L, C)
    un = (un * conv_ln_g.astype(jnp.float32) + conv_ln_b.astype(jnp.float32)).astype(a.dtype)
    return jax.nn.silu(un)


def expert_choice_ffn(h, w_router, w_gate, w_up, w_down):
    B, L, D = h.shape
    n = B * L
    cap = max(1, EC_FACTOR * n // N_EXPERTS)
    hf = h.reshape(n, D)
    aff = jax.nn.softmax((hf @ w_router).astype(jnp.float32), axis=-1)
    g, idx = lax.top_k(aff.T, cap)
    xe = hf[idx]
    hid = jax.nn.silu(jnp.einsum('ecd,edf->ecf', xe, w_gate)) * jnp.einsum('ecd,edf->ecf', xe, w_up)
    ye = jnp.einsum('ecf,efd->ecd', hid, w_down) * g[..., None].astype(h.dtype)
    y = jnp.zeros((n, D), h.dtype).at[idx.reshape(-1)].add(ye.reshape(-1, D))
    return y.reshape(B, L, D)


def encoder_layer(x, c, ada_w, ada_b, norm1_g, w_in, q_norm_g, k_norm_g, rpb,
                  conv_w, conv_b, conv_ln_g, conv_ln_b, w_out, norm2_g,
                  w_router, w_gate, w_up, w_down):
    B, L, D = x.shape
    ada = jax.nn.silu(c) @ ada_w + ada_b
    shift1, scale1, gate1, shift2, scale2, gate2 = [m[:, None, :] for m in jnp.split(ada, N_MOD, axis=-1)]

    h = rms_norm(x, norm1_g) * (1 + scale1) + shift1
    z = h @ w_in
    q, k, v, ca, cg = jnp.split(
        z, [ATTN_WIDTH, 2 * ATTN_WIDTH, 3 * ATTN_WIDTH, 3 * ATTN_WIDTH + CONV_WIDTH], axis=-1)
    q = rms_norm(q.reshape(B, L, N_ATTN_HEADS, HEAD_DIM), q_norm_g)
    k = rms_norm(k.reshape(B, L, N_ATTN_HEADS, HEAD_DIM), k_norm_g)
    v = v.reshape(B, L, N_ATTN_HEADS, HEAD_DIM)
    attn_out = neighbourhood_attention(q, k, v, rpb)
    conv_out = conformer_conv(ca, cg, conv_w, conv_b, conv_ln_g, conv_ln_b)
    mix = jnp.concatenate([attn_out, conv_out], axis=-1) @ w_out
    x = x + gate1 * mix

    h2 = rms_norm(x, norm2_g) * (1 + scale2) + shift2
    x = x + gate2 * expert_choice_ffn(h2, w_router, w_gate, w_up, w_down)
    return x


def setup_inputs(seed: int = 0) -> dict:
    key = jax.random.key(seed)
    ks = jax.random.split(key, 24)
    f32 = jnp.float32
    nrm = lambda k, shape, s: (jax.random.normal(k, shape, f32) * s).astype(f32)
    return {
        "x_prompt": nrm(ks[0], (BATCH, SEQ, D_MODEL), 1.0),
        "x_sample": nrm(ks[1], (DEC_BATCH, DEC_SEQ, D_MODEL), 1.0),
        "c_prompt": nrm(ks[2], (BATCH, D_MODEL), 1.0),
        "c_sample": nrm(ks[3], (DEC_BATCH, D_MODEL), 1.0),
        "ada_w": nrm(ks[4], (D_MODEL, N_MOD * D_MODEL), 0.5 * D_MODEL ** -0.5),
        "ada_b": nrm(ks[5], (N_MOD * D_MODEL,), 0.01),
        "norm1_g": 1.0 + nrm(ks[6], (D_MODEL,), 0.01),
        "w_in": nrm(ks[7], (D_MODEL, IN_COLS), D_MODEL ** -0.5),
        "q_norm_g": 1.0 + nrm(ks[8], (HEAD_DIM,), 0.01),
        "k_norm_g": 1.0 + nrm(ks[9], (HEAD_DIM,), 0.01),
        "rpb": nrm(ks[10], (N_ATTN_HEADS, 2 * WIN_ROWS_MAX - 1, 2 * WIN_COLS - 1), 0.1),
        "conv_w": nrm(ks[11], (CONV_K, CONV_WIDTH), CONV_K ** -0.5),
        "conv_b": nrm(ks[12], (CONV_WIDTH,), 0.01),
        "conv_ln_g": 1.0 + nrm(ks[13], (CONV_WIDTH,), 0.01),
        "conv_ln_b": nrm(ks[14], (CONV_WIDTH,), 0.01),
        "w_out": nrm(ks[15], (D_MODEL, D_MODEL), D_MODEL ** -0.5),
        "norm2_g": 1.0 + nrm(ks[16], (D_MODEL,), 0.01),
        "w_router": nrm(ks[17], (D_MODEL, N_EXPERTS), D_MODEL ** -0.5),
        "w_gate": nrm(ks[18], (N_EXPERTS, D_MODEL, D_EXPERT), D_MODEL ** -0.5),
        "w_up": nrm(ks[19], (N_EXPERTS, D_MODEL, D_EXPERT), D_MODEL ** -0.5),
        "w_down": nrm(ks[20], (N_EXPERTS, D_EXPERT, D_MODEL), D_EXPERT ** -0.5),
    }


def reference(x_prompt, x_sample, c_prompt, c_sample, ada_w, ada_b, norm1_g, w_in,
              q_norm_g, k_norm_g, rpb, conv_w, conv_b, conv_ln_g, conv_ln_b, w_out,
              norm2_g, w_router, w_gate, w_up, w_down):
    y_prompt = x_prompt
    y_sample = x_sample
    for _ in range(DEPTH):
        y_prompt = encoder_layer(y_prompt, c_prompt, ada_w, ada_b, norm1_g, w_in, q_norm_g, k_norm_g, rpb,
                                 conv_w, conv_b, conv_ln_g, conv_ln_b, w_out, norm2_g,
                                 w_router, w_gate, w_up, w_down)
        y_sample = encoder_layer(y_sample, c_sample, ada_w, ada_b, norm1_g, w_in, q_norm_g, k_norm_g, rpb,
                                 conv_w, conv_b, conv_ln_g, conv_ln_b, w_out, norm2_g,
                                 w_router, w_gate, w_up, w_down)
    return (y_prompt, y_sample)
```

```python
import functools

import jax
import jax.numpy as jnp
from jax import lax
from jax.experimental import pallas as pl
from jax.experimental.pallas import tpu as pltpu

f32 = jnp.float32
bf16 = jnp.bfloat16
i32 = jnp.int32

GRID_W = 64
N_HEADS = 8
HEAD_DIM = 64
ATTN_W = N_HEADS * HEAD_DIM
CONV_W = 512
CONV_K = 31
CONV_GROUP = 64
WIN_ROWS = 8
WIN_COLS = 16
N_EXPERTS = 16
EC_FACTOR = 2
N_MOD = 6
EPS = 1e-6
NEG = -1e30

LANES = 128
VMEM_LIMIT = 56 * 1024 * 1024

NT_DIMS = (((1,), (1,)), ((), ()))


def _params(*sem):
    return pltpu.CompilerParams(dimension_semantics=sem, vmem_limit_bytes=VMEM_LIMIT)


def _split_bf16(x):
    hi = x.astype(bf16)
    lo = (x - hi.astype(f32)).astype(bf16)
    return hi, lo


def _ada_kernel(c_ref, w_ref, b_ref, o_ref):
    c = c_ref[...]
    s = c * jax.nn.sigmoid(c)
    s_hi, s_lo = _split_bf16(s)
    w_hi, w_lo = _split_bf16(w_ref[...])
    acc = jnp.dot(s_hi, w_hi, preferred_element_type=f32)
    acc += jnp.dot(s_hi, w_lo, preferred_element_type=f32)
    acc += jnp.dot(s_lo, w_hi, preferred_element_type=f32)
    o_ref[...] = acc + b_ref[...]


def _ada(c_all, ada_w, ada_b):
    rows, d = c_all.shape
    ncols = ada_w.shape[1]
    tn = next(t for t in (1024, 512, 256, 128) if ncols % t == 0)
    return pl.pallas_call(
        _ada_kernel,
        grid=(ncols // tn,),
        in_specs=[pl.BlockSpec((rows, d), lambda j: (0, 0)),
                  pl.BlockSpec((d, tn), lambda j: (0, j)),
                  pl.BlockSpec((1, tn), lambda j: (0, j))],
        out_specs=pl.BlockSpec((rows, tn), lambda j: (0, j)),
        out_shape=jax.ShapeDtypeStruct((rows, ncols), f32),
        compiler_params=_params("arbitrary"),
        name="ada",
    )(c_all, ada_w, ada_b.reshape(1, ncols))


def _inproj_kernel(x_ref, mod_ref, g1_ref, w_ref, seg_ref, gq_ref, gk_ref,
                   q_ref, k_ref, v_ref, u_ref):
    x = x_ref[...]
    d = x.shape[-1]
    ms = jnp.mean(x * x, axis=-1, keepdims=True)
    h = x * lax.rsqrt(ms + EPS) * g1_ref[...]
    shift = mod_ref[:, 0:d]
    scale = mod_ref[:, d:2 * d]
    h = h * (1.0 + scale) + shift
    z = jnp.dot(h.astype(bf16), w_ref[...], preferred_element_type=f32)
    zq = z[:, 0:ATTN_W]
    zk = z[:, ATTN_W:2 * ATTN_W]
    seg = seg_ref[...]
    ssq_q = jnp.dot((zq * zq).astype(bf16), seg, preferred_element_type=f32)
    ssq_k = jnp.dot((zk * zk).astype(bf16), seg, preferred_element_type=f32)
    qs = gq_ref[...] * gk_ref[...] * (HEAD_DIM ** -0.5)
    q_ref[...] = (zq * lax.rsqrt(ssq_q * (1.0 / HEAD_DIM) + EPS) * qs).astype(bf16)
    k_ref[...] = (zk * lax.rsqrt(ssq_k * (1.0 / HEAD_DIM) + EPS)).astype(bf16)
    v_ref[...] = z[:, 2 * ATTN_W:3 * ATTN_W].astype(bf16)
    a = z[:, 3 * ATTN_W:3 * ATTN_W + CONV_W]
    gt = z[:, 3 * ATTN_W + CONV_W:]
    u_ref[...] = a * jax.nn.sigmoid(gt)


def _inproj(x, mod, norm1_g, w_in_b, seg, gq_t, gk_t):
    b, l, d = x.shape
    tm = min(512, l)
    ncols = w_in_b.shape[1]
    tok = lambda w: pl.BlockSpec((None, tm, w), lambda bi, i: (bi, i, 0))
    full = lambda a: pl.BlockSpec(a.shape, lambda bi, i: (0,) * a.ndim)
    return pl.pallas_call(
        _inproj_kernel,
        grid=(b, l // tm),
        in_specs=[tok(d),
                  pl.BlockSpec((None, 1, mod.shape[-1]), lambda bi, i: (bi, 0, 0)),
                  full(norm1_g), full(w_in_b), full(seg), full(gq_t), full(gk_t)],
        out_specs=[tok(ATTN_W), tok(ATTN_W), tok(ATTN_W), tok(CONV_W)],
        out_shape=[jax.ShapeDtypeStruct((b, l, ATTN_W), bf16)] * 3
                  + [jax.ShapeDtypeStruct((b, l, CONV_W), f32)],
        compiler_params=_params("arbitrary", "arbitrary"),
        name="inproj",
    )(x, mod, norm1_g, w_in_b, seg, gq_t, gk_t)


def _attn_kernel(q_ref, k_ref, v_ref, bias_ref, o_ref, *, rows, rg, wr):
    g = pl.program_id(1)
    lane = lax.broadcasted_iota(i32, (GRID_W, LANES), 1)
    first_head = lane < HEAD_DIM
    nkeys = wr * GRID_W

    def row_body(rr, carry):
        r = g * rg + rr
        rs = jnp.clip(r - wr // 2, 0, rows - wr)
        tok0 = pl.multiple_of(rs * GRID_W, GRID_W)
        rb = rs - r + (WIN_ROWS - 1)
        q0 = pl.multiple_of(rr * GRID_W, GRID_W)
        for p in range(N_HEADS // 2):
            cols = slice(p * LANES, (p + 1) * LANES)
            qp = q_ref[pl.ds(q0, GRID_W), cols]
            kp = k_ref[pl.ds(tok0, nkeys), cols]
            vp = v_ref[pl.ds(tok0, nkeys), cols]
            outs = []
            for j in range(2):
                h = 2 * p + j
                qm = jnp.where(first_head if j == 0 else jnp.logical_not(first_head),
                               qp, jnp.zeros_like(qp))
                s = lax.dot_general(qm, kp, NT_DIMS, preferred_element_type=f32)
                bias = jnp.concatenate(
                    [bias_ref[h, rb + 2 * m] for m in range(nkeys // LANES)], axis=-1)
                s = s + bias
                mx = jnp.max(s, axis=-1, keepdims=True)
                e = jnp.exp(s - mx)
                den = jnp.sum(e, axis=-1, keepdims=True)
                o = jnp.dot(e.astype(bf16), vp, preferred_element_type=f32)
                outs.append(o / den)
            o_ref[pl.ds(q0, GRID_W), cols] = jnp.where(first_head, outs[0], outs[1]).astype(bf16)
        return carry

    lax.fori_loop(0, rg, row_body, 0)


def _attention(q, k, v, bias_tab):
    b, l, _ = q.shape
    rows = l // GRID_W
    wr = min(WIN_ROWS, rows)
    rg = min(8, rows)
    once = pl.Buffered(1)
    kern = functools.partial(_attn_kernel, rows=rows, rg=rg, wr=wr)
    return pl.pallas_call(
        kern,
        grid=(b, rows // rg),
        in_specs=[pl.BlockSpec((None, rg * GRID_W, ATTN_W), lambda bi, g: (bi, g, 0)),
                  pl.BlockSpec((None, l, ATTN_W), lambda bi, g: (bi, 0, 0), pipeline_mode=once),
                  pl.BlockSpec((None, l, ATTN_W), lambda bi, g: (bi, 0, 0), pipeline_mode=once),
                  pl.BlockSpec(bias_tab.shape, lambda bi, g: (0, 0, 0, 0), pipeline_mode=once)],
        out_specs=pl.BlockSpec((None, rg * GRID_W, ATTN_W), lambda bi, g: (bi, g, 0)),
        out_shape=jax.ShapeDtypeStruct((b, l, ATTN_W), bf16),
        compiler_params=_params("arbitrary", "arbitrary"),
        name="attention",
    )(q, k, v, bias_tab)


def _bias_table(rpb):
    cols = jnp.arange(GRID_W)
    col_start = jnp.clip(cols - WIN_COLS // 2, 0, GRID_W - WIN_COLS)
    kc = jnp.arange(GRID_W)[None, :]
    valid = (kc >= col_start[:, None]) & (kc < col_start[:, None] + WIN_COLS)
    off = jnp.clip(kc - cols[:, None] + (WIN_COLS - 1), 0, 2 * WIN_COLS - 2)
    tab = jnp.where(valid[None, None], rpb[:, :, off].astype(f32), NEG)
    return jnp.concatenate([tab[:, :-1], tab[:, 1:]], axis=-1)


def _conv_kernel(u_ref, prev_ref, next_ref, w_ref, cb_ref, lg_ref, lb_ref, seg_ref, o_ref, win_ref,
                 *, tl, chunk):
    i = pl.program_id(1)
    n_i = pl.num_programs(1)
    halo = prev_ref.shape[0]
    win_ref[0:halo, :] = jnp.where(i > 0, prev_ref[...], 0.0)
    win_ref[halo:halo + tl, :] = u_ref[...]
    win_ref[halo + tl:, :] = jnp.where(i < n_i - 1, next_ref[...], 0.0)
    seg = seg_ref[...]
    pad = CONV_K // 2
    for c in range(tl // chunk):
        acc = jnp.zeros((chunk, CONV_W), f32)
        for j in range(CONV_K):
            start = halo - pad + c * chunk + j
            acc = acc + win_ref[start:start + chunk, :] * w_ref[j:j + 1, :]
        y = acc + cb_ref[...]
        y_hi, y_lo = _split_bf16(y)
        mu = (jnp.dot(y_hi, seg, preferred_element_type=f32)
              + jnp.dot(y_lo, seg, preferred_element_type=f32)) * (1.0 / CONV_GROUP)
        dlt = y - mu
        d_hi, d_lo = _split_bf16(dlt * dlt)
        var = (jnp.dot(d_hi, seg, preferred_element_type=f32)
               + jnp.dot(d_lo, seg, preferred_element_type=f32)) * (1.0 / CONV_GROUP)
        un = dlt * lax.rsqrt(var + EPS) * lg_ref[...] + lb_ref[...]
        o_ref[c * chunk:(c + 1) * chunk, :] = (un * jax.nn.sigmoid(un)).astype(bf16)


def _conv(u, conv_w, conv_b, ln_g, ln_b, seg):
    b, l, _ = u.shape
    tl = min(512, l)
    halo = 16
    nh = l // halo
    kern = functools.partial(_conv_kernel, tl=tl, chunk=min(64, tl))
    full = lambda a: pl.BlockSpec(a.shape, lambda bi, i: (0,) * a.ndim)
    return pl.pallas_call(
        kern,
        grid=(b, l // tl),
        in_specs=[pl.BlockSpec((None, tl, CONV_W), lambda bi, i: (bi, i, 0)),
                  pl.BlockSpec((None, halo, CONV_W),
                               lambda bi, i: (bi, jnp.maximum(i * (tl // halo) - 1, 0), 0)),
                  pl.BlockSpec((None, halo, CONV_W),
                               lambda bi, i: (bi, jnp.minimum((i + 1) * (tl // halo), nh - 1), 0)),
                  full(conv_w), full(conv_b), full(ln_g), full(ln_b), full(seg)],
        out_specs=pl.BlockSpec((None, tl, CONV_W), lambda bi, i: (bi, i, 0)),
        out_shape=jax.ShapeDtypeStruct((b, l, CONV_W), bf16),
        scratch_shapes=[pltpu.VMEM((tl + 2 * halo, CONV_W), f32)],
        compiler_params=_params("arbitrary", "arbitrary"),
        name="conv",
    )(u, u, u, conv_w, conv_b, ln_g, ln_b, seg)


def _outproj_kernel(at_ref, cv_ref, x_ref, mod_ref, wo_ref, g2_ref, wr_ref,
                    x1_ref, h2_ref, aff_ref):
    x = x_ref[...]
    d = x.shape[-1]
    mix = jnp.dot(at_ref[...], wo_ref[0:ATTN_W, :], preferred_element_type=f32)
    mix += jnp.dot(cv_ref[...], wo_ref[ATTN_W:, :], preferred_element_type=f32)
    gate1 = mod_ref[:, 2 * d:3 * d]
    shift2 = mod_ref[:, 3 * d:4 * d]
    scale2 = mod_ref[:, 4 * d:5 * d]
    x1 = x + gate1 * mix
    x1_ref[...] = x1
    ms = jnp.mean(x1 * x1, axis=-1, keepdims=True)
    h2 = x1 * lax.rsqrt(ms + EPS) * g2_ref[...]
    h2 = h2 * (1.0 + scale2) + shift2
    h_hi, h_lo = _split_bf16(h2)
    tm = x.shape[0]
    h_rows = h_hi.astype(f32)
    for c in range(d // LANES):
        h2_ref[pl.ds(c, tm, stride=d // LANES), :] = h_rows[:, c * LANES:(c + 1) * LANES]
    w_hi, w_lo = _split_bf16(wr_ref[...])
    logits = lax.dot_general(w_hi, h_hi, NT_DIMS, preferred_element_type=f32)
    logits += lax.dot_general(w_hi, h_lo, NT_DIMS, preferred_element_type=f32)
    logits += lax.dot_general(w_lo, h_hi, NT_DIMS, preferred_element_type=f32)
    mx = jnp.max(logits, axis=0, keepdims=True)
    ex = jnp.exp(logits - mx)
    aff_ref[...] = ex / jnp.sum(ex, axis=0, keepdims=True)


def _outproj(attn, conv, x, mod, w_out_b, norm2_g, w_router_t):
    b, l, d = x.shape
    tm = min(512, l)
    nt = l // tm
    e = w_router_t.shape[0]
    tok = lambda w: pl.BlockSpec((None, tm, w), lambda bi, i: (bi, i, 0))
    full = lambda a: pl.BlockSpec(a.shape, lambda bi, i: (0,) * a.ndim)
    return pl.pallas_call(
        _outproj_kernel,
        grid=(b, nt),
        in_specs=[tok(ATTN_W), tok(CONV_W), tok(d),
                  pl.BlockSpec((None, 1, mod.shape[-1]), lambda bi, i: (bi, 0, 0)),
                  full(w_out_b), full(norm2_g), full(w_router_t)],
        out_specs=[tok(d),
                   pl.BlockSpec((tm * (d // LANES), LANES), lambda bi, i: (bi * nt + i, 0)),
                   pl.BlockSpec((e, tm), lambda bi, i: (0, bi * nt + i))],
        out_shape=[jax.ShapeDtypeStruct((b, l, d), f32),
                   jax.ShapeDtypeStruct((b * l * (d // LANES), LANES), f32),
                   jax.ShapeDtypeStruct((e, b * l), f32)],
        compiler_params=_params("arbitrary", "arbitrary"),
        name="outproj",
    )(attn, conv, x, mod, w_out_b, norm2_g, w_router_t)


def _route_kernel(aff_ref, cs_ref, tst_ref, cbk_ref, *, cap, tsc, tbc):
    a = aff_ref[...]
    ne, nc, _ = a.shape
    keys = lax.bitcast_convert_type(a, i32)

    def count(mask):
        return jnp.sum(mask.astype(f32), axis=(1, 2), keepdims=True)

    def bit_body(it, prefix):
        cand = prefix | jnp.left_shift(jnp.int32(1), 30 - it)
        return jnp.where(count(keys >= cand) >= cap, cand, prefix)

    thr = lax.fori_loop(0, 31, bit_body, jnp.zeros((ne, 1, 1), i32))

    r_i = lax.broadcasted_iota(i32, (LANES, LANES), 0)
    c_i = lax.broadcasted_iota(i32, (LANES, LANES), 1)
    upper = (r_i <= c_i).astype(bf16)
    ones = jnp.ones((LANES, LANES), bf16)
    rr = lax.broadcasted_iota(i32, (nc, nc), 0)
    cc = lax.broadcasted_iota(i32, (nc, nc), 1)
    lower = (cc < rr).astype(bf16)
    blk_r = lax.broadcasted_iota(i32, (nc, LANES), 0)
    blk_c = lax.broadcasted_iota(i32, (nc, LANES), 1)
    before_blk = (blk_r < blk_c * (tbc // LANES)).astype(bf16)
    ones8 = jnp.ones((8, LANES), bf16)
    lane8 = lax.broadcasted_iota(i32, (1, LANES), 1)

    def cumsum(m):
        within = jnp.dot(m, upper, preferred_element_type=f32)
        prev = jnp.dot(lower, m, preferred_element_type=f32)
        offs = jnp.dot(prev.astype(bf16), ones, preferred_element_type=f32)
        return within + offs

    for e in range(ne):
        keys_e = keys[e]
        thr_e = thr[e]
        gt_e = keys_e > thr_e
        eq_e = (keys_e == thr_e).astype(f32)
        need = cap - jnp.sum(gt_e.astype(f32), keepdims=True)
        rank_eq = cumsum(eq_e.astype(bf16)) - eq_e
        sel = jnp.logical_or(gt_e, jnp.logical_and(eq_e > 0.0, rank_eq < need))
        sel_b = sel.astype(bf16)
        c_incl = cumsum(sel_b)
        cs_ref[e] = jnp.where(sel, c_incl, 0.0).astype(i32)
        tst = jnp.zeros((1, LANES), f32)
        for j in range(cap // tsc + 1):
            cnt_j = jnp.sum((c_incl <= float(j * tsc)).astype(f32), keepdims=True)
            tst = jnp.where(lane8 == j, cnt_j, tst)
        tst_ref[e:e + 1, :] = tst.astype(i32)
        per_chunk = lax.dot_general(ones8, sel_b, NT_DIMS, preferred_element_type=f32)
        cbk = jnp.dot(per_chunk.astype(bf16), before_blk, preferred_element_type=f32)
        cbk_ref[e:e + 1, :] = cbk[0:1].astype(i32)


def _route(aff_t, cap, tsc, tbc):
    e, n = aff_t.shape
    nc = n // LANES
    kern = functools.partial(_route_kernel, cap=cap, tsc=tsc, tbc=tbc)
    return pl.pallas_call(
        kern,
        out_shape=[jax.ShapeDtypeStruct((e, nc, LANES), i32),
                   jax.ShapeDtypeStruct((e, LANES), i32),
                   jax.ShapeDtypeStruct((e, LANES), i32)],
        compiler_params=pltpu.CompilerParams(vmem_limit_bytes=VMEM_LIMIT),
        name="route",
    )(aff_t.reshape(e, nc, LANES))


def _compact_kernel(tst_ref, cs_ref, aff_ref, idx_ref, g_ref, *, tsc, tbc):
    e = pl.program_id(0)
    j = pl.program_id(1)
    shift = tbc.bit_length() - 1
    f_lo = tst_ref[e * LANES + j]
    f_hi = tst_ref[e * LANES + j + 1]
    b_lo = jnp.right_shift(f_lo, shift)
    b_hi = jnp.right_shift(f_hi - 1, shift)
    slot1 = j * tsc + 1 + lax.broadcasted_iota(i32, (tsc, tbc), 0)
    sub = lax.broadcasted_iota(i32, (8, tbc), 0)
    lane = lax.broadcasted_iota(i32, (8, tbc), 1)

    def body(b, acc):
        row = cs_ref[e, pl.ds(b, 1), :]
        a = aff_ref[e, pl.ds(b, 1), :]
        onehot_t = (row == slot1).astype(f32).astype(bf16)
        t = b * tbc + lane
        a_hi = a.astype(bf16).astype(f32)
        r1 = a - a_hi
        a_mid = r1.astype(bf16).astype(f32)
        a_lo = r1 - a_mid
        pay = jnp.where(sub == 0, jnp.bitwise_and(t, 255).astype(f32),
              jnp.where(sub == 1, jnp.right_shift(t, 8).astype(f32),
              jnp.where(sub == 2, a_hi,
              jnp.where(sub == 3, a_mid,
              jnp.where(sub == 4, a_lo, 0.0)))))
        return acc + lax.dot_general(pay.astype(bf16), onehot_t, NT_DIMS, preferred_element_type=f32)

    acc = lax.fori_loop(b_lo, b_hi + 1, body, jnp.zeros((8, tsc), f32))
    idx = acc[1:2] * 256.0 + acc[0:1]
    idx_ref[...] = jnp.broadcast_to(idx, (8, tsc)).astype(i32)
    g_ref[...] = jnp.broadcast_to(acc[2:3] + acc[3:4] + acc[4:5], (8, tsc))


def _compact(tst, cs, aff_t, cap, tsc, tbc):
    e, n = aff_t.shape
    nb = n // tbc
    ntile = cap // tsc
    kern = functools.partial(_compact_kernel, tsc=tsc, tbc=tbc)
    res = pl.BlockSpec((e, nb, tbc), lambda ei, j, tst_r: (0, 0, 0))
    out = pl.BlockSpec((None, None, 8, tsc), lambda ei, j, tst_r: (ei, j, 0, 0))
    idx8, g8 = pl.pallas_call(
        kern,
        grid_spec=pltpu.PrefetchScalarGridSpec(
            num_scalar_prefetch=1, grid=(e, ntile), in_specs=[res, res], out_specs=[out, out]),
        out_shape=[jax.ShapeDtypeStruct((e, ntile, 8, tsc), i32),
                   jax.ShapeDtypeStruct((e, ntile, 8, tsc), f32)],
        compiler_params=_params("arbitrary", "arbitrary"),
        name="compact",
    )(tst.reshape(-1), cs.reshape(e, nb, tbc), aff_t.reshape(e, nb, tbc))
    return idx8[:, :, 0, :].reshape(e, cap), g8[:, :, 0, :].reshape(e, cap)


def _gather_kernel(idx_ref, h0_ref, h1_ref, xe_ref, sem, *, rb, cap, steps_per_group, rpt):
    i = pl.program_id(0)
    grp = i // steps_per_group
    li = i - grp * steps_per_group
    e = (li * rb) // cap
    s0 = li * rb - e * cap
    base = e * 2 * cap + grp * cap + s0

    def token_rows(t):
        return pl.ds(pl.multiple_of(t * rpt, rpt), rpt)

    def issue(src_ref):
        def body(r, carry):
            tok = idx_ref[0, r]
            pltpu.make_async_copy(src_ref.at[token_rows(tok)], xe_ref.at[token_rows(base + r)], sem).start()
            return carry
        lax.fori_loop(0, rb, body, 0, unroll=8)

    @pl.when(grp == 0)
    def _():
        issue(h0_ref)

    @pl.when(grp == 1)
    def _():
        issue(h1_ref)

    step_rows = pl.ds(pl.multiple_of(base * rpt, rpt), rb * rpt)
    pltpu.make_async_copy(h0_ref.at[pl.ds(0, rb * rpt)], xe_ref.at[step_rows], sem).wait()


def _gather(idx_all, h0, h1, cap, rpt):
    d = h0.shape[-1]
    total = idx_all.shape[1]
    rb = min(512, cap)
    steps_per_group = total // rb
    kern = functools.partial(_gather_kernel, rb=rb, cap=cap, steps_per_group=steps_per_group, rpt=rpt)
    return pl.pallas_call(
        kern,
        grid=(2 * steps_per_group,),
        in_specs=[pl.BlockSpec((None, 1, rb), lambda i: (i, 0, 0), memory_space=pltpu.SMEM),
                  pl.BlockSpec(memory_space=pl.ANY),
                  pl.BlockSpec(memory_space=pl.ANY)],
        out_specs=pl.BlockSpec(memory_space=pl.ANY),
        out_shape=jax.ShapeDtypeStruct((2 * total * rpt, d), f32),
        scratch_shapes=[pltpu.SemaphoreType.DMA],
        compiler_params=_params("arbitrary"),
        name="gather",
    )(idx_all.reshape(2 * steps_per_group, 1, rb), h0, h1)


def _expert_kernel(x_ref, g_ref, wg_ref, wu_ref, wd_ref, y_ref, xb_ref, acc_ref, gcol_ref):
    f = pl.program_id(2)

    @pl.when(f == 0)
    def _():
        tm, d = xb_ref.shape
        rpt = d // LANES
        for c in range(rpt):
            xb_ref[:, c * LANES:(c + 1) * LANES] = x_ref[pl.ds(c, tm, stride=rpt), :].astype(bf16)
        acc_ref[...] = jnp.zeros_like(acc_ref)
        gcol_ref[...] = jnp.broadcast_to(g_ref[...], (LANES, g_ref.shape[-1])).T

    xb = xb_ref[...]
    hg = jnp.dot(xb, wg_ref[...], preferred_element_type=f32)
    hu = jnp.dot(xb, wu_ref[...], preferred_element_type=f32)
    hid = (hg * jax.nn.sigmoid(hg) * hu).astype(bf16)
    acc_ref[...] += jnp.dot(hid, wd_ref[...], preferred_element_type=f32)

    @pl.when(f == pl.num_programs(2) - 1)
    def _():
        gcol = gcol_ref[...]
        d = acc_ref.shape[-1]
        for c in range(d // LANES):
            cols = slice(c * LANES, (c + 1) * LANES)
            y_ref[:, cols] = acc_ref[:, cols] * gcol


def _experts(xe, g_rows, wg_b, wu_b, wd_b, tm, fc):
    e, d, fdim = wg_b.shape
    rpt = d // LANES
    rows = xe.shape[0] // rpt
    per_e = rows // e
    nt = per_e // tm
    return pl.pallas_call(
        _expert_kernel,
        grid=(e, nt, fdim // fc),
        in_specs=[pl.BlockSpec((tm * rpt, LANES), lambda ei, t, f: (ei * nt + t, 0)),
                  pl.BlockSpec((None, 1, tm), lambda ei, t, f: (ei * nt + t, 0, 0)),
                  pl.BlockSpec((None, d, fc), lambda ei, t, f: (ei, 0, f)),
                  pl.BlockSpec((None, d, fc), lambda ei, t, f: (ei, 0, f)),
                  pl.BlockSpec((None, fc, d), lambda ei, t, f: (ei, f, 0))],
        out_specs=pl.BlockSpec((tm, d), lambda ei, t, f: (ei * nt + t, 0)),
        out_shape=jax.ShapeDtypeStruct((rows, d), f32),
        scratch_shapes=[pltpu.VMEM((tm, d), bf16), pltpu.VMEM((tm, d), f32),
                        pltpu.VMEM((tm, LANES), f32)],
        compiler_params=_params("arbitrary", "arbitrary", "arbitrary"),
        name="experts",
    )(xe, g_rows.reshape(rows // tm, 1, tm), wg_b, wu_b, wd_b)


def _combine_kernel(cbk_ref, cs_ref, x1_ref, mod_ref, ye_ref, o_ref, win_ref, sem, acc_ref,
                    *, grp, cap, tbc, win, total_rows):
    blk = pl.program_id(0)
    e = pl.program_id(1)
    ne = pl.num_programs(1)
    step = blk * ne + e
    nsteps = pl.num_programs(0) * ne
    d = acc_ref.shape[-1]

    def region(ee):
        return ee * 2 * cap + grp * cap

    def first_row(bb, ee):
        st = cbk_ref[ee * LANES + bb]
        row = region(ee) + jnp.left_shift(jnp.right_shift(st, 3), 3)
        return pl.multiple_of(jnp.minimum(row, total_rows - win), 8)

    def window_copy(row0, slot):
        return pltpu.make_async_copy(ye_ref.at[pl.ds(row0, win)], win_ref.at[slot], sem.at[slot])

    row0 = first_row(blk, e)
    slot = step % 2

    @pl.when(step == 0)
    def _():
        window_copy(row0, slot).start()

    nxt = step + 1

    @pl.when(nxt < nsteps)
    def _():
        nb = nxt // ne
        window_copy(first_row(nb, nxt - nb * ne), nxt % 2).start()

    @pl.when(e == 0)
    def _():
        acc_ref[...] = jnp.zeros_like(acc_ref)

    end = cbk_ref[e * LANES + blk + 1]
    cs_row = cs_ref[...]
    w_iota = lax.broadcasted_iota(i32, (win, tbc), 0)

    def expand(row_w, buf, done_slots):
        base_slot = row_w - region(e)
        hit = jnp.logical_and(cs_row == base_slot + 1 + w_iota, cs_row > done_slots)
        onehot = hit.astype(f32).T.astype(bf16)
        acc_ref[...] += jnp.dot(onehot, buf.astype(bf16), preferred_element_type=f32)

    window_copy(row0, slot).wait()
    expand(row0, win_ref[slot], 0)

    base0 = row0 - region(e)
    n_win = (end - base0 + win - 1) // win

    def extra(w, carry):
        row_w = pl.multiple_of(jnp.minimum(row0 + w * win, total_rows - win), 8)
        cp = window_copy(row_w, 2)
        cp.start()
        cp.wait()
        expand(row_w, win_ref[2], base0 + w * win)
        return carry

    lax.fori_loop(1, n_win, extra, 0)

    @pl.when(e == ne - 1)
    def _():
        gate2 = mod_ref[:, 5 * d:6 * d]
        o_ref[...] = x1_ref[...] + gate2 * acc_ref[...]


def _combine(cbk, cs, x1, mod, ye, grp, cap, tbc):
    b, l, d = x1.shape
    n = b * l
    e = cs.shape[0]
    nb = n // tbc
    win = min(128, cap)
    per_b = l // tbc
    kern = functools.partial(_combine_kernel, grp=grp, cap=cap, tbc=tbc, win=win,
                             total_rows=ye.shape[0])
    return pl.pallas_call(
        kern,
        grid_spec=pltpu.PrefetchScalarGridSpec(
            num_scalar_prefetch=1, grid=(nb, e),
            in_specs=[pl.BlockSpec((None, None, 1, tbc), lambda bk, ei, c: (ei, bk, 0, 0)),
                      pl.BlockSpec((tbc, d), lambda bk, ei, c: (bk, 0)),
                      pl.BlockSpec((None, 1, mod.shape[-1]), lambda bk, ei, c: (bk // per_b, 0, 0)),
                      pl.BlockSpec(memory_space=pl.ANY)],
            out_specs=pl.BlockSpec((tbc, d), lambda bk, ei, c: (bk, 0)),
            scratch_shapes=[pltpu.VMEM((3, win, d), f32), pltpu.SemaphoreType.DMA((3,)),
                            pltpu.VMEM((tbc, d), f32)]),
        out_shape=jax.ShapeDtypeStruct((n, d), f32),
        compiler_params=_params("arbitrary", "arbitrary"),
        name="combine",
    )(cbk.reshape(-1), cs.reshape(e, nb, 1, tbc), x1.reshape(n, d), mod, ye).reshape(b, l, d)


TSC = 256
TBC = 256
TBF = 512
TM_EXPERT = 1024
FC_EXPERT = 256


def kernel(x_prompt, x_sample, c_prompt, c_sample, ada_w, ada_b, norm1_g, w_in, q_norm_g, k_norm_g, rpb,
           conv_w, conv_b, conv_ln_g, conv_ln_b, w_out, norm2_g, w_router, w_gate, w_up, w_down):
    xs = (x_prompt, x_sample)
    d = x_prompt.shape[-1]
    ne = w_router.shape[1]
    row = lambda a: a.reshape(1, -1).astype(f32)

    nb0, nb1 = c_prompt.shape[0], c_sample.shape[0]
    pad = (-(nb0 + nb1)) % 8
    c_all = jnp.concatenate([c_prompt, c_sample, jnp.zeros((pad, d), f32)], axis=0)
    mod_all = _ada(c_all, ada_w, ada_b)
    mods = (mod_all[:nb0].reshape(nb0, 1, -1), mod_all[nb0:nb0 + nb1].reshape(nb1, 1, -1))

    seg_i = jnp.arange(ATTN_W) // HEAD_DIM
    seg = (seg_i[:, None] == seg_i[None, :]).astype(bf16)
    gq_t = jnp.tile(q_norm_g.astype(f32), N_HEADS).reshape(1, ATTN_W)
    gk_t = jnp.tile(k_norm_g.astype(f32), N_HEADS).reshape(1, ATTN_W)
    bias_tab = _bias_table(rpb)
    w_in_b = w_in.astype(bf16)
    w_out_b = w_out.astype(bf16)
    w_router_t = w_router.T.astype(f32)
    wg_b, wu_b, wd_b = w_gate.astype(bf16), w_up.astype(bf16), w_down.astype(bf16)

    x1s, h2s, idxs, gs, css, cbks = [], [], [], [], [], []
    cap = None
    for x, mod in zip(xs, mods):
        b, l, _ = x.shape
        n = b * l
        cap = max(1, EC_FACTOR * n // ne)
        q, k, v, u = _inproj(x, mod, row(norm1_g), w_in_b, seg, gq_t, gk_t)
        attn = _attention(q, k, v, bias_tab)
        conv = _conv(u, conv_w.astype(f32), row(conv_b), row(conv_ln_g), row(conv_ln_b), seg)
        x1, h2, aff_t = _outproj(attn, conv, x, mod, w_out_b, row(norm2_g), w_router_t)
        cs, tst, cbk = _route(aff_t, cap, TSC, TBF)
        idx, g = _compact(tst, cs, aff_t, cap, TSC, TBC)
        x1s.append(x1); h2s.append(h2); idxs.append(idx.reshape(-1)); gs.append(g)
        css.append(cs.reshape(ne, n)); cbks.append(cbk)

    xe = _gather(jnp.stack(idxs), h2s[0], h2s[1], cap, d // LANES)
    g_rows = jnp.concatenate([gs[0], gs[1]], axis=1)
    ye = _experts(xe, g_rows, wg_b, wu_b, wd_b, min(TM_EXPERT, cap), min(FC_EXPERT, w_gate.shape[2]))
    outs = [_combine(cbks[gi], css[gi], x1s[gi], mods[gi], ye, gi, cap, TBF) for gi in range(2)]
    return tuple(outs)
```

```python
import functools

import jax
import jax.numpy as jnp
from jax import lax
from jax.experimental import pallas as pl
from jax.experimental.pallas import tpu as pltpu

f32 = jnp.float32
bf16 = jnp.bfloat16
i32 = jnp.int32

GRID_W = 64
N_HEADS = 8
HEAD_DIM = 64
ATTN_W = N_HEADS * HEAD_DIM
CONV_W = 512
CONV_K = 31
CONV_GROUP = 64
WIN_ROWS = 8
WIN_COLS = 16
N_EXPERTS = 16
EC_FACTOR = 2
N_MOD = 6
EPS = 1e-6
NEG = -1e30

LANES = 128
VMEM_LIMIT = 56 * 1024 * 1024

NT_DIMS = (((1,), (1,)), ((), ()))


def _params(*sem):
    return pltpu.CompilerParams(dimension_semantics=sem, vmem_limit_bytes=VMEM_LIMIT)


def _split_bf16(x):
    hi = x.astype(bf16)
    lo = (x - hi.astype(f32)).astype(bf16)
    return hi, lo


def _ada_kernel(c_ref, w_ref, b_ref, o_ref):
    c = c_ref[...]
    s = c * jax.nn.sigmoid(c)
    s_hi, s_lo = _split_bf16(s)
    w_hi, w_lo = _split_bf16(w_ref[...])
    acc = jnp.dot(s_hi, w_hi, preferred_element_type=f32)
    acc += jnp.dot(s_hi, w_lo, preferred_element_type=f32)
    acc += jnp.dot(s_lo, w_hi, preferred_element_type=f32)
    o_ref[...] = acc + b_ref[...]


def _ada(c_all, ada_w, ada_b):
    rows, d = c_all.shape
    ncols = ada_w.shape[1]
    tn = next(t for t in (1024, 512, 256, 128) if ncols % t == 0)
    return pl.pallas_call(
        _ada_kernel,
        grid=(ncols // tn,),
        in_specs=[pl.BlockSpec((rows, d), lambda j: (0, 0)),
                  pl.BlockSpec((d, tn), lambda j: (0, j)),
                  pl.BlockSpec((1, tn), lambda j: (0, j))],
        out_specs=pl.BlockSpec((rows, tn), lambda j: (0, j)),
        out_shape=jax.ShapeDtypeStruct((rows, ncols), f32),
        compiler_params=_params("arbitrary"),
        name="ada",
    )(c_all, ada_w, ada_b.reshape(1, ncols))


def _inproj_kernel(x_ref, mod_ref, g1_ref, w_ref, seg_ref, gq_ref, gk_ref,
                   q_ref, k_ref, v_ref, u_ref):
    x = x_ref[...]
    d = x.shape[-1]
    ms = jnp.mean(x * x, axis=-1, keepdims=True)
    h = x * lax.rsqrt(ms + EPS) * g1_ref[...]
    shift = mod_ref[:, 0:d]
    scale = mod_ref[:, d:2 * d]
    h = h * (1.0 + scale) + shift
    z = jnp.dot(h.astype(bf16), w_ref[...], preferred_element_type=f32)
    zq = z[:, 0:ATTN_W]
    zk = z[:, ATTN_W:2 * ATTN_W]
    seg = seg_ref[...]
    ssq_q = jnp.dot((zq * zq).astype(bf16), seg, preferred_element_type=f32)
    ssq_k = jnp.dot((zk * zk).astype(bf16), seg, preferred_element_type=f32)
    qs = gq_ref[...] * gk_ref[...] * (HEAD_DIM ** -0.5)
    q_ref[...] = (zq * lax.rsqrt(ssq_q * (1.0 / HEAD_DIM) + EPS) * qs).astype(bf16)
    k_ref[...] = (zk * lax.rsqrt(ssq_k * (1.0 / HEAD_DIM) + EPS)).astype(bf16)
    v_ref[...] = z[:, 2 * ATTN_W:3 * ATTN_W].astype(bf16)
    a = z[:, 3 * ATTN_W:3 * ATTN_W + CONV_W]
    gt = z[:, 3 * ATTN_W + CONV_W:]
    u_ref[...] = a * jax.nn.sigmoid(gt)


def _inproj(x, mod, norm1_g, w_in_b, seg, gq_t, gk_t):
    b, l, d = x.shape
    tm = min(512, l)
    ncols = w_in_b.shape[1]
    tok = lambda w: pl.BlockSpec((None, tm, w), lambda bi, i: (bi, i, 0))
    full = lambda a: pl.BlockSpec(a.shape, lambda bi, i: (0,) * a.ndim)
    return pl.pallas_call(
        _inproj_kernel,
        grid=(b, l // tm),
        in_specs=[tok(d),
                  pl.BlockSpec((None, 1, mod.shape[-1]), lambda bi, i: (bi, 0, 0)),
                  full(norm1_g), full(w_in_b), full(seg), full(gq_t), full(gk_t)],
        out_specs=[tok(ATTN_W), tok(ATTN_W), tok(ATTN_W), tok(CONV_W)],
        out_shape=[jax.ShapeDtypeStruct((b, l, ATTN_W), bf16)] * 3
                  + [jax.ShapeDtypeStruct((b, l, CONV_W), f32)],
        compiler_params=_params("arbitrary", "arbitrary"),
        name="inproj",
    )(x, mod, norm1_g, w_in_b, seg, gq_t, gk_t)


def _attn_kernel(q_ref, k_ref, v_ref, bias_ref, o_ref, *, rows, rg, wr, nr):
    g = pl.program_id(1)
    lane = lax.broadcasted_iota(i32, (GRID_W, LANES), 1)
    first_head = lane < HEAD_DIM
    nkeys = wr * GRID_W

    pairs = [(rr, p) for rr in range(nr) for p in range(N_HEADS // 2)]

    def rows_body(it, carry):
        geo = []
        for rr in range(nr):
            lr = it * nr + rr
            r = g * rg + lr
            rs = jnp.clip(r - wr // 2, 0, rows - wr)
            geo.append((pl.multiple_of(lr * GRID_W, GRID_W),
                        pl.multiple_of(rs * GRID_W, GRID_W),
                        rs - r + (WIN_ROWS - 1)))
        scores = []
        for rr, p in pairs:
            q0, tok0, rb = geo[rr]
            cols = slice(p * LANES, (p + 1) * LANES)
            qp = q_ref[pl.ds(q0, GRID_W), cols]
            kp = k_ref[pl.ds(tok0, nkeys), cols]
            zero = jnp.zeros_like(qp)
            q2 = jnp.concatenate([jnp.where(first_head, qp, zero),
                                  jnp.where(first_head, zero, qp)], axis=0)
            s = lax.dot_general(q2, kp, NT_DIMS, preferred_element_type=f32)
            bias = jnp.concatenate(
                [jnp.concatenate([bias_ref[2 * p + j, rb + 2 * m] for m in range(nkeys // LANES)],
                                 axis=-1) for j in range(2)], axis=0)
            scores.append(s + bias)
        probs = []
        for s in scores:
            mx = jnp.max(s, axis=-1, keepdims=True)
            e = jnp.exp(s - mx)
            probs.append((e.astype(bf16), jnp.sum(e, axis=-1, keepdims=True)))
        for (rr, p), (e, den) in zip(pairs, probs):
            q0, tok0, rb = geo[rr]
            cols = slice(p * LANES, (p + 1) * LANES)
            vp = v_ref[pl.ds(tok0, nkeys), cols]
            o = jnp.dot(e, vp, preferred_element_type=f32) / den
            o_ref[pl.ds(q0, GRID_W), cols] = jnp.where(
                first_head, o[0:GRID_W], o[GRID_W:]).astype(bf16)
        return carry

    lax.fori_loop(0, rg // nr, rows_body, 0)


def _attention(q, k, v, bias_tab):
    b, l, _ = q.shape
    rows = l // GRID_W
    wr = min(WIN_ROWS, rows)
    rg = min(8, rows)
    once = pl.Buffered(1)
    kern = functools.partial(_attn_kernel, rows=rows, rg=rg, wr=wr, nr=2)
    return pl.pallas_call(
        kern,
        grid=(b, rows // rg),
        in_specs=[pl.BlockSpec((None, rg * GRID_W, ATTN_W), lambda bi, g: (bi, g, 0)),
                  pl.BlockSpec((None, l, ATTN_W), lambda bi, g: (bi, 0, 0), pipeline_mode=once),
                  pl.BlockSpec((None, l, ATTN_W), lambda bi, g: (bi, 0, 0), pipeline_mode=once),
                  pl.BlockSpec(bias_tab.shape, lambda bi, g: (0, 0, 0, 0), pipeline_mode=once)],
        out_specs=pl.BlockSpec((None, rg * GRID_W, ATTN_W), lambda bi, g: (bi, g, 0)),
        out_shape=jax.ShapeDtypeStruct((b, l, ATTN_W), bf16),
        compiler_params=_params("arbitrary", "arbitrary"),
        name="attention",
    )(q, k, v, bias_tab)


def _bias_table(rpb):
    cols = jnp.arange(GRID_W)
    col_start = jnp.clip(cols - WIN_COLS // 2, 0, GRID_W - WIN_COLS)
    kc = jnp.arange(GRID_W)[None, :]
    valid = (kc >= col_start[:, None]) & (kc < col_start[:, None] + WIN_COLS)
    off = jnp.clip(kc - cols[:, None] + (WIN_COLS - 1), 0, 2 * WIN_COLS - 2)
    tab = jnp.where(valid[None, None], rpb[:, :, off].astype(f32), NEG)
    return jnp.concatenate([tab[:, :-1], tab[:, 1:]], axis=-1)


def _conv_kernel(u_ref, prev_ref, next_ref, w_ref, cb_ref, lg_ref, lb_ref, seg_ref, o_ref,
                 win_ref, sh_ref, *, tl, chunk):
    i = pl.program_id(1)
    n_i = pl.num_programs(1)
    halo = prev_ref.shape[0]
    sub = 8
    win_ref[0:halo, :] = jnp.where(i > 0, prev_ref[...], 0.0)
    win_ref[halo:halo + tl, :] = u_ref[...]
    win_ref[halo + tl:, :] = jnp.where(i < n_i - 1, next_ref[...], 0.0)
    sh_rows = sh_ref.shape[1]
    for r in range(1, sub):
        for k0 in range(0, sh_rows, chunk):
            k1 = min(k0 + chunk, sh_rows)
            sh_ref[r - 1, k0:k1, :] = win_ref[r + k0:r + k1, :]
    seg = seg_ref[...]
    pad = CONV_K // 2
    for c in range(tl // chunk):
        acc = jnp.zeros((chunk, CONV_W), f32)
        for j in range(CONV_K):
            start = halo - pad + c * chunk + j
            r = start % sub
            src = win_ref if r == 0 else sh_ref.at[r - 1]
            acc = acc + src[start - r:start - r + chunk, :] * w_ref[j:j + 1, :]
        y = acc + cb_ref[...]
        y_hi, y_lo = _split_bf16(y)
        mu = (jnp.dot(y_hi, seg, preferred_element_type=f32)
              + jnp.dot(y_lo, seg, preferred_element_type=f32)) * (1.0 / CONV_GROUP)
        dlt = y - mu
        d_hi, d_lo = _split_bf16(dlt * dlt)
        var = (jnp.dot(d_hi, seg, preferred_element_type=f32)
               + jnp.dot(d_lo, seg, preferred_element_type=f32)) * (1.0 / CONV_GROUP)
        un = dlt * lax.rsqrt(var + EPS) * lg_ref[...] + lb_ref[...]
        o_ref[c * chunk:(c + 1) * chunk, :] = (un * jax.nn.sigmoid(un)).astype(bf16)


def _conv(u, conv_w, conv_b, ln_g, ln_b, seg):
    b, l, _ = u.shape
    tl = min(512, l)
    halo = 16
    nh = l // halo
    kern = functools.partial(_conv_kernel, tl=tl, chunk=min(64, tl))
    full = lambda a: pl.BlockSpec(a.shape, lambda bi, i: (0,) * a.ndim)
    return pl.pallas_call(
        kern,
        grid=(b, l // tl),
        in_specs=[pl.BlockSpec((None, tl, CONV_W), lambda bi, i: (bi, i, 0)),
                  pl.BlockSpec((None, halo, CONV_W),
                               lambda bi, i: (bi, jnp.maximum(i * (tl // halo) - 1, 0), 0)),
                  pl.BlockSpec((None, halo, CONV_W),
                               lambda bi, i: (bi, jnp.minimum((i + 1) * (tl // halo), nh - 1), 0)),
                  full(conv_w), full(conv_b), full(ln_g), full(ln_b), full(seg)],
        out_specs=pl.BlockSpec((None, tl, CONV_W), lambda bi, i: (bi, i, 0)),
        out_shape=jax.ShapeDtypeStruct((b, l, CONV_W), bf16),
        scratch_shapes=[pltpu.VMEM((tl + 2 * halo, CONV_W), f32),
                        pltpu.VMEM((7, tl + 2 * halo - 8, CONV_W), f32)],
        compiler_params=_params("arbitrary", "arbitrary"),
        name="conv",
    )(u, u, u, conv_w, conv_b, ln_g, ln_b, seg)


def _outproj_kernel(at_ref, cv_ref, x_ref, mod_ref, wo_ref, g2_ref, wr_ref,
                    x1_ref, h2_ref, aff_ref):
    x = x_ref[...]
    d = x.shape[-1]
    mix = jnp.dot(at_ref[...], wo_ref[0:ATTN_W, :], preferred_element_type=f32)
    mix += jnp.dot(cv_ref[...], wo_ref[ATTN_W:, :], preferred_element_type=f32)
    gate1 = mod_ref[:, 2 * d:3 * d]
    shift2 = mod_ref[:, 3 * d:4 * d]
    scale2 = mod_ref[:, 4 * d:5 * d]
    x1 = x + gate1 * mix
    x1_ref[...] = x1
    ms = jnp.mean(x1 * x1, axis=-1, keepdims=True)
    h2 = x1 * lax.rsqrt(ms + EPS) * g2_ref[...]
    h2 = h2 * (1.0 + scale2) + shift2
    h_hi, h_lo = _split_bf16(h2)
    tm = x.shape[0]
    h_rows = h_hi.astype(f32)
    for c in range(d // LANES):
        h2_ref[pl.ds(c, tm, stride=d // LANES), :] = h_rows[:, c * LANES:(c + 1) * LANES]
    w_hi, w_lo = _split_bf16(wr_ref[...])
    logits = lax.dot_general(w_hi, h_hi, NT_DIMS, preferred_element_type=f32)
    logits += lax.dot_general(w_hi, h_lo, NT_DIMS, preferred_element_type=f32)
    logits += lax.dot_general(w_lo, h_hi, NT_DIMS, preferred_element_type=f32)
    mx = jnp.max(logits, axis=0, keepdims=True)
    ex = jnp.exp(logits - mx)
    aff_ref[...] = ex / jnp.sum(ex, axis=0, keepdims=True)


def _outproj(attn, conv, x, mod, w_out_b, norm2_g, w_router_t):
    b, l, d = x.shape
    tm = min(512, l)
    nt = l // tm
    e = w_router_t.shape[0]
    tok = lambda w: pl.BlockSpec((None, tm, w), lambda bi, i: (bi, i, 0))
    full = lambda a: pl.BlockSpec(a.shape, lambda bi, i: (0,) * a.ndim)
    return pl.pallas_call(
        _outproj_kernel,
        grid=(b, nt),
        in_specs=[tok(ATTN_W), tok(CONV_W), tok(d),
                  pl.BlockSpec((None, 1, mod.shape[-1]), lambda bi, i: (bi, 0, 0)),
                  full(w_out_b), full(norm2_g), full(w_router_t)],
        out_specs=[tok(d),
                   pl.BlockSpec((tm * (d // LANES), LANES), lambda bi, i: (bi * nt + i, 0)),
                   pl.BlockSpec((e, tm), lambda bi, i: (0, bi * nt + i))],
        out_shape=[jax.ShapeDtypeStruct((b, l, d), f32),
                   jax.ShapeDtypeStruct((b * l * (d // LANES), LANES), f32),
                   jax.ShapeDtypeStruct((e, b * l), f32)],
        compiler_params=_params("arbitrary", "arbitrary"),
        name="outproj",
    )(attn, conv, x, mod, w_out_b, norm2_g, w_router_t)


def _route_kernel(aff_ref, cs_ref, tst_ref, cbk_ref, *, cap, tsc, tbc):
    a = aff_ref[...]
    ne, nc, _ = a.shape

    def enough(v):
        return jnp.sum((a >= v).astype(f32), axis=(1, 2), keepdims=True) >= cap

    def pow2(k):
        return lax.bitcast_convert_type(jnp.left_shift(k + 127, 23), f32)

    def exp_body(_, st):
        klo, khi = st
        kmid = jnp.right_shift(klo + khi, 1)
        ok = enough(pow2(kmid))
        return jnp.where(ok, kmid, klo), jnp.where(ok, khi, kmid)

    klo, khi = lax.fori_loop(0, 7, exp_body,
                             (jnp.full((ne, 1, 1), -127, i32), jnp.full((ne, 1, 1), 1, i32)))

    def lin_body(_, st):
        lo, hi = st
        mid = lo + (hi - lo) * 0.5
        ok = enough(mid)
        return jnp.where(ok, mid, lo), jnp.where(ok, hi, mid)

    lo, _ = lax.fori_loop(0, 48, lin_body, (pow2(klo), pow2(khi)))
    thr = jnp.min(jnp.where(a >= lo, a, jnp.inf), axis=(1, 2), keepdims=True)

    r_i = lax.broadcasted_iota(i32, (LANES, LANES), 0)
    c_i = lax.broadcasted_iota(i32, (LANES, LANES), 1)
    upper = (r_i <= c_i).astype(bf16)
    ones = jnp.ones((LANES, LANES), bf16)
    rr = lax.broadcasted_iota(i32, (nc, nc), 0)
    cc = lax.broadcasted_iota(i32, (nc, nc), 1)
    lower = (cc < rr).astype(bf16)
    blk_r = lax.broadcasted_iota(i32, (nc, LANES), 0)
    blk_c = lax.broadcasted_iota(i32, (nc, LANES), 1)
    before_blk = (blk_r < blk_c * (tbc // LANES)).astype(bf16)
    ones8 = jnp.ones((8, LANES), bf16)
    lane8 = lax.broadcasted_iota(i32, (1, LANES), 1)

    def cumsum(m):
        within = jnp.dot(m, upper, preferred_element_type=f32)
        prev = jnp.dot(lower, m, preferred_element_type=f32)
        offs = jnp.dot(prev.astype(bf16), ones, preferred_element_type=f32)
        return within + offs

    for e in range(ne):
        a_e = a[e]
        thr_e = thr[e]
        gt_e = a_e > thr_e
        eq_e = (a_e == thr_e).astype(f32)
        need = cap - jnp.sum(gt_e.astype(f32), keepdims=True)
        rank_eq = cumsum(eq_e.astype(bf16)) - eq_e
        sel = jnp.logical_or(gt_e, jnp.logical_and(eq_e > 0.0, rank_eq < need))
        sel_b = sel.astype(bf16)
        c_incl = cumsum(sel_b)
        cs_ref[e] = jnp.where(sel, c_incl, 0.0).astype(i32)
        tst = jnp.zeros((1, LANES), f32)
        for j in range(cap // tsc + 1):
            cnt_j = jnp.sum((c_incl <= float(j * tsc)).astype(f32), keepdims=True)
            tst = jnp.where(lane8 == j, cnt_j, tst)
        tst_ref[e:e + 1, :] = tst.astype(i32)
        per_chunk = lax.dot_general(ones8, sel_b, NT_DIMS, preferred_element_type=f32)
        cbk = jnp.dot(per_chunk.astype(bf16), before_blk, preferred_element_type=f32)
        cbk_ref[e:e + 1, :] = cbk[0:1].astype(i32)


def _route(aff_t, cap, tsc, tbc):
    e, n = aff_t.shape
    nc = n // LANES
    kern = functools.partial(_route_kernel, cap=cap, tsc=tsc, tbc=tbc)
    return pl.pallas_call(
        kern,
        out_shape=[jax.ShapeDtypeStruct((e, nc, LANES), i32),
                   jax.ShapeDtypeStruct((e, LANES), i32),
                   jax.ShapeDtypeStruct((e, LANES), i32)],
        compiler_params=pltpu.CompilerParams(vmem_limit_bytes=VMEM_LIMIT),
        name="route",
    )(aff_t.reshape(e, nc, LANES))


def _compact_kernel(tst_ref, cs_ref, aff_ref, idx_ref, g_ref, *, tsc, tbc):
    e = pl.program_id(0)
    j = pl.program_id(1)
    shift = tbc.bit_length() - 1
    f_lo = tst_ref[e * LANES + j]
    f_hi = tst_ref[e * LANES + j + 1]
    b_lo = jnp.right_shift(f_lo, shift)
    b_hi = jnp.right_shift(f_hi - 1, shift)
    slot1 = j * tsc + 1 + lax.broadcasted_iota(i32, (tsc, tbc), 0)
    sub = lax.broadcasted_iota(i32, (8, tbc), 0)
    lane = lax.broadcasted_iota(i32, (8, tbc), 1)

    def body(b, acc):
        row = cs_ref[e, pl.ds(b, 1), :]
        a = aff_ref[e, pl.ds(b, 1), :]
        onehot_t = (row == slot1).astype(f32).astype(bf16)
        t = b * tbc + lane
        a_hi = a.astype(bf16).astype(f32)
        r1 = a - a_hi
        a_mid = r1.astype(bf16).astype(f32)
        a_lo = r1 - a_mid
        pay = jnp.where(sub == 0, jnp.bitwise_and(t, 255).astype(f32),
              jnp.where(sub == 1, jnp.right_shift(t, 8).astype(f32),
              jnp.where(sub == 2, a_hi,
              jnp.where(sub == 3, a_mid,
              jnp.where(sub == 4, a_lo, 0.0)))))
        return acc + lax.dot_general(pay.astype(bf16), onehot_t, NT_DIMS, preferred_element_type=f32)

    acc = lax.fori_loop(b_lo, b_hi + 1, body, jnp.zeros((8, tsc), f32))
    idx = acc[1:2] * 256.0 + acc[0:1]
    idx_ref[...] = jnp.broadcast_to(idx, (8, tsc)).astype(i32)
    g_ref[...] = jnp.broadcast_to(acc[2:3] + acc[3:4] + acc[4:5], (8, tsc))


def _compact(tst, cs, aff_t, cap, tsc, tbc):
    e, n = aff_t.shape
    nb = n // tbc
    ntile = cap // tsc
    kern = functools.partial(_compact_kernel, tsc=tsc, tbc=tbc)
    res = pl.BlockSpec((e, nb, tbc), lambda ei, j, tst_r: (0, 0, 0))
    out = pl.BlockSpec((None, None, 8, tsc), lambda ei, j, tst_r: (ei, j, 0, 0))
    idx8, g8 = pl.pallas_call(
        kern,
        grid_spec=pltpu.PrefetchScalarGridSpec(
            num_scalar_prefetch=1, grid=(e, ntile), in_specs=[res, res], out_specs=[out, out]),
        out_shape=[jax.ShapeDtypeStruct((e, ntile, 8, tsc), i32),
                   jax.ShapeDtypeStruct((e, ntile, 8, tsc), f32)],
        compiler_params=_params("arbitrary", "arbitrary"),
        name="compact",
    )(tst.reshape(-1), cs.reshape(e, nb, tbc), aff_t.reshape(e, nb, tbc))
    return idx8[:, :, 0, :].reshape(e, cap), g8[:, :, 0, :].reshape(e, cap)


def _gather_kernel(idx_ref, h0_ref, h1_ref, xe_ref, sem, *, rb, steps_per_group, rpt):
    grp = pl.program_id(0) // steps_per_group

    def token_rows(t):
        return pl.ds(pl.multiple_of(t * rpt, rpt), rpt)

    def issue(src_ref):
        def body(r, carry):
            tok = idx_ref[0, r]
            pltpu.make_async_copy(src_ref.at[token_rows(tok)], xe_ref.at[token_rows(r)], sem).start()
            return carry
        lax.fori_loop(0, rb, body, 0, unroll=8)

    @pl.when(grp == 0)
    def _():
        issue(h0_ref)

    @pl.when(grp == 1)
    def _():
        issue(h1_ref)

    pltpu.make_async_copy(h0_ref.at[pl.ds(0, rb * rpt)], xe_ref, sem).wait()


def _gather(idx_all, h0, h1, cap, rpt):
    d = h0.shape[-1]
    total = idx_all.shape[1]
    rb = min(512, cap)
    steps_per_group = total // rb
    kern = functools.partial(_gather_kernel, rb=rb, steps_per_group=steps_per_group, rpt=rpt)
    per_cap = cap // rb

    def dest_block(i):
        grp = i // steps_per_group
        li = i - grp * steps_per_group
        e = li // per_cap
        return (e * 2 * per_cap + grp * per_cap + (li - e * per_cap), 0)

    return pl.pallas_call(
        kern,
        grid=(2 * steps_per_group,),
        in_specs=[pl.BlockSpec((None, 1, rb), lambda i: (i, 0, 0), memory_space=pltpu.SMEM),
                  pl.BlockSpec(memory_space=pl.ANY),
                  pl.BlockSpec(memory_space=pl.ANY)],
        out_specs=pl.BlockSpec((rb * rpt, d), dest_block),
        out_shape=jax.ShapeDtypeStruct((2 * total * rpt, d), f32),
        scratch_shapes=[pltpu.SemaphoreType.DMA],
        compiler_params=_params("arbitrary"),
        name="gather",
    )(idx_all.reshape(2 * steps_per_group, 1, rb), h0, h1)


def _expert_kernel(x_ref, g_ref, wg_ref, wu_ref, wd_ref, y_ref, xb_ref, acc_ref, gcol_ref):
    f = pl.program_id(2)

    @pl.when(f == 0)
    def _():
        tm, d = xb_ref.shape
        rpt = d // LANES
        for c in range(rpt):
            xb_ref[:, c * LANES:(c + 1) * LANES] = x_ref[pl.ds(c, tm, stride=rpt), :].astype(bf16)
        acc_ref[...] = jnp.zeros_like(acc_ref)
        gcol_ref[...] = jnp.broadcast_to(g_ref[...], (LANES, g_ref.shape[-1])).T

    xb = xb_ref[...]
    hg = jnp.dot(xb, wg_ref[...], preferred_element_type=f32)
    hu = jnp.dot(xb, wu_ref[...], preferred_element_type=f32)
    hid = (hg * jax.nn.sigmoid(hg) * hu).astype(bf16)
    acc_ref[...] += jnp.dot(hid, wd_ref[...], preferred_element_type=f32)

    @pl.when(f == pl.num_programs(2) - 1)
    def _():
        gcol = gcol_ref[...]
        d = acc_ref.shape[-1]
        for c in range(d // LANES):
            cols = slice(c * LANES, (c + 1) * LANES)
            y_ref[:, cols] = acc_ref[:, cols] * gcol


def _experts(xe, g_rows, wg_b, wu_b, wd_b, tm, fc):
    e, d, fdim = wg_b.shape
    rpt = d // LANES
    rows = xe.shape[0] // rpt
    per_e = rows // e
    nt = per_e // tm
    return pl.pallas_call(
        _expert_kernel,
        grid=(e, nt, fdim // fc),
        in_specs=[pl.BlockSpec((tm * rpt, LANES), lambda ei, t, f: (ei * nt + t, 0)),
                  pl.BlockSpec((None, 1, tm), lambda ei, t, f: (ei * nt + t, 0, 0)),
                  pl.BlockSpec((None, d, fc), lambda ei, t, f: (ei, 0, f)),
                  pl.BlockSpec((None, d, fc), lambda ei, t, f: (ei, 0, f)),
                  pl.BlockSpec((None, fc, d), lambda ei, t, f: (ei, f, 0))],
        out_specs=pl.BlockSpec((tm, d), lambda ei, t, f: (ei * nt + t, 0)),
        out_shape=jax.ShapeDtypeStruct((rows, d), f32),
        scratch_shapes=[pltpu.VMEM((tm, d), bf16), pltpu.VMEM((tm, d), f32),
                        pltpu.VMEM((tm, LANES), f32)],
        compiler_params=_params("arbitrary", "arbitrary", "arbitrary"),
        name="experts",
    )(xe, g_rows.reshape(rows // tm, 1, tm), wg_b, wu_b, wd_b)


def _combine_kernel(cbk_ref, cs_ref, x1_ref, mod_ref, ye_ref, o_ref, win_ref, xwin_ref, sem, xsem,
                    acc_ref, *, grp, cap, tbc, win, ne, total_rows):
    blk = pl.program_id(0)
    nblk = pl.num_programs(0)
    d = acc_ref.shape[-1]
    slot = blk % 2

    def region(ee):
        return ee * 2 * cap + grp * cap

    def first_row(bb, ee):
        st = cbk_ref[ee * LANES + bb]
        row = region(ee) + jnp.left_shift(jnp.right_shift(st, 3), 3)
        return pl.multiple_of(jnp.minimum(row, total_rows - win), 8)

    def start_block(bb, sl):
        for ee in range(ne):
            pltpu.make_async_copy(ye_ref.at[pl.ds(first_row(bb, ee), win)],
                                  win_ref.at[sl, pl.ds(ee * win, win)], sem.at[sl]).start()

    @pl.when(blk == 0)
    def _():
        start_block(blk, slot)

    @pl.when(blk + 1 < nblk)
    def _():
        start_block(blk + 1, (blk + 1) % 2)

    pltpu.make_async_copy(ye_ref.at[pl.ds(0, ne * win)], win_ref.at[slot], sem.at[slot]).wait()

    w_iota = lax.broadcasted_iota(i32, (win, tbc), 0)

    def onehot_of(cs_row, base_slot, done_slots):
        hit = jnp.logical_and(cs_row == base_slot + 1 + w_iota, cs_row > done_slots)
        return hit.astype(f32).T.astype(bf16)

    onehot = jnp.concatenate(
        [onehot_of(cs_ref[ee:ee + 1, :], first_row(blk, ee) - region(ee), 0) for ee in range(ne)],
        axis=1)
    acc_ref[...] = jnp.dot(onehot, win_ref[slot].astype(bf16), preferred_element_type=f32)

    def extra_expert(ee, carry):
        row0 = first_row(blk, ee)
        base0 = row0 - region(ee)
        end = cbk_ref[ee * LANES + blk + 1]
        cs_row = cs_ref[pl.ds(ee, 1), :]

        def extra(w, c2):
            row_w = pl.multiple_of(jnp.minimum(row0 + w * win, total_rows - win), 8)
            cp = pltpu.make_async_copy(ye_ref.at[pl.ds(row_w, win)], xwin_ref, xsem)
            cp.start()
            cp.wait()
            oh = onehot_of(cs_row, row_w - region(ee), base0 + w * win)
            acc_ref[...] += jnp.dot(oh, xwin_ref[...].astype(bf16), preferred_element_type=f32)
            return c2

        lax.fori_loop(1, (end - base0 + win - 1) // win, extra, 0)
        return carry

    lax.fori_loop(0, ne, extra_expert, 0)

    gate2 = mod_ref[:, 5 * d:6 * d]
    o_ref[...] = x1_ref[...] + gate2 * acc_ref[...]


def _combine(cbk, cs, x1, mod, ye, grp, cap, tbc):
    b, l, d = x1.shape
    n = b * l
    e = cs.shape[0]
    nb = n // tbc
    win = min(128, cap)
    per_b = l // tbc
    kern = functools.partial(_combine_kernel, grp=grp, cap=cap, tbc=tbc, win=win, ne=e,
                             total_rows=ye.shape[0])
    return pl.pallas_call(
        kern,
        grid_spec=pltpu.PrefetchScalarGridSpec(
            num_scalar_prefetch=1, grid=(nb,),
            in_specs=[pl.BlockSpec((e, tbc), lambda bk, c: (0, bk)),
                      pl.BlockSpec((tbc, d), lambda bk, c: (bk, 0)),
                      pl.BlockSpec((None, 1, mod.shape[-1]), lambda bk, c: (bk // per_b, 0, 0)),
                      pl.BlockSpec(memory_space=pl.ANY)],
            out_specs=pl.BlockSpec((tbc, d), lambda bk, c: (bk, 0)),
            scratch_shapes=[pltpu.VMEM((2, e * win, d), f32), pltpu.VMEM((win, d), f32),
                            pltpu.SemaphoreType.DMA((2,)), pltpu.SemaphoreType.DMA,
                            pltpu.VMEM((tbc, d), f32)]),
        out_shape=jax.ShapeDtypeStruct((n, d), f32),
        compiler_params=_params("arbitrary"),
        name="combine",
    )(cbk.reshape(-1), cs, x1.reshape(n, d), mod, ye).reshape(b, l, d)


TSC = 256
TBC = 256
TBF = 512
TM_EXPERT = 1024
FC_EXPERT = 256


def kernel(x_prompt, x_sample, c_prompt, c_sample, ada_w, ada_b, norm1_g, w_in, q_norm_g, k_norm_g, rpb,
           conv_w, conv_b, conv_ln_g, conv_ln_b, w_out, norm2_g, w_router, w_gate, w_up, w_down):
    xs = (x_prompt, x_sample)
    d = x_prompt.shape[-1]
    ne = w_router.shape[1]
    row = lambda a: a.reshape(1, -1).astype(f32)

    nb0, nb1 = c_prompt.shape[0], c_sample.shape[0]
    pad = (-(nb0 + nb1)) % 8
    c_all = jnp.concatenate([c_prompt, c_sample, jnp.zeros((pad, d), f32)], axis=0)
    mod_all = _ada(c_all, ada_w, ada_b)
    mods = (mod_all[:nb0].reshape(nb0, 1, -1), mod_all[nb0:nb0 + nb1].reshape(nb1, 1, -1))

    seg_i = jnp.arange(ATTN_W) // HEAD_DIM
    seg = (seg_i[:, None] == seg_i[None, :]).astype(bf16)
    gq_t = jnp.tile(q_norm_g.astype(f32), N_HEADS).reshape(1, ATTN_W)
    gk_t = jnp.tile(k_norm_g.astype(f32), N_HEADS).reshape(1, ATTN_W)
    bias_tab = _bias_table(rpb)
    w_in_b = w_in.astype(bf16)
    w_out_b = w_out.astype(bf16)
    w_router_t = w_router.T.astype(f32)
    wg_b, wu_b, wd_b = w_gate.astype(bf16), w_up.astype(bf16), w_down.astype(bf16)

    x1s, h2s, idxs, gs, css, cbks = [], [], [], [], [], []
    cap = None
    for x, mod in zip(xs, mods):
        b, l, _ = x.shape
        n = b * l
        cap = max(1, EC_FACTOR * n // ne)
        q, k, v, u = _inproj(x, mod, row(norm1_g), w_in_b, seg, gq_t, gk_t)
        attn = _attention(q, k, v, bias_tab)
        conv = _conv(u, conv_w.astype(f32), row(conv_b), row(conv_ln_g), row(conv_ln_b), seg)
        x1, h2, aff_t = _outproj(attn, conv, x, mod, w_out_b, row(norm2_g), w_router_t)
        cs, tst, cbk = _route(aff_t, cap, TSC, TBF)
        idx, g = _compact(tst, cs, aff_t, cap, TSC, TBC)
        x1s.append(x1); h2s.append(h2); idxs.append(idx.reshape(-1)); gs.append(g)
        css.append(cs.reshape(ne, n)); cbks.append(cbk)

    xe = _gather(jnp.stack(idxs), h2s[0], h2s[1], cap, d // LANES)
    g_rows = jnp.concatenate([gs[0], gs[1]], axis=1)
    ye = _experts(xe, g_rows, wg_b, wu_b, wd_b, min(TM_EXPERT, cap), min(FC_EXPERT, w_gate.shape[2]))
    outs = [_combine(cbks[gi], css[gi], x1s[gi], mods[gi], ye, gi, cap, TBF) for gi in range(2)]
    return tuple(outs)
```

```python
import functools

import jax
import jax.numpy as jnp
from jax import lax
from jax.experimental import pallas as pl
from jax.experimental.pallas import tpu as pltpu

f32 = jnp.float32
bf16 = jnp.bfloat16
i32 = jnp.int32

GRID_W = 64
N_HEADS = 8
HEAD_DIM = 64
ATTN_W = N_HEADS * HEAD_DIM
CONV_W = 512
CONV_K = 31
CONV_GROUP = 64
WIN_ROWS = 8
WIN_COLS = 16
N_EXPERTS = 16
EC_FACTOR = 2
N_MOD = 6
EPS = 1e-6
NEG = -1e30

LANES = 128
VMEM_LIMIT = 56 * 1024 * 1024

NT_DIMS = (((1,), (1,)), ((), ()))


def _params(*sem):
    return pltpu.CompilerParams(dimension_semantics=sem, vmem_limit_bytes=VMEM_LIMIT)


def _split_bf16(x):
    hi = x.astype(bf16)
    lo = (x - hi.astype(f32)).astype(bf16)
    return hi, lo


def _ada_kernel(c_ref, w_ref, b_ref, o_ref):
    c = c_ref[...]
    s = c * jax.nn.sigmoid(c)
    s_hi, s_lo = _split_bf16(s)
    w_hi, w_lo = _split_bf16(w_ref[...])
    acc = jnp.dot(s_hi, w_hi, preferred_element_type=f32)
    acc += jnp.dot(s_hi, w_lo, preferred_element_type=f32)
    acc += jnp.dot(s_lo, w_hi, preferred_element_type=f32)
    o_ref[...] = acc + b_ref[...]


def _ada(c_all, ada_w, ada_b):
    rows, d = c_all.shape
    ncols = ada_w.shape[1]
    tn = next(t for t in (1024, 512, 256, 128) if ncols % t == 0)
    return pl.pallas_call(
        _ada_kernel,
        grid=(ncols // tn,),
        in_specs=[pl.BlockSpec((rows, d), lambda j: (0, 0)),
                  pl.BlockSpec((d, tn), lambda j: (0, j)),
                  pl.BlockSpec((1, tn), lambda j: (0, j))],
        out_specs=pl.BlockSpec((rows, tn), lambda j: (0, j)),
        out_shape=jax.ShapeDtypeStruct((rows, ncols), f32),
        compiler_params=_params("arbitrary"),
        name="ada",
    )(c_all, ada_w, ada_b.reshape(1, ncols))


def _inproj_kernel(x_ref, mod_ref, g1_ref, w_ref, seg_ref, gq_ref, gk_ref,
                   q_ref, k_ref, v_ref, u_ref):
    x = x_ref[...]
    d = x.shape[-1]
    ms = jnp.mean(x * x, axis=-1, keepdims=True)
    h = x * lax.rsqrt(ms + EPS) * g1_ref[...]
    shift = mod_ref[:, 0:d]
    scale = mod_ref[:, d:2 * d]
    h = h * (1.0 + scale) + shift
    z = jnp.dot(h.astype(bf16), w_ref[...], preferred_element_type=f32)
    zq = z[:, 0:ATTN_W]
    zk = z[:, ATTN_W:2 * ATTN_W]
    seg = seg_ref[...]
    ssq_q = jnp.dot((zq * zq).astype(bf16), seg, preferred_element_type=f32)
    ssq_k = jnp.dot((zk * zk).astype(bf16), seg, preferred_element_type=f32)
    qs = gq_ref[...] * gk_ref[...] * (HEAD_DIM ** -0.5)
    q_ref[...] = (zq * lax.rsqrt(ssq_q * (1.0 / HEAD_DIM) + EPS) * qs).astype(bf16)
    k_ref[...] = (zk * lax.rsqrt(ssq_k * (1.0 / HEAD_DIM) + EPS)).astype(bf16)
    v_ref[...] = z[:, 2 * ATTN_W:3 * ATTN_W].astype(bf16)
    a = z[:, 3 * ATTN_W:3 * ATTN_W + CONV_W]
    gt = z[:, 3 * ATTN_W + CONV_W:]
    u_ref[...] = a * jax.nn.sigmoid(gt)


def _inproj(x, mod, norm1_g, w_in_b, seg, gq_t, gk_t):
    b, l, d = x.shape
    tm = min(512, l)
    ncols = w_in_b.shape[1]
    tok = lambda w: pl.BlockSpec((None, tm, w), lambda bi, i: (bi, i, 0))
    full = lambda a: pl.BlockSpec(a.shape, lambda bi, i: (0,) * a.ndim)
    return pl.pallas_call(
        _inproj_kernel,
        grid=(b, l // tm),
        in_specs=[tok(d),
                  pl.BlockSpec((None, 1, mod.shape[-1]), lambda bi, i: (bi, 0, 0)),
                  full(norm1_g), full(w_in_b), full(seg), full(gq_t), full(gk_t)],
        out_specs=[tok(ATTN_W), tok(ATTN_W), tok(ATTN_W), tok(CONV_W)],
        out_shape=[jax.ShapeDtypeStruct((b, l, ATTN_W), bf16)] * 3
                  + [jax.ShapeDtypeStruct((b, l, CONV_W), f32)],
        compiler_params=_params("arbitrary", "arbitrary"),
        name="inproj",
    )(x, mod, norm1_g, w_in_b, seg, gq_t, gk_t)


def _attn_kernel(q_ref, k_ref, v_ref, bias_ref, o_ref, *, rows, rg, wr, nr):
    g = pl.program_id(1)
    lane = lax.broadcasted_iota(i32, (GRID_W, LANES), 1)
    first_head = lane < HEAD_DIM
    nkeys = wr * GRID_W

    pairs = [(rr, p) for rr in range(nr) for p in range(N_HEADS // 2)]

    def rows_body(it, carry):
        geo = []
        for rr in range(nr):
            lr = it * nr + rr
            r = g * rg + lr
            rs = jnp.clip(r - wr // 2, 0, rows - wr)
            geo.append((pl.multiple_of(lr * GRID_W, GRID_W),
                        pl.multiple_of(rs * GRID_W, GRID_W),
                        rs - r + (WIN_ROWS - 1)))
        scores = []
        for rr, p in pairs:
            q0, tok0, rb = geo[rr]
            cols = slice(p * LANES, (p + 1) * LANES)
            qp = q_ref[pl.ds(q0, GRID_W), cols]
            kp = k_ref[pl.ds(tok0, nkeys), cols]
            zero = jnp.zeros_like(qp)
            q2 = jnp.concatenate([jnp.where(first_head, qp, zero),
                                  jnp.where(first_head, zero, qp)], axis=0)
            s = lax.dot_general(q2, kp, NT_DIMS, preferred_element_type=f32)
            bias = jnp.concatenate(
                [jnp.concatenate([bias_ref[2 * p + j, rb + 2 * m] for m in range(nkeys // LANES)],
                                 axis=-1) for j in range(2)], axis=0)
            scores.append(s + bias)
        probs = []
        for s in scores:
            mx = jnp.max(s, axis=-1, keepdims=True)
            e = jnp.exp(s - mx)
            probs.append((e.astype(bf16), jnp.sum(e, axis=-1, keepdims=True)))
        for (rr, p), (e, den) in zip(pairs, probs):
            q0, tok0, rb = geo[rr]
            cols = slice(p * LANES, (p + 1) * LANES)
            vp = v_ref[pl.ds(tok0, nkeys), cols]
            o = jnp.dot(e, vp, preferred_element_type=f32) / den
            o_ref[pl.ds(q0, GRID_W), cols] = jnp.where(
                first_head, o[0:GRID_W], o[GRID_W:]).astype(bf16)
        return carry

    lax.fori_loop(0, rg // nr, rows_body, 0)


def _attention(q, k, v, bias_tab):
    b, l, _ = q.shape
    rows = l // GRID_W
    wr = min(WIN_ROWS, rows)
    rg = min(8, rows)
    once = pl.Buffered(1)
    kern = functools.partial(_attn_kernel, rows=rows, rg=rg, wr=wr, nr=2)
    return pl.pallas_call(
        kern,
        grid=(b, rows // rg),
        in_specs=[pl.BlockSpec((None, rg * GRID_W, ATTN_W), lambda bi, g: (bi, g, 0)),
                  pl.BlockSpec((None, l, ATTN_W), lambda bi, g: (bi, 0, 0), pipeline_mode=once),
                  pl.BlockSpec((None, l, ATTN_W), lambda bi, g: (bi, 0, 0), pipeline_mode=once),
                  pl.BlockSpec(bias_tab.shape, lambda bi, g: (0, 0, 0, 0), pipeline_mode=once)],
        out_specs=pl.BlockSpec((None, rg * GRID_W, ATTN_W), lambda bi, g: (bi, g, 0)),
        out_shape=jax.ShapeDtypeStruct((b, l, ATTN_W), bf16),
        compiler_params=_params("arbitrary", "arbitrary"),
        name="attention",
    )(q, k, v, bias_tab)


def _bias_table(rpb):
    cols = jnp.arange(GRID_W)
    col_start = jnp.clip(cols - WIN_COLS // 2, 0, GRID_W - WIN_COLS)
    kc = jnp.arange(GRID_W)[None, :]
    valid = (kc >= col_start[:, None]) & (kc < col_start[:, None] + WIN_COLS)
    off = jnp.clip(kc - cols[:, None] + (WIN_COLS - 1), 0, 2 * WIN_COLS - 2)
    tab = jnp.where(valid[None, None], rpb[:, :, off].astype(f32), NEG)
    return jnp.concatenate([tab[:, :-1], tab[:, 1:]], axis=-1)


def _conv_kernel(u_ref, prev_ref, next_ref, w_ref, cb_ref, lg_ref, lb_ref, seg_ref, o_ref,
                 win_ref, sh_ref, *, tl, chunk):
    i = pl.program_id(1)
    n_i = pl.num_programs(1)
    halo = prev_ref.shape[0]
    sub = 8
    win_ref[0:halo, :] = jnp.where(i > 0, prev_ref[...], 0.0)
    win_ref[halo:halo + tl, :] = u_ref[...]
    win_ref[halo + tl:, :] = jnp.where(i < n_i - 1, next_ref[...], 0.0)
    sh_rows = sh_ref.shape[1]
    for r in range(1, sub):
        for k0 in range(0, sh_rows, chunk):
            k1 = min(k0 + chunk, sh_rows)
            sh_ref[r - 1, k0:k1, :] = win_ref[r + k0:r + k1, :]
    seg = seg_ref[...]
    pad = CONV_K // 2
    for c in range(tl // chunk):
        acc = jnp.zeros((chunk, CONV_W), f32)
        for j in range(CONV_K):
            start = halo - pad + c * chunk + j
            r = start % sub
            src = win_ref if r == 0 else sh_ref.at[r - 1]
            acc = acc + src[start - r:start - r + chunk, :] * w_ref[j:j + 1, :]
        y = acc + cb_ref[...]
        y_hi, y_lo = _split_bf16(y)
        mu = (jnp.dot(y_hi, seg, preferred_element_type=f32)
              + jnp.dot(y_lo, seg, preferred_element_type=f32)) * (1.0 / CONV_GROUP)
        dlt = y - mu
        d_hi, d_lo = _split_bf16(dlt * dlt)
        var = (jnp.dot(d_hi, seg, preferred_element_type=f32)
               + jnp.dot(d_lo, seg, preferred_element_type=f32)) * (1.0 / CONV_GROUP)
        un = dlt * lax.rsqrt(var + EPS) * lg_ref[...] + lb_ref[...]
        o_ref[c * chunk:(c + 1) * chunk, :] = (un * jax.nn.sigmoid(un)).astype(bf16)


def _conv(u, conv_w, conv_b, ln_g, ln_b, seg):
    b, l, _ = u.shape
    tl = min(512, l)
    halo = 16
    nh = l // halo
    kern = functools.partial(_conv_kernel, tl=tl, chunk=min(128, tl))
    full = lambda a: pl.BlockSpec(a.shape, lambda bi, i: (0,) * a.ndim)
    return pl.pallas_call(
        kern,
        grid=(b, l // tl),
        in_specs=[pl.BlockSpec((None, tl, CONV_W), lambda bi, i: (bi, i, 0)),
                  pl.BlockSpec((None, halo, CONV_W),
                               lambda bi, i: (bi, jnp.maximum(i * (tl // halo) - 1, 0), 0)),
                  pl.BlockSpec((None, halo, CONV_W),
                               lambda bi, i: (bi, jnp.minimum((i + 1) * (tl // halo), nh - 1), 0)),
                  full(conv_w), full(conv_b), full(ln_g), full(ln_b), full(seg)],
        out_specs=pl.BlockSpec((None, tl, CONV_W), lambda bi, i: (bi, i, 0)),
        out_shape=jax.ShapeDtypeStruct((b, l, CONV_W), bf16),
        scratch_shapes=[pltpu.VMEM((tl + 2 * halo, CONV_W), f32),
                        pltpu.VMEM((7, tl + 2 * halo - 8, CONV_W), f32)],
        compiler_params=_params("arbitrary", "arbitrary"),
        name="conv",
    )(u, u, u, conv_w, conv_b, ln_g, ln_b, seg)


def _outproj_kernel(at_ref, cv_ref, x_ref, mod_ref, wo_ref, g2_ref, wr_ref,
                    x1_ref, h2_ref, aff_ref, *, nparts):
    tm, d = x_ref.shape
    rpt = d // LANES
    gate1 = mod_ref[:, 2 * d:3 * d]
    shift2 = mod_ref[:, 3 * d:4 * d]
    scale2 = mod_ref[:, 4 * d:5 * d]
    w_hi, w_lo = _split_bf16(wr_ref[...])
    parts = [slice(p * (tm // nparts), (p + 1) * (tm // nparts)) for p in range(nparts)]
    mixes = [jnp.dot(at_ref[rows, :], wo_ref[0:ATTN_W, :], preferred_element_type=f32)
             + jnp.dot(cv_ref[rows, :], wo_ref[ATTN_W:, :], preferred_element_type=f32)
             for rows in parts]
    splits = []
    for rows, mix in zip(parts, mixes):
        x1 = x_ref[rows, :] + gate1 * mix
        x1_ref[rows, :] = x1
        ms = jnp.mean(x1 * x1, axis=-1, keepdims=True)
        h2 = x1 * lax.rsqrt(ms + EPS) * g2_ref[...]
        h2 = h2 * (1.0 + scale2) + shift2
        h_hi, h_lo = _split_bf16(h2)
        h_rows = h_hi.astype(f32)
        nrows = rows.stop - rows.start
        for c in range(rpt):
            h2_ref[pl.ds(rows.start * rpt + c, nrows, stride=rpt), :] = h_rows[:, c * LANES:(c + 1) * LANES]
        splits.append((h_hi, h_lo))
    lgs = [jnp.dot(h_hi, w_hi, preferred_element_type=f32)
           + jnp.dot(h_lo, w_hi, preferred_element_type=f32)
           + jnp.dot(h_hi, w_lo, preferred_element_type=f32) for h_hi, h_lo in splits]
    for rows, lg in zip(parts, lgs):
        logits = lg.T[0:aff_ref.shape[0], :]
        mx = jnp.max(logits, axis=0, keepdims=True)
        ex = jnp.exp(logits - mx)
        aff_ref[:, rows] = ex / jnp.sum(ex, axis=0, keepdims=True)


def _outproj(attn, conv, x, mod, w_out_b, norm2_g, w_router):
    b, l, d = x.shape
    tm = min(512, l)
    nt = l // tm
    e = w_router.shape[1]
    w_router_t = jnp.pad(w_router.astype(f32), ((0, 0), (0, LANES - e)))
    tok = lambda w: pl.BlockSpec((None, tm, w), lambda bi, i: (bi, i, 0))
    full = lambda a: pl.BlockSpec(a.shape, lambda bi, i: (0,) * a.ndim)
    return pl.pallas_call(
        functools.partial(_outproj_kernel, nparts=2),
        grid=(b, nt),
        in_specs=[tok(ATTN_W), tok(CONV_W), tok(d),
                  pl.BlockSpec((None, 1, mod.shape[-1]), lambda bi, i: (bi, 0, 0)),
                  full(w_out_b), full(norm2_g), full(w_router_t)],
        out_specs=[tok(d),
                   pl.BlockSpec((tm * (d // LANES), LANES), lambda bi, i: (bi * nt + i, 0)),
                   pl.BlockSpec((e, tm), lambda bi, i: (0, bi * nt + i))],
        out_shape=[jax.ShapeDtypeStruct((b, l, d), f32),
                   jax.ShapeDtypeStruct((b * l * (d // LANES), LANES), f32),
                   jax.ShapeDtypeStruct((e, b * l), f32)],
        compiler_params=_params("arbitrary", "arbitrary"),
        name="outproj",
    )(attn, conv, x, mod, w_out_b, norm2_g, w_router_t)


def _route_kernel(aff_ref, cs_ref, tst_ref, cbk_ref, *, cap, tsc, tbc):
    a = aff_ref[...]
    ne, nc, _ = a.shape

    def enough(v):
        return jnp.sum((a >= v).astype(f32), axis=(1, 2), keepdims=True) >= cap

    def pow2(k):
        return lax.bitcast_convert_type(jnp.left_shift(k + 127, 23), f32)

    def exp_body(_, st):
        klo, khi = st
        kmid = jnp.right_shift(klo + khi, 1)
        ok = enough(pow2(kmid))
        return jnp.where(ok, kmid, klo), jnp.where(ok, khi, kmid)

    klo, khi = lax.fori_loop(0, 7, exp_body,
                             (jnp.full((ne, 1, 1), -127, i32), jnp.full((ne, 1, 1), 1, i32)))

    def lin_body(_, st):
        lo, hi = st
        mid = lo + (hi - lo) * 0.5
        ok = enough(mid)
        return jnp.where(ok, mid, lo), jnp.where(ok, hi, mid)

    lo, _ = lax.fori_loop(0, 48, lin_body, (pow2(klo), pow2(khi)))
    thr = jnp.min(jnp.where(a >= lo, a, jnp.inf), axis=(1, 2), keepdims=True)

    r_i = lax.broadcasted_iota(i32, (LANES, LANES), 0)
    c_i = lax.broadcasted_iota(i32, (LANES, LANES), 1)
    upper = (r_i <= c_i).astype(bf16)
    ones = jnp.ones((LANES, LANES), bf16)
    rr = lax.broadcasted_iota(i32, (nc, nc), 0)
    cc = lax.broadcasted_iota(i32, (nc, nc), 1)
    lower = (cc < rr).astype(bf16)
    blk_r = lax.broadcasted_iota(i32, (nc, LANES), 0)
    blk_c = lax.broadcasted_iota(i32, (nc, LANES), 1)
    before_blk = (blk_r < blk_c * (tbc // LANES)).astype(bf16)
    ones8 = jnp.ones((8, LANES), bf16)
    lane8 = lax.broadcasted_iota(i32, (1, LANES), 1)

    def cumsum(m):
        within = jnp.dot(m, upper, preferred_element_type=f32)
        prev = jnp.dot(lower, m, preferred_element_type=f32)
        offs = jnp.dot(prev.astype(bf16), ones, preferred_element_type=f32)
        return within + offs

    for e in range(ne):
        a_e = a[e]
        thr_e = thr[e]
        gt_e = a_e > thr_e
        eq_e = (a_e == thr_e).astype(f32)
        need = cap - jnp.sum(gt_e.astype(f32), keepdims=True)
        rank_eq = cumsum(eq_e.astype(bf16)) - eq_e
        sel = jnp.logical_or(gt_e, jnp.logical_and(eq_e > 0.0, rank_eq < need))
        sel_b = sel.astype(bf16)
        c_incl = cumsum(sel_b)
        cs_ref[e] = jnp.where(sel, c_incl, 0.0).astype(i32)
        tst = jnp.zeros((1, LANES), f32)
        for j in range(cap // tsc + 1):
            cnt_j = jnp.sum((c_incl <= float(j * tsc)).astype(f32), keepdims=True)
            tst = jnp.where(lane8 == j, cnt_j, tst)
        tst_ref[e:e + 1, :] = tst.astype(i32)
        per_chunk = lax.dot_general(ones8, sel_b, NT_DIMS, preferred_element_type=f32)
        cbk = jnp.dot(per_chunk.astype(bf16), before_blk, preferred_element_type=f32)
        cbk_ref[e:e + 1, :] = cbk[0:1].astype(i32)


def _route(aff_t, cap, tsc, tbc):
    e, n = aff_t.shape
    nc = n // LANES
    kern = functools.partial(_route_kernel, cap=cap, tsc=tsc, tbc=tbc)
    return pl.pallas_call(
        kern,
        out_shape=[jax.ShapeDtypeStruct((e, nc, LANES), i32),
                   jax.ShapeDtypeStruct((e, LANES), i32),
                   jax.ShapeDtypeStruct((e, LANES), i32)],
        compiler_params=pltpu.CompilerParams(vmem_limit_bytes=VMEM_LIMIT),
        name="route",
    )(aff_t.reshape(e, nc, LANES))


def _compact_kernel(tst_ref, cs_ref, aff_ref, idx_ref, g_ref, *, tsc, tbc):
    e = pl.program_id(0)
    j = pl.program_id(1)
    shift = tbc.bit_length() - 1
    f_lo = tst_ref[e * LANES + j]
    f_hi = tst_ref[e * LANES + j + 1]
    b_lo = jnp.right_shift(f_lo, shift)
    b_hi = jnp.right_shift(f_hi - 1, shift)
    slot1 = j * tsc + 1 + lax.broadcasted_iota(i32, (tsc, tbc), 0)
    sub = lax.broadcasted_iota(i32, (8, tbc), 0)
    lane = lax.broadcasted_iota(i32, (8, tbc), 1)

    def body(b, acc):
        row = cs_ref[e, pl.ds(b, 1), :]
        a = aff_ref[e, pl.ds(b, 1), :]
        onehot_t = (row == slot1).astype(f32).astype(bf16)
        t = b * tbc + lane
        a_hi = a.astype(bf16).astype(f32)
        r1 = a - a_hi
        a_mid = r1.astype(bf16).astype(f32)
        a_lo = r1 - a_mid
        pay = jnp.where(sub == 0, jnp.bitwise_and(t, 255).astype(f32),
              jnp.where(sub == 1, jnp.right_shift(t, 8).astype(f32),
              jnp.where(sub == 2, a_hi,
              jnp.where(sub == 3, a_mid,
              jnp.where(sub == 4, a_lo, 0.0)))))
        return acc + lax.dot_general(pay.astype(bf16), onehot_t, NT_DIMS, preferred_element_type=f32)

    acc = lax.fori_loop(b_lo, b_hi + 1, body, jnp.zeros((8, tsc), f32))
    idx = acc[1:2] * 256.0 + acc[0:1]
    idx_ref[...] = jnp.broadcast_to(idx, (8, tsc)).astype(i32)
    g_ref[...] = jnp.broadcast_to(acc[2:3] + acc[3:4] + acc[4:5], (8, tsc))


def _compact(tst, cs, aff_t, cap, tsc, tbc):
    e, n = aff_t.shape
    nb = n // tbc
    ntile = cap // tsc
    kern = functools.partial(_compact_kernel, tsc=tsc, tbc=tbc)
    res = pl.BlockSpec((e, nb, tbc), lambda ei, j, tst_r: (0, 0, 0))
    out = pl.BlockSpec((None, None, 8, tsc), lambda ei, j, tst_r: (ei, j, 0, 0))
    idx8, g8 = pl.pallas_call(
        kern,
        grid_spec=pltpu.PrefetchScalarGridSpec(
            num_scalar_prefetch=1, grid=(e, ntile), in_specs=[res, res], out_specs=[out, out]),
        out_shape=[jax.ShapeDtypeStruct((e, ntile, 8, tsc), i32),
                   jax.ShapeDtypeStruct((e, ntile, 8, tsc), f32)],
        compiler_params=_params("arbitrary", "arbitrary"),
        name="compact",
    )(tst.reshape(-1), cs.reshape(e, nb, tbc), aff_t.reshape(e, nb, tbc))
    return idx8[:, :, 0, :].reshape(e, cap), g8[:, :, 0, :].reshape(e, cap)


def _gather_kernel(idx_ref, h0_ref, h1_ref, xe_ref, sem, *, rb, steps_per_group, rpt):
    grp = pl.program_id(0) // steps_per_group

    def token_rows(t):
        return pl.ds(pl.multiple_of(t * rpt, rpt), rpt)

    def issue(src_ref):
        def body(r8, carry):
            for j in range(8):
                r = r8 * 8 + j
                tok = idx_ref[0, r]
                pltpu.make_async_copy(src_ref.at[token_rows(tok)], xe_ref.at[token_rows(r)],
                                      sem).start(priority=j % 2)
            return carry
        lax.fori_loop(0, rb // 8, body, 0)

    @pl.when(grp == 0)
    def _():
        issue(h0_ref)

    @pl.when(grp == 1)
    def _():
        issue(h1_ref)

    pltpu.make_async_copy(h0_ref.at[pl.ds(0, rb * rpt)], xe_ref, sem).wait()


def _gather(idx_all, h0, h1, cap, rpt):
    d = h0.shape[-1]
    total = idx_all.shape[1]
    rb = min(512, cap)
    steps_per_group = total // rb
    kern = functools.partial(_gather_kernel, rb=rb, steps_per_group=steps_per_group, rpt=rpt)
    per_cap = cap // rb

    def dest_block(i):
        grp = i // steps_per_group
        li = i - grp * steps_per_group
        e = li // per_cap
        return (e * 2 * per_cap + grp * per_cap + (li - e * per_cap), 0)

    return pl.pallas_call(
        kern,
        grid=(2 * steps_per_group,),
        in_specs=[pl.BlockSpec((None, 1, rb), lambda i: (i, 0, 0), memory_space=pltpu.SMEM),
                  pl.BlockSpec(memory_space=pl.ANY),
                  pl.BlockSpec(memory_space=pl.ANY)],
        out_specs=pl.BlockSpec((rb * rpt, d), dest_block),
        out_shape=jax.ShapeDtypeStruct((2 * total * rpt, d), f32),
        scratch_shapes=[pltpu.SemaphoreType.DMA],
        compiler_params=_params("arbitrary"),
        name="gather",
    )(idx_all.reshape(2 * steps_per_group, 1, rb), h0, h1)


def _expert_kernel(x_ref, g_ref, wg_ref, wu_ref, wd_ref, y_ref, xb_ref, acc_ref, gcol_ref):
    f = pl.program_id(2)

    @pl.when(f == 0)
    def _():
        tm, d = xb_ref.shape
        rpt = d // LANES
        for c in range(rpt):
            xb_ref[:, c * LANES:(c + 1) * LANES] = x_ref[pl.ds(c, tm, stride=rpt), :].astype(bf16)
        acc_ref[...] = jnp.zeros_like(acc_ref)
        gcol_ref[...] = jnp.broadcast_to(g_ref[...], (LANES, g_ref.shape[-1])).T

    xb = xb_ref[...]
    hg = jnp.dot(xb, wg_ref[...], preferred_element_type=f32)
    hu = jnp.dot(xb, wu_ref[...], preferred_element_type=f32)
    hid = (hg * jax.nn.sigmoid(hg) * hu).astype(bf16)
    acc_ref[...] += jnp.dot(hid, wd_ref[...], preferred_element_type=f32)

    @pl.when(f == pl.num_programs(2) - 1)
    def _():
        gcol = gcol_ref[...]
        d = acc_ref.shape[-1]
        for c in range(d // LANES):
            cols = slice(c * LANES, (c + 1) * LANES)
            y_ref[:, cols] = acc_ref[:, cols] * gcol


def _experts(xe, g_rows, wg_b, wu_b, wd_b, tm, fc):
    e, d, fdim = wg_b.shape
    rpt = d // LANES
    rows = xe.shape[0] // rpt
    per_e = rows // e
    nt = per_e // tm
    return pl.pallas_call(
        _expert_kernel,
        grid=(e, nt, fdim // fc),
        in_specs=[pl.BlockSpec((tm * rpt, LANES), lambda ei, t, f: (ei * nt + t, 0)),
                  pl.BlockSpec((None, 1, tm), lambda ei, t, f: (ei * nt + t, 0, 0)),
                  pl.BlockSpec((None, d, fc), lambda ei, t, f: (ei, 0, f)),
                  pl.BlockSpec((None, d, fc), lambda ei, t, f: (ei, 0, f)),
                  pl.BlockSpec((None, fc, d), lambda ei, t, f: (ei, f, 0))],
        out_specs=pl.BlockSpec((tm, d), lambda ei, t, f: (ei * nt + t, 0)),
        out_shape=jax.ShapeDtypeStruct((rows, d), f32),
        scratch_shapes=[pltpu.VMEM((tm, d), bf16), pltpu.VMEM((tm, d), f32),
                        pltpu.VMEM((tm, LANES), f32)],
        compiler_params=_params("arbitrary", "arbitrary", "arbitrary"),
        name="experts",
    )(xe, g_rows.reshape(rows // tm, 1, tm), wg_b, wu_b, wd_b)


def _combine_kernel(cbk_ref, cs_ref, x1_ref, mod_ref, ye_ref, o_ref, win_ref, xwin_ref, sem, xsem,
                    acc_ref, *, grp, cap, tbc, win, ne, total_rows):
    blk = pl.program_id(0)
    nblk = pl.num_programs(0)
    d = acc_ref.shape[-1]
    slot = blk % 2

    def region(ee):
        return ee * 2 * cap + grp * cap

    def first_row(bb, ee):
        st = cbk_ref[ee * LANES + bb]
        row = region(ee) + jnp.left_shift(jnp.right_shift(st, 3), 3)
        return pl.multiple_of(jnp.minimum(row, total_rows - win), 8)

    def start_block(bb, sl):
        for ee in range(ne):
            pltpu.make_async_copy(ye_ref.at[pl.ds(first_row(bb, ee), win)],
                                  win_ref.at[sl, pl.ds(ee * win, win)], sem.at[sl]).start()

    @pl.when(blk == 0)
    def _():
        start_block(blk, slot)

    @pl.when(blk + 1 < nblk)
    def _():
        start_block(blk + 1, (blk + 1) % 2)

    pltpu.make_async_copy(ye_ref.at[pl.ds(0, ne * win)], win_ref.at[slot], sem.at[slot]).wait()

    w_iota = lax.broadcasted_iota(i32, (win, tbc), 0)

    def onehot_of(cs_row, base_slot, done_slots):
        hit = jnp.logical_and(cs_row == base_slot + 1 + w_iota, cs_row > done_slots)
        return hit.astype(f32).T.astype(bf16)

    onehot = jnp.concatenate(
        [onehot_of(cs_ref[ee:ee + 1, :], first_row(blk, ee) - region(ee), 0) for ee in range(ne)],
        axis=1)
    acc_ref[...] = jnp.dot(onehot, win_ref[slot].astype(bf16), preferred_element_type=f32)

    def extra_expert(ee, carry):
        row0 = first_row(blk, ee)
        base0 = row0 - region(ee)
        end = cbk_ref[ee * LANES + blk + 1]
        cs_row = cs_ref[pl.ds(ee, 1), :]

        def extra(w, c2):
            row_w = pl.multiple_of(jnp.minimum(row0 + w * win, total_rows - win), 8)
            cp = pltpu.make_async_copy(ye_ref.at[pl.ds(row_w, win)], xwin_ref, xsem)
            cp.start()
            cp.wait()
            oh = onehot_of(cs_row, row_w - region(ee), base0 + w * win)
            acc_ref[...] += jnp.dot(oh, xwin_ref[...].astype(bf16), preferred_element_type=f32)
            return c2

        lax.fori_loop(1, (end - base0 + win - 1) // win, extra, 0)
        return carry

    lax.fori_loop(0, ne, extra_expert, 0)

    gate2 = mod_ref[:, 5 * d:6 * d]
    o_ref[...] = x1_ref[...] + gate2 * acc_ref[...]


def _combine(cbk, cs, x1, mod, ye, grp, cap, tbc):
    b, l, d = x1.shape
    n = b * l
    e = cs.shape[0]
    nb = n // tbc
    win = min(128, cap)
    per_b = l // tbc
    kern = functools.partial(_combine_kernel, grp=grp, cap=cap, tbc=tbc, win=win, ne=e,
                             total_rows=ye.shape[0])
    return pl.pallas_call(
        kern,
        grid_spec=pltpu.PrefetchScalarGridSpec(
            num_scalar_prefetch=1, grid=(nb,),
            in_specs=[pl.BlockSpec((e, tbc), lambda bk, c: (0, bk)),
                      pl.BlockSpec((tbc, d), lambda bk, c: (bk, 0)),
                      pl.BlockSpec((None, 1, mod.shape[-1]), lambda bk, c: (bk // per_b, 0, 0)),
                      pl.BlockSpec(memory_space=pl.ANY)],
            out_specs=pl.BlockSpec((tbc, d), lambda bk, c: (bk, 0)),
            scratch_shapes=[pltpu.VMEM((2, e * win, d), f32), pltpu.VMEM((win, d), f32),
                            pltpu.SemaphoreType.DMA((2,)), pltpu.SemaphoreType.DMA,
                            pltpu.VMEM((tbc, d), f32)]),
        out_shape=jax.ShapeDtypeStruct((n, d), f32),
        compiler_params=_params("arbitrary"),
        name="combine",
    )(cbk.reshape(-1), cs, x1.reshape(n, d), mod, ye).reshape(b, l, d)


TSC = 512
TBC = 1024
TBF = 512
TM_EXPERT = 1024
FC_EXPERT = 256


def kernel(x_prompt, x_sample, c_prompt, c_sample, ada_w, ada_b, norm1_g, w_in, q_norm_g, k_norm_g, rpb,
           conv_w, conv_b, conv_ln_g, conv_ln_b, w_out, norm2_g, w_router, w_gate, w_up, w_down):
    xs = (x_prompt, x_sample)
    d = x_prompt.shape[-1]
    ne = w_router.shape[1]
    row = lambda a: a.reshape(1, -1).astype(f32)

    nb0, nb1 = c_prompt.shape[0], c_sample.shape[0]
    pad = (-(nb0 + nb1)) % 8
    c_all = jnp.concatenate([c_prompt, c_sample, jnp.zeros((pad, d), f32)], axis=0)
    mod_all = _ada(c_all, ada_w, ada_b)
    mods = (mod_all[:nb0].reshape(nb0, 1, -1), mod_all[nb0:nb0 + nb1].reshape(nb1, 1, -1))

    seg_i = jnp.arange(ATTN_W) // HEAD_DIM
    seg = (seg_i[:, None] == seg_i[None, :]).astype(bf16)
    gq_t = jnp.tile(q_norm_g.astype(f32), N_HEADS).reshape(1, ATTN_W)
    gk_t = jnp.tile(k_norm_g.astype(f32), N_HEADS).reshape(1, ATTN_W)
    bias_tab = _bias_table(rpb)
    w_in_b = w_in.astype(bf16)
    w_out_b = w_out.astype(bf16)
    wg_b, wu_b, wd_b = w_gate.astype(bf16), w_up.astype(bf16), w_down.astype(bf16)

    x1s, h2s, idxs, gs, css, cbks = [], [], [], [], [], []
    cap = None
    for x, mod in zip(xs, mods):
        b, l, _ = x.shape
        n = b * l
        cap = max(1, EC_FACTOR * n // ne)
        q, k, v, u = _inproj(x, mod, row(norm1_g), w_in_b, seg, gq_t, gk_t)
        attn = _attention(q, k, v, bias_tab)
        conv = _conv(u, conv_w.astype(f32), row(conv_b), row(conv_ln_g), row(conv_ln_b), seg)
        x1, h2, aff_t = _outproj(attn, conv, x, mod, w_out_b, row(norm2_g), w_router)
        cs, tst, cbk = _route(aff_t, cap, TSC, TBF)
        idx, g = _compact(tst, cs, aff_t, cap, TSC, TBC)
        x1s.append(x1); h2s.append(h2); idxs.append(idx.reshape(-1)); gs.append(g)
        css.append(cs.reshape(ne, n)); cbks.append(cbk)

    xe = _gather(jnp.stack(idxs), h2s[0], h2s[1], cap, d // LANES)
    g_rows = jnp.concatenate([gs[0], gs[1]], axis=1)
    ye = _experts(xe, g_rows, wg_b, wu_b, wd_b, min(TM_EXPERT, cap), min(FC_EXPERT, w_gate.shape[2]))
    outs = [_combine(cbks[gi], css[gi], x1s[gi], mods[gi], ye, gi, cap, TBF) for gi in range(2)]
    return tuple(outs)
```

```python
import functools

import jax
import jax.numpy as jnp
from jax import lax
from jax.experimental import pallas as pl
from jax.experimental.pallas import tpu as pltpu
from jax.experimental.pallas import tpu_sc as plsc

f32 = jnp.float32
bf16 = jnp.bfloat16
i32 = jnp.int32

GRID_W = 64
N_HEADS = 8
HEAD_DIM = 64
ATTN_W = N_HEADS * HEAD_DIM
CONV_W = 512
CONV_K = 31
CONV_GROUP = 64
WIN_ROWS = 8
WIN_COLS = 16
N_EXPERTS = 16
EC_FACTOR = 2
N_MOD = 6
EPS = 1e-6
NEG = -1e30

LANES = 128
VMEM_LIMIT = 56 * 1024 * 1024

NT_DIMS = (((1,), (1,)), ((), ()))


def _params(*sem):
    return pltpu.CompilerParams(dimension_semantics=sem, vmem_limit_bytes=VMEM_LIMIT)


def _split_bf16(x):
    hi = x.astype(bf16)
    lo = (x - hi.astype(f32)).astype(bf16)
    return hi, lo


def _ada_kernel(c_ref, w_ref, b_ref, o_ref):
    c = c_ref[...]
    s = c * jax.nn.sigmoid(c)
    s_hi, s_lo = _split_bf16(s)
    w_hi, w_lo = _split_bf16(w_ref[...])
    acc = jnp.dot(s_hi, w_hi, preferred_element_type=f32)
    acc += jnp.dot(s_hi, w_lo, preferred_element_type=f32)
    acc += jnp.dot(s_lo, w_hi, preferred_element_type=f32)
    o_ref[...] = acc + b_ref[...]


def _ada(c_all, ada_w, ada_b):
    rows, d = c_all.shape
    ncols = ada_w.shape[1]
    tn = next(t for t in (1024, 512, 256, 128) if ncols % t == 0)
    return pl.pallas_call(
        _ada_kernel,
        grid=(ncols // tn,),
        in_specs=[pl.BlockSpec((rows, d), lambda j: (0, 0)),
                  pl.BlockSpec((d, tn), lambda j: (0, j)),
                  pl.BlockSpec((1, tn), lambda j: (0, j))],
        out_specs=pl.BlockSpec((rows, tn), lambda j: (0, j)),
        out_shape=jax.ShapeDtypeStruct((rows, ncols), f32),
        compiler_params=_params("arbitrary"),
        name="ada",
    )(c_all, ada_w, ada_b.reshape(1, ncols))


def _inproj_kernel(x_ref, mod_ref, g1_ref, w_ref, seg_ref, gq_ref, gk_ref,
                   q_ref, k_ref, v_ref, u_ref):
    x = x_ref[...]
    d = x.shape[-1]
    ms = jnp.mean(x * x, axis=-1, keepdims=True)
    h = x * lax.rsqrt(ms + EPS) * g1_ref[...]
    shift = mod_ref[:, 0:d]
    scale = mod_ref[:, d:2 * d]
    h = h * (1.0 + scale) + shift
    z = jnp.dot(h.astype(bf16), w_ref[...], preferred_element_type=f32)
    zq = z[:, 0:ATTN_W]
    zk = z[:, ATTN_W:2 * ATTN_W]
    seg = seg_ref[...]
    ssq_q = jnp.dot((zq * zq).astype(bf16), seg, preferred_element_type=f32)
    ssq_k = jnp.dot((zk * zk).astype(bf16), seg, preferred_element_type=f32)
    qs = gq_ref[...] * gk_ref[...] * (HEAD_DIM ** -0.5)
    q_ref[...] = (zq * lax.rsqrt(ssq_q * (1.0 / HEAD_DIM) + EPS) * qs).astype(bf16)
    k_ref[...] = (zk * lax.rsqrt(ssq_k * (1.0 / HEAD_DIM) + EPS)).astype(bf16)
    v_ref[...] = z[:, 2 * ATTN_W:3 * ATTN_W].astype(bf16)
    a = z[:, 3 * ATTN_W:3 * ATTN_W + CONV_W]
    gt = z[:, 3 * ATTN_W + CONV_W:]
    u_ref[...] = a * jax.nn.sigmoid(gt)


def _inproj(x, mod, norm1_g, w_in_b, seg, gq_t, gk_t):
    b, l, d = x.shape
    tm = min(512, l)
    ncols = w_in_b.shape[1]
    tok = lambda w: pl.BlockSpec((None, tm, w), lambda bi, i: (bi, i, 0))
    full = lambda a: pl.BlockSpec(a.shape, lambda bi, i: (0,) * a.ndim)
    return pl.pallas_call(
        _inproj_kernel,
        grid=(b, l // tm),
        in_specs=[tok(d),
                  pl.BlockSpec((None, 1, mod.shape[-1]), lambda bi, i: (bi, 0, 0)),
                  full(norm1_g), full(w_in_b), full(seg), full(gq_t), full(gk_t)],
        out_specs=[tok(ATTN_W), tok(ATTN_W), tok(ATTN_W), tok(CONV_W)],
        out_shape=[jax.ShapeDtypeStruct((b, l, ATTN_W), bf16)] * 3
                  + [jax.ShapeDtypeStruct((b, l, CONV_W), f32)],
        compiler_params=_params("arbitrary", "arbitrary"),
        name="inproj",
    )(x, mod, norm1_g, w_in_b, seg, gq_t, gk_t)


def _attn_kernel(q_ref, k_ref, v_ref, bias_ref, o_ref, *, rows, rg, wr, nr):
    g = pl.program_id(1)
    lane = lax.broadcasted_iota(i32, (GRID_W, LANES), 1)
    first_head = lane < HEAD_DIM
    nkeys = wr * GRID_W

    pairs = [(rr, p) for rr in range(nr) for p in range(N_HEADS // 2)]

    def rows_body(it, carry):
        geo = []
        for rr in range(nr):
            lr = it * nr + rr
            r = g * rg + lr
            rs = jnp.clip(r - wr // 2, 0, rows - wr)
            geo.append((pl.multiple_of(lr * GRID_W, GRID_W),
                        pl.multiple_of(rs * GRID_W, GRID_W),
                        rs - r + (WIN_ROWS - 1)))
        scores = []
        for rr, p in pairs:
            q0, tok0, rb = geo[rr]
            cols = slice(p * LANES, (p + 1) * LANES)
            qp = q_ref[pl.ds(q0, GRID_W), cols]
            kp = k_ref[pl.ds(tok0, nkeys), cols]
            zero = jnp.zeros_like(qp)
            q2 = jnp.concatenate([jnp.where(first_head, qp, zero),
                                  jnp.where(first_head, zero, qp)], axis=0)
            s = lax.dot_general(q2, kp, NT_DIMS, preferred_element_type=f32)
            bias = jnp.concatenate(
                [jnp.concatenate([bias_ref[2 * p + j, rb + 2 * m] for m in range(nkeys // LANES)],
                                 axis=-1) for j in range(2)], axis=0)
            scores.append(s + bias)
        probs = []
        for s in scores:
            mx = jnp.max(s, axis=-1, keepdims=True)
            e = jnp.exp(s - mx)
            probs.append((e.astype(bf16), jnp.sum(e, axis=-1, keepdims=True)))
        for (rr, p), (e, den) in zip(pairs, probs):
            q0, tok0, rb = geo[rr]
            cols = slice(p * LANES, (p + 1) * LANES)
            vp = v_ref[pl.ds(tok0, nkeys), cols]
            o = jnp.dot(e, vp, preferred_element_type=f32) / den
            o_ref[pl.ds(q0, GRID_W), cols] = jnp.where(
                first_head, o[0:GRID_W], o[GRID_W:]).astype(bf16)
        return carry

    lax.fori_loop(0, rg // nr, rows_body, 0)


def _attention(q, k, v, bias_tab):
    b, l, _ = q.shape
    rows = l // GRID_W
    wr = min(WIN_ROWS, rows)
    rg = min(8, rows)
    once = pl.Buffered(1)
    kern = functools.partial(_attn_kernel, rows=rows, rg=rg, wr=wr, nr=2)
    return pl.pallas_call(
        kern,
        grid=(b, rows // rg),
        in_specs=[pl.BlockSpec((None, rg * GRID_W, ATTN_W), lambda bi, g: (bi, g, 0)),
                  pl.BlockSpec((None, l, ATTN_W), lambda bi, g: (bi, 0, 0), pipeline_mode=once),
                  pl.BlockSpec((None, l, ATTN_W), lambda bi, g: (bi, 0, 0), pipeline_mode=once),
                  pl.BlockSpec(bias_tab.shape, lambda bi, g: (0, 0, 0, 0), pipeline_mode=once)],
        out_specs=pl.BlockSpec((None, rg * GRID_W, ATTN_W), lambda bi, g: (bi, g, 0)),
        out_shape=jax.ShapeDtypeStruct((b, l, ATTN_W), bf16),
        compiler_params=_params("arbitrary", "arbitrary"),
        name="attention",
    )(q, k, v, bias_tab)


def _bias_table(rpb):
    cols = jnp.arange(GRID_W)
    col_start = jnp.clip(cols - WIN_COLS // 2, 0, GRID_W - WIN_COLS)
    kc = jnp.arange(GRID_W)[None, :]
    valid = (kc >= col_start[:, None]) & (kc < col_start[:, None] + WIN_COLS)
    off = jnp.clip(kc - cols[:, None] + (WIN_COLS - 1), 0, 2 * WIN_COLS - 2)
    tab = jnp.where(valid[None, None], rpb[:, :, off].astype(f32), NEG)
    return jnp.concatenate([tab[:, :-1], tab[:, 1:]], axis=-1)


def _conv_kernel(u_ref, prev_ref, next_ref, w_ref, cb_ref, lg_ref, lb_ref, seg_ref, o_ref,
                 win_ref, sh_ref, *, tl, chunk):
    i = pl.program_id(1)
    n_i = pl.num_programs(1)
    halo = prev_ref.shape[0]
    sub = 8
    win_ref[0:halo, :] = jnp.where(i > 0, prev_ref[...], 0.0)
    win_ref[halo:halo + tl, :] = u_ref[...]
    win_ref[halo + tl:, :] = jnp.where(i < n_i - 1, next_ref[...], 0.0)
    sh_rows = sh_ref.shape[1]
    for r in range(1, sub):
        for k0 in range(0, sh_rows, chunk):
            k1 = min(k0 + chunk, sh_rows)
            sh_ref[r - 1, k0:k1, :] = win_ref[r + k0:r + k1, :]
    seg = seg_ref[...]
    pad = CONV_K // 2
    for c in range(tl // chunk):
        acc = jnp.zeros((chunk, CONV_W), f32)
        for j in range(CONV_K):
            start = halo - pad + c * chunk + j
            r = start % sub
            src = win_ref if r == 0 else sh_ref.at[r - 1]
            acc = acc + src[start - r:start - r + chunk, :] * w_ref[j:j + 1, :]
        y = acc + cb_ref[...]
        y_hi, y_lo = _split_bf16(y)
        mu = (jnp.dot(y_hi, seg, preferred_element_type=f32)
              + jnp.dot(y_lo, seg, preferred_element_type=f32)) * (1.0 / CONV_GROUP)
        dlt = y - mu
        d_hi, d_lo = _split_bf16(dlt * dlt)
        var = (jnp.dot(d_hi, seg, preferred_element_type=f32)
               + jnp.dot(d_lo, seg, preferred_element_type=f32)) * (1.0 / CONV_GROUP)
        un = dlt * lax.rsqrt(var + EPS) * lg_ref[...] + lb_ref[...]
        o_ref[c * chunk:(c + 1) * chunk, :] = (un * jax.nn.sigmoid(un)).astype(bf16)


def _conv(u, conv_w, conv_b, ln_g, ln_b, seg):
    b, l, _ = u.shape
    tl = min(512, l)
    halo = 16
    nh = l // halo
    kern = functools.partial(_conv_kernel, tl=tl, chunk=min(128, tl))
    full = lambda a: pl.BlockSpec(a.shape, lambda bi, i: (0,) * a.ndim)
    return pl.pallas_call(
        kern,
        grid=(b, l // tl),
        in_specs=[pl.BlockSpec((None, tl, CONV_W), lambda bi, i: (bi, i, 0)),
                  pl.BlockSpec((None, halo, CONV_W),
                               lambda bi, i: (bi, jnp.maximum(i * (tl // halo) - 1, 0), 0)),
                  pl.BlockSpec((None, halo, CONV_W),
                               lambda bi, i: (bi, jnp.minimum((i + 1) * (tl // halo), nh - 1), 0)),
                  full(conv_w), full(conv_b), full(ln_g), full(ln_b), full(seg)],
        out_specs=pl.BlockSpec((None, tl, CONV_W), lambda bi, i: (bi, i, 0)),
        out_shape=jax.ShapeDtypeStruct((b, l, CONV_W), bf16),
        scratch_shapes=[pltpu.VMEM((tl + 2 * halo, CONV_W), f32),
                        pltpu.VMEM((7, tl + 2 * halo - 8, CONV_W), f32)],
        compiler_params=_params("arbitrary", "arbitrary"),
        name="conv",
    )(u, u, u, conv_w, conv_b, ln_g, ln_b, seg)


def _outproj_kernel(at_ref, cv_ref, x_ref, mod_ref, wo_ref, g2_ref, wr_ref,
                    x1_ref, h2_ref, aff_ref, *, nparts):
    tm, d = x_ref.shape
    rpt = d // LANES
    gate1 = mod_ref[:, 2 * d:3 * d]
    shift2 = mod_ref[:, 3 * d:4 * d]
    scale2 = mod_ref[:, 4 * d:5 * d]
    w_hi, w_lo = _split_bf16(wr_ref[...])
    parts = [slice(p * (tm // nparts), (p + 1) * (tm // nparts)) for p in range(nparts)]
    mixes = [jnp.dot(at_ref[rows, :], wo_ref[0:ATTN_W, :], preferred_element_type=f32)
             + jnp.dot(cv_ref[rows, :], wo_ref[ATTN_W:, :], preferred_element_type=f32)
             for rows in parts]
    splits = []
    for rows, mix in zip(parts, mixes):
        x1 = x_ref[rows, :] + gate1 * mix
        x1_ref[rows, :] = x1
        ms = jnp.mean(x1 * x1, axis=-1, keepdims=True)
        h2 = x1 * lax.rsqrt(ms + EPS) * g2_ref[...]
        h2 = h2 * (1.0 + scale2) + shift2
        h_hi, h_lo = _split_bf16(h2)
        h_rows = h_hi.astype(f32)
        nrows = rows.stop - rows.start
        for c in range(rpt):
            h2_ref[pl.ds(rows.start * rpt + c, nrows, stride=rpt), :] = h_rows[:, c * LANES:(c + 1) * LANES]
        splits.append((h_hi, h_lo))
    lgs = [jnp.dot(h_hi, w_hi, preferred_element_type=f32)
           + jnp.dot(h_lo, w_hi, preferred_element_type=f32)
           + jnp.dot(h_hi, w_lo, preferred_element_type=f32) for h_hi, h_lo in splits]
    for rows, lg in zip(parts, lgs):
        logits = lg.T[0:aff_ref.shape[0], :]
        mx = jnp.max(logits, axis=0, keepdims=True)
        ex = jnp.exp(logits - mx)
        aff_ref[:, rows] = ex / jnp.sum(ex, axis=0, keepdims=True)


def _outproj(attn, conv, x, mod, w_out_b, norm2_g, w_router):
    b, l, d = x.shape
    tm = min(512, l)
    nt = l // tm
    e = w_router.shape[1]
    w_router_t = jnp.pad(w_router.astype(f32), ((0, 0), (0, LANES - e)))
    tok = lambda w: pl.BlockSpec((None, tm, w), lambda bi, i: (bi, i, 0))
    full = lambda a: pl.BlockSpec(a.shape, lambda bi, i: (0,) * a.ndim)
    return pl.pallas_call(
        functools.partial(_outproj_kernel, nparts=2),
        grid=(b, nt),
        in_specs=[tok(ATTN_W), tok(CONV_W), tok(d),
                  pl.BlockSpec((None, 1, mod.shape[-1]), lambda bi, i: (bi, 0, 0)),
                  full(w_out_b), full(norm2_g), full(w_router_t)],
        out_specs=[tok(d),
                   pl.BlockSpec((tm * (d // LANES), LANES), lambda bi, i: (bi * nt + i, 0)),
                   pl.BlockSpec((e, tm), lambda bi, i: (0, bi * nt + i))],
        out_shape=[jax.ShapeDtypeStruct((b, l, d), f32),
                   jax.ShapeDtypeStruct((b * l * (d // LANES), LANES), f32),
                   jax.ShapeDtypeStruct((e, b * l), f32)],
        compiler_params=_params("arbitrary", "arbitrary"),
        name="outproj",
    )(attn, conv, x, mod, w_out_b, norm2_g, w_router_t)


def _route_kernel(aff_ref, cs_ref, tst_ref, cbk_ref, *, cap, tsc, tbc):
    a = aff_ref[...]
    ne, nc, _ = a.shape

    def enough(v):
        return jnp.sum((a >= v).astype(f32), axis=(1, 2), keepdims=True) >= cap

    def pow2(k):
        return lax.bitcast_convert_type(jnp.left_shift(k + 127, 23), f32)

    def exp_body(_, st):
        klo, khi = st
        kmid = jnp.right_shift(klo + khi, 1)
        ok = enough(pow2(kmid))
        return jnp.where(ok, kmid, klo), jnp.where(ok, khi, kmid)

    klo, khi = lax.fori_loop(0, 7, exp_body,
                             (jnp.full((ne, 1, 1), -127, i32), jnp.full((ne, 1, 1), 1, i32)))

    def lin_body(_, st):
        lo, hi = st
        mid = lo + (hi - lo) * 0.5
        ok = enough(mid)
        return jnp.where(ok, mid, lo), jnp.where(ok, hi, mid)

    lo, _ = lax.fori_loop(0, 48, lin_body, (pow2(klo), pow2(khi)))
    thr = jnp.min(jnp.where(a >= lo, a, jnp.inf), axis=(1, 2), keepdims=True)

    r_i = lax.broadcasted_iota(i32, (LANES, LANES), 0)
    c_i = lax.broadcasted_iota(i32, (LANES, LANES), 1)
    upper = (r_i <= c_i).astype(bf16)
    ones = jnp.ones((LANES, LANES), bf16)
    rr = lax.broadcasted_iota(i32, (nc, nc), 0)
    cc = lax.broadcasted_iota(i32, (nc, nc), 1)
    lower = (cc < rr).astype(bf16)
    blk_r = lax.broadcasted_iota(i32, (nc, LANES), 0)
    blk_c = lax.broadcasted_iota(i32, (nc, LANES), 1)
    before_blk = (blk_r < blk_c * (tbc // LANES)).astype(bf16)
    ones8 = jnp.ones((8, LANES), bf16)
    lane8 = lax.broadcasted_iota(i32, (1, LANES), 1)

    def cumsum(m):
        within = jnp.dot(m, upper, preferred_element_type=f32)
        prev = jnp.dot(lower, m, preferred_element_type=f32)
        offs = jnp.dot(prev.astype(bf16), ones, preferred_element_type=f32)
        return within + offs

    for e in range(ne):
        a_e = a[e]
        thr_e = thr[e]
        gt_e = a_e > thr_e
        eq_e = (a_e == thr_e).astype(f32)
        need = cap - jnp.sum(gt_e.astype(f32), keepdims=True)
        rank_eq = cumsum(eq_e.astype(bf16)) - eq_e
        sel = jnp.logical_or(gt_e, jnp.logical_and(eq_e > 0.0, rank_eq < need))
        sel_b = sel.astype(bf16)
        c_incl = cumsum(sel_b)
        cs_ref[e] = jnp.where(sel, c_incl, 0.0).astype(i32)
        tst = jnp.zeros((1, LANES), f32)
        for j in range(cap // tsc + 1):
            cnt_j = jnp.sum((c_incl <= float(j * tsc)).astype(f32), keepdims=True)
            tst = jnp.where(lane8 == j, cnt_j, tst)
        tst_ref[e:e + 1, :] = tst.astype(i32)
        per_chunk = lax.dot_general(ones8, sel_b, NT_DIMS, preferred_element_type=f32)
        cbk = jnp.dot(per_chunk.astype(bf16), before_blk, preferred_element_type=f32)
        cbk_ref[e:e + 1, :] = cbk[0:1].astype(i32)


def _route(aff_t, cap, tsc, tbc):
    e, n = aff_t.shape
    nc = n // LANES
    kern = functools.partial(_route_kernel, cap=cap, tsc=tsc, tbc=tbc)
    return pl.pallas_call(
        kern,
        out_shape=[jax.ShapeDtypeStruct((e, nc, LANES), i32),
                   jax.ShapeDtypeStruct((e, LANES), i32),
                   jax.ShapeDtypeStruct((e, LANES), i32)],
        compiler_params=pltpu.CompilerParams(vmem_limit_bytes=VMEM_LIMIT),
        name="route",
    )(aff_t.reshape(e, nc, LANES))


def _compact_kernel(tst_ref, cs_ref, aff_ref, idx_ref, g_ref, *, tsc, tbc):
    e = pl.program_id(0)
    j = pl.program_id(1)
    shift = tbc.bit_length() - 1
    f_lo = tst_ref[e * LANES + j]
    f_hi = tst_ref[e * LANES + j + 1]
    b_lo = jnp.right_shift(f_lo, shift)
    b_hi = jnp.right_shift(f_hi - 1, shift)
    slot1 = j * tsc + 1 + lax.broadcasted_iota(i32, (tsc, tbc), 0)
    sub = lax.broadcasted_iota(i32, (8, tbc), 0)
    lane = lax.broadcasted_iota(i32, (8, tbc), 1)

    def body(b, acc):
        row = cs_ref[e, pl.ds(b, 1), :]
        a = aff_ref[e, pl.ds(b, 1), :]
        onehot_t = (row == slot1).astype(f32).astype(bf16)
        t = b * tbc + lane
        a_hi = a.astype(bf16).astype(f32)
        r1 = a - a_hi
        a_mid = r1.astype(bf16).astype(f32)
        a_lo = r1 - a_mid
        pay = jnp.where(sub == 0, jnp.bitwise_and(t, 255).astype(f32),
              jnp.where(sub == 1, jnp.right_shift(t, 8).astype(f32),
              jnp.where(sub == 2, a_hi,
              jnp.where(sub == 3, a_mid,
              jnp.where(sub == 4, a_lo, 0.0)))))
        return acc + lax.dot_general(pay.astype(bf16), onehot_t, NT_DIMS, preferred_element_type=f32)

    acc = lax.fori_loop(b_lo, b_hi + 1, body, jnp.zeros((8, tsc), f32))
    idx = acc[1:2] * 256.0 + acc[0:1]
    idx_ref[...] = jnp.broadcast_to(idx, (8, tsc)).astype(i32)
    g_ref[...] = jnp.broadcast_to(acc[2:3] + acc[3:4] + acc[4:5], (8, tsc))


def _compact(tst, cs, aff_t, cap, tsc, tbc):
    e, n = aff_t.shape
    nb = n // tbc
    ntile = cap // tsc
    kern = functools.partial(_compact_kernel, tsc=tsc, tbc=tbc)
    res = pl.BlockSpec((e, nb, tbc), lambda ei, j, tst_r: (0, 0, 0))
    out = pl.BlockSpec((None, None, 8, tsc), lambda ei, j, tst_r: (ei, j, 0, 0))
    idx8, g8 = pl.pallas_call(
        kern,
        grid_spec=pltpu.PrefetchScalarGridSpec(
            num_scalar_prefetch=1, grid=(e, ntile), in_specs=[res, res], out_specs=[out, out]),
        out_shape=[jax.ShapeDtypeStruct((e, ntile, 8, tsc), i32),
                   jax.ShapeDtypeStruct((e, ntile, 8, tsc), f32)],
        compiler_params=_params("arbitrary", "arbitrary"),
        name="compact",
    )(tst.reshape(-1), cs.reshape(e, nb, tbc), aff_t.reshape(e, nb, tbc))
    return idx8[:, :, 0, :].reshape(e, cap), g8[:, :, 0, :].reshape(e, cap)


SC_CORES = 2
SC_SUBCORES = 16
SC_CHUNK = 32


def _sc_gather(h_tiles, idx, rpt):
    slots = idx.shape[0]
    workers = SC_CORES * SC_SUBCORES
    per_w = slots // workers
    assert per_w * workers == slots and per_w % SC_CHUNK == 0
    table = h_tiles.reshape(-1, rpt, LANES)
    mesh = plsc.VectorSubcoreMesh(core_axis_name="c", subcore_axis_name="s",
                                  num_cores=SC_CORES, num_subcores=SC_SUBCORES)

    @functools.partial(
        pl.kernel, mesh=mesh,
        out_type=jax.ShapeDtypeStruct((slots, rpt, LANES), f32),
        scratch_types=[pltpu.VMEM((per_w,), i32),
                       pltpu.VMEM((SC_CHUNK, rpt, LANES), f32),
                       pltpu.SemaphoreType.DMA],
        name="sc_gather")
    def gather(table_hbm, idx_hbm, out_hbm, idx_v, rows_v, sem):
        wid = lax.axis_index("s") * SC_CORES + lax.axis_index("c")
        base = wid * per_w
        pltpu.sync_copy(idx_hbm.at[pl.ds(base, per_w)], idx_v)

        @pl.loop(0, per_w // SC_CHUNK)
        def _(c):
            off = pl.multiple_of(c * SC_CHUNK, SC_CHUNK)
            pltpu.async_copy(table_hbm.at[idx_v.at[pl.ds(off, SC_CHUNK)]], rows_v, sem).wait()
            pltpu.sync_copy(rows_v, out_hbm.at[pl.ds(base + off, SC_CHUNK)])

    return gather(table, idx).reshape(slots * rpt, LANES)


def _expert_kernel(x_ref, g_ref, wg_ref, wu_ref, wd_ref, y_ref, xb_ref, acc_ref, gcol_ref):
    f = pl.program_id(2)

    @pl.when(f == 0)
    def _():
        tm, d = xb_ref.shape
        rpt = d // LANES
        for c in range(rpt):
            xb_ref[:, c * LANES:(c + 1) * LANES] = x_ref[pl.ds(c, tm, stride=rpt), :].astype(bf16)
        acc_ref[...] = jnp.zeros_like(acc_ref)
        gcol_ref[...] = jnp.broadcast_to(g_ref[...], (LANES, g_ref.shape[-1])).T

    xb = xb_ref[...]
    hg = jnp.dot(xb, wg_ref[...], preferred_element_type=f32)
    hu = jnp.dot(xb, wu_ref[...], preferred_element_type=f32)
    hid = (hg * jax.nn.sigmoid(hg) * hu).astype(bf16)
    acc_ref[...] += jnp.dot(hid, wd_ref[...], preferred_element_type=f32)

    @pl.when(f == pl.num_programs(2) - 1)
    def _():
        gcol = gcol_ref[...]
        d = acc_ref.shape[-1]
        for c in range(d // LANES):
            cols = slice(c * LANES, (c + 1) * LANES)
            y_ref[:, cols] = acc_ref[:, cols] * gcol


def _experts(xe, g_rows, wg_b, wu_b, wd_b, tm, fc):
    e, d, fdim = wg_b.shape
    rpt = d // LANES
    rows = xe.shape[0] // rpt
    per_e = rows // e
    nt = per_e // tm
    return pl.pallas_call(
        _expert_kernel,
        grid=(e, nt, fdim // fc),
        in_specs=[pl.BlockSpec((tm * rpt, LANES), lambda ei, t, f: (ei * nt + t, 0)),
                  pl.BlockSpec((None, 1, tm), lambda ei, t, f: (ei * nt + t, 0, 0)),
                  pl.BlockSpec((None, d, fc), lambda ei, t, f: (ei, 0, f)),
                  pl.BlockSpec((None, d, fc), lambda ei, t, f: (ei, 0, f)),
                  pl.BlockSpec((None, fc, d), lambda ei, t, f: (ei, f, 0))],
        out_specs=pl.BlockSpec((tm, d), lambda ei, t, f: (ei * nt + t, 0)),
        out_shape=jax.ShapeDtypeStruct((rows, d), f32),
        scratch_shapes=[pltpu.VMEM((tm, d), bf16), pltpu.VMEM((tm, d), f32),
                        pltpu.VMEM((tm, LANES), f32)],
        compiler_params=_params("arbitrary", "arbitrary", "arbitrary"),
        name="experts",
    )(xe, g_rows.reshape(rows // tm, 1, tm), wg_b, wu_b, wd_b)


def _combine_kernel(cbk_ref, cs_ref, x1_ref, mod_ref, ye_ref, o_ref, win_ref, xwin_ref, sem, xsem,
                    acc_ref, *, cap, tbc, win, ne, total_rows):
    blk = pl.program_id(0)
    nblk = pl.num_programs(0)
    d = acc_ref.shape[-1]
    slot = blk % 2

    def region(ee):
        return ee * cap

    def first_row(bb, ee):
        st = cbk_ref[ee * LANES + bb]
        row = region(ee) + jnp.left_shift(jnp.right_shift(st, 3), 3)
        return pl.multiple_of(jnp.minimum(row, total_rows - win), 8)

    def start_block(bb, sl):
        for ee in range(ne):
            pltpu.make_async_copy(ye_ref.at[pl.ds(first_row(bb, ee), win)],
                                  win_ref.at[sl, pl.ds(ee * win, win)], sem.at[sl]).start()

    @pl.when(blk == 0)
    def _():
        start_block(blk, slot)

    @pl.when(blk + 1 < nblk)
    def _():
        start_block(blk + 1, (blk + 1) % 2)

    pltpu.make_async_copy(ye_ref.at[pl.ds(0, ne * win)], win_ref.at[slot], sem.at[slot]).wait()

    w_iota = lax.broadcasted_iota(i32, (win, tbc), 0)

    def onehot_of(cs_row, base_slot, done_slots):
        hit = jnp.logical_and(cs_row == base_slot + 1 + w_iota, cs_row > done_slots)
        return hit.astype(f32).T.astype(bf16)

    onehot = jnp.concatenate(
        [onehot_of(cs_ref[ee:ee + 1, :], first_row(blk, ee) - region(ee), 0) for ee in range(ne)],
        axis=1)
    acc_ref[...] = jnp.dot(onehot, win_ref[slot].astype(bf16), preferred_element_type=f32)

    def extra_expert(ee, carry):
        row0 = first_row(blk, ee)
        base0 = row0 - region(ee)
        end = cbk_ref[ee * LANES + blk + 1]
        cs_row = cs_ref[pl.ds(ee, 1), :]

        def extra(w, c2):
            row_w = pl.multiple_of(jnp.minimum(row0 + w * win, total_rows - win), 8)
            cp = pltpu.make_async_copy(ye_ref.at[pl.ds(row_w, win)], xwin_ref, xsem)
            cp.start()
            cp.wait()
            oh = onehot_of(cs_row, row_w - region(ee), base0 + w * win)
            acc_ref[...] += jnp.dot(oh, xwin_ref[...].astype(bf16), preferred_element_type=f32)
            return c2

        lax.fori_loop(1, (end - base0 + win - 1) // win, extra, 0)
        return carry

    lax.fori_loop(0, ne, extra_expert, 0)

    gate2 = mod_ref[:, 5 * d:6 * d]
    o_ref[...] = x1_ref[...] + gate2 * acc_ref[...]


def _combine(cbk, cs, x1, mod, ye, cap, tbc):
    b, l, d = x1.shape
    n = b * l
    e = cs.shape[0]
    nb = n // tbc
    win = min(128, cap)
    per_b = l // tbc
    kern = functools.partial(_combine_kernel, cap=cap, tbc=tbc, win=win, ne=e,
                             total_rows=ye.shape[0])
    return pl.pallas_call(
        kern,
        grid_spec=pltpu.PrefetchScalarGridSpec(
            num_scalar_prefetch=1, grid=(nb,),
            in_specs=[pl.BlockSpec((e, tbc), lambda bk, c: (0, bk)),
                      pl.BlockSpec((tbc, d), lambda bk, c: (bk, 0)),
                      pl.BlockSpec((None, 1, mod.shape[-1]), lambda bk, c: (bk // per_b, 0, 0)),
                      pl.BlockSpec(memory_space=pl.ANY)],
            out_specs=pl.BlockSpec((tbc, d), lambda bk, c: (bk, 0)),
            scratch_shapes=[pltpu.VMEM((2, e * win, d), f32), pltpu.VMEM((win, d), f32),
                            pltpu.SemaphoreType.DMA((2,)), pltpu.SemaphoreType.DMA,
                            pltpu.VMEM((tbc, d), f32)]),
        out_shape=jax.ShapeDtypeStruct((n, d), f32),
        compiler_params=_params("arbitrary"),
        name="combine",
    )(cbk.reshape(-1), cs, x1.reshape(n, d), mod, ye).reshape(b, l, d)


TSC = 512
TBC = 1024
TBF = 512
TM_EXPERT = 1024
FC_EXPERT = 256


def kernel(x_prompt, x_sample, c_prompt, c_sample, ada_w, ada_b, norm1_g, w_in, q_norm_g, k_norm_g, rpb,
           conv_w, conv_b, conv_ln_g, conv_ln_b, w_out, norm2_g, w_router, w_gate, w_up, w_down):
    xs = (x_prompt, x_sample)
    d = x_prompt.shape[-1]
    ne = w_router.shape[1]
    row = lambda a: a.reshape(1, -1).astype(f32)

    nb0, nb1 = c_prompt.shape[0], c_sample.shape[0]
    pad = (-(nb0 + nb1)) % 8
    c_all = jnp.concatenate([c_prompt, c_sample, jnp.zeros((pad, d), f32)], axis=0)
    mod_all = _ada(c_all, ada_w, ada_b)
    mods = (mod_all[:nb0].reshape(nb0, 1, -1), mod_all[nb0:nb0 + nb1].reshape(nb1, 1, -1))

    seg_i = jnp.arange(ATTN_W) // HEAD_DIM
    seg = (seg_i[:, None] == seg_i[None, :]).astype(bf16)
    gq_t = jnp.tile(q_norm_g.astype(f32), N_HEADS).reshape(1, ATTN_W)
    gk_t = jnp.tile(k_norm_g.astype(f32), N_HEADS).reshape(1, ATTN_W)
    bias_tab = _bias_table(rpb)
    w_in_b = w_in.astype(bf16)
    w_out_b = w_out.astype(bf16)
    wg_b, wu_b, wd_b = w_gate.astype(bf16), w_up.astype(bf16), w_down.astype(bf16)

    staged = []
    for x, mod in zip(xs, mods):
        b, l, _ = x.shape
        n = b * l
        cap = max(1, EC_FACTOR * n // ne)
        q, k, v, u = _inproj(x, mod, row(norm1_g), w_in_b, seg, gq_t, gk_t)
        attn = _attention(q, k, v, bias_tab)
        conv = _conv(u, conv_w.astype(f32), row(conv_b), row(conv_ln_g), row(conv_ln_b), seg)
        x1, h2, aff_t = _outproj(attn, conv, x, mod, w_out_b, row(norm2_g), w_router)
        cs, tst, cbk = _route(aff_t, cap, TSC, TBF)
        idx, g = _compact(tst, cs, aff_t, cap, TSC, TBC)
        xe = _sc_gather(h2, idx.reshape(-1), d // LANES)
        staged.append((x1, mod, xe, g, cs.reshape(ne, n), cbk, cap))

    outs = []
    for x1, mod, xe, g, cs, cbk, cap in staged:
        ye = _experts(xe, g, wg_b, wu_b, wd_b, min(TM_EXPERT, cap), min(FC_EXPERT, w_gate.shape[2]))
        outs.append(_combine(cbk, cs, x1, mod, ye, cap, TBF))
    return tuple(outs)
```

```python
import functools

import jax
import jax.numpy as jnp
from jax import lax
from jax.experimental import pallas as pl
from jax.experimental.pallas import tpu as pltpu
from jax.experimental.pallas import tpu_sc as plsc

f32 = jnp.float32
bf16 = jnp.bfloat16
i32 = jnp.int32

GRID_W = 64
N_HEADS = 8
HEAD_DIM = 64
ATTN_W = N_HEADS * HEAD_DIM
CONV_W = 512
CONV_K = 31
CONV_GROUP = 64
WIN_ROWS = 8
WIN_COLS = 16
N_EXPERTS = 16
EC_FACTOR = 2
N_MOD = 6
EPS = 1e-6
NEG = -1e30

LANES = 128
BF16_ROWS = 16
VMEM_LIMIT = 56 * 1024 * 1024

NT_DIMS = (((1,), (1,)), ((), ()))


def _params(*sem):
    return pltpu.CompilerParams(dimension_semantics=sem, vmem_limit_bytes=VMEM_LIMIT)


def _split_bf16(x):
    hi = x.astype(bf16)
    lo = (x - hi.astype(f32)).astype(bf16)
    return hi, lo


def _ada_kernel(c_ref, w_ref, b_ref, o_ref):
    c = c_ref[...]
    s = c * jax.nn.sigmoid(c)
    s_hi, s_lo = _split_bf16(s)
    w_hi, w_lo = _split_bf16(w_ref[...])
    acc = jnp.dot(s_hi, w_hi, preferred_element_type=f32)
    acc += jnp.dot(s_hi, w_lo, preferred_element_type=f32)
    acc += jnp.dot(s_lo, w_hi, preferred_element_type=f32)
    o_ref[...] = acc + b_ref[...]


def _ada(c_all, ada_w, ada_b):
    rows, d = c_all.shape
    ncols = ada_w.shape[1]
    tn = next(t for t in (1024, 512, 256, 128) if ncols % t == 0)
    return pl.pallas_call(
        _ada_kernel,
        grid=(ncols // tn,),
        in_specs=[pl.BlockSpec((rows, d), lambda j: (0, 0)),
                  pl.BlockSpec((d, tn), lambda j: (0, j)),
                  pl.BlockSpec((1, tn), lambda j: (0, j))],
        out_specs=pl.BlockSpec((rows, tn), lambda j: (0, j)),
        out_shape=jax.ShapeDtypeStruct((rows, ncols), f32),
        compiler_params=_params("arbitrary"),
        name="ada",
    )(c_all, ada_w, ada_b.reshape(1, ncols))


def _inproj_kernel(x_ref, mod_ref, g1_ref, w_ref, seg_ref, gq_ref, gk_ref,
                   q_ref, k_ref, v_ref, u_ref):
    x = x_ref[...]
    d = x.shape[-1]
    ms = jnp.mean(x * x, axis=-1, keepdims=True)
    h = x * lax.rsqrt(ms + EPS) * g1_ref[...]
    shift = mod_ref[:, 0:d]
    scale = mod_ref[:, d:2 * d]
    h = h * (1.0 + scale) + shift
    z = jnp.dot(h.astype(bf16), w_ref[...], preferred_element_type=f32)
    zq = z[:, 0:ATTN_W]
    zk = z[:, ATTN_W:2 * ATTN_W]
    seg = seg_ref[...]
    ssq_q = jnp.dot((zq * zq).astype(bf16), seg, preferred_element_type=f32)
    ssq_k = jnp.dot((zk * zk).astype(bf16), seg, preferred_element_type=f32)
    qs = gq_ref[...] * gk_ref[...] * (HEAD_DIM ** -0.5)
    q_ref[...] = (zq * lax.rsqrt(ssq_q * (1.0 / HEAD_DIM) + EPS) * qs).astype(bf16)
    k_ref[...] = (zk * lax.rsqrt(ssq_k * (1.0 / HEAD_DIM) + EPS)).astype(bf16)
    v_ref[...] = z[:, 2 * ATTN_W:3 * ATTN_W].astype(bf16)
    a = z[:, 3 * ATTN_W:3 * ATTN_W + CONV_W]
    gt = z[:, 3 * ATTN_W + CONV_W:]
    u_ref[...] = a * jax.nn.sigmoid(gt)


def _inproj(x, mod, norm1_g, w_in_b, seg, gq_t, gk_t):
    b, l, d = x.shape
    tm = min(512, l)
    ncols = w_in_b.shape[1]
    tok = lambda w: pl.BlockSpec((None, tm, w), lambda bi, i: (bi, i, 0))
    full = lambda a: pl.BlockSpec(a.shape, lambda bi, i: (0,) * a.ndim)
    return pl.pallas_call(
        _inproj_kernel,
        grid=(b, l // tm),
        in_specs=[tok(d),
                  pl.BlockSpec((None, 1, mod.shape[-1]), lambda bi, i: (bi, 0, 0)),
                  full(norm1_g), full(w_in_b), full(seg), full(gq_t), full(gk_t)],
        out_specs=[tok(ATTN_W), tok(ATTN_W), tok(ATTN_W), tok(CONV_W)],
        out_shape=[jax.ShapeDtypeStruct((b, l, ATTN_W), bf16)] * 3
                  + [jax.ShapeDtypeStruct((b, l, CONV_W), f32)],
        compiler_params=_params("arbitrary", "arbitrary"),
        name="inproj",
    )(x, mod, norm1_g, w_in_b, seg, gq_t, gk_t)


def _attn_kernel(q_ref, k_ref, v_ref, bias_ref, o_ref, *, rows, rg, wr, nr):
    g = pl.program_id(1)
    lane = lax.broadcasted_iota(i32, (GRID_W, LANES), 1)
    first_head = lane < HEAD_DIM
    nkeys = wr * GRID_W

    pairs = [(rr, p) for rr in range(nr) for p in range(N_HEADS // 2)]

    def rows_body(it, carry):
        geo = []
        for rr in range(nr):
            lr = it * nr + rr
            r = g * rg + lr
            rs = jnp.clip(r - wr // 2, 0, rows - wr)
            geo.append((pl.multiple_of(lr * GRID_W, GRID_W),
                        pl.multiple_of(rs * GRID_W, GRID_W),
                        rs - r + (WIN_ROWS - 1)))
        scores = []
        for rr, p in pairs:
            q0, tok0, rb = geo[rr]
            cols = slice(p * LANES, (p + 1) * LANES)
            qp = q_ref[pl.ds(q0, GRID_W), cols]
            kp = k_ref[pl.ds(tok0, nkeys), cols]
            zero = jnp.zeros_like(qp)
            q2 = jnp.concatenate([jnp.where(first_head, qp, zero),
                                  jnp.where(first_head, zero, qp)], axis=0)
            s = lax.dot_general(q2, kp, NT_DIMS, preferred_element_type=f32)
            bias = jnp.concatenate(
                [jnp.concatenate([bias_ref[2 * p + j, rb + 2 * m] for m in range(nkeys // LANES)],
                                 axis=-1) for j in range(2)], axis=0)
            scores.append(s + bias)
        probs = []
        for s in scores:
            mx = jnp.max(s, axis=-1, keepdims=True)
            e = jnp.exp(s - mx)
            probs.append((e.astype(bf16), jnp.sum(e, axis=-1, keepdims=True)))
        for (rr, p), (e, den) in zip(pairs, probs):
            q0, tok0, rb = geo[rr]
            cols = slice(p * LANES, (p + 1) * LANES)
            vp = v_ref[pl.ds(tok0, nkeys), cols]
            o = jnp.dot(e, vp, preferred_element_type=f32) / den
            o_ref[pl.ds(q0, GRID_W), cols] = jnp.where(
                first_head, o[0:GRID_W], o[GRID_W:]).astype(bf16)
        return carry

    lax.fori_loop(0, rg // nr, rows_body, 0)


def _attention(q, k, v, bias_tab):
    b, l, _ = q.shape
    rows = l // GRID_W
    wr = min(WIN_ROWS, rows)
    rg = min(8, rows)
    once = pl.Buffered(1)
    kern = functools.partial(_attn_kernel, rows=rows, rg=rg, wr=wr, nr=2)
    return pl.pallas_call(
        kern,
        grid=(b, rows // rg),
        in_specs=[pl.BlockSpec((None, rg * GRID_W, ATTN_W), lambda bi, g: (bi, g, 0)),
                  pl.BlockSpec((None, l, ATTN_W), lambda bi, g: (bi, 0, 0), pipeline_mode=once),
                  pl.BlockSpec((None, l, ATTN_W), lambda bi, g: (bi, 0, 0), pipeline_mode=once),
                  pl.BlockSpec(bias_tab.shape, lambda bi, g: (0, 0, 0, 0), pipeline_mode=once)],
        out_specs=pl.BlockSpec((None, rg * GRID_W, ATTN_W), lambda bi, g: (bi, g, 0)),
        out_shape=jax.ShapeDtypeStruct((b, l, ATTN_W), bf16),
        compiler_params=_params("arbitrary", "arbitrary"),
        name="attention",
    )(q, k, v, bias_tab)


def _bias_table(rpb):
    cols = jnp.arange(GRID_W)
    col_start = jnp.clip(cols - WIN_COLS // 2, 0, GRID_W - WIN_COLS)
    kc = jnp.arange(GRID_W)[None, :]
    valid = (kc >= col_start[:, None]) & (kc < col_start[:, None] + WIN_COLS)
    off = jnp.clip(kc - cols[:, None] + (WIN_COLS - 1), 0, 2 * WIN_COLS - 2)
    tab = jnp.where(valid[None, None], rpb[:, :, off].astype(f32), NEG)
    return jnp.concatenate([tab[:, :-1], tab[:, 1:]], axis=-1)


def _conv_kernel(u_ref, prev_ref, next_ref, w_ref, cb_ref, lg_ref, lb_ref, seg_ref, o_ref,
                 win_ref, sh_ref, *, tl, chunk):
    i = pl.program_id(1)
    n_i = pl.num_programs(1)
    halo = prev_ref.shape[0]
    sub = 8
    win_ref[0:halo, :] = jnp.where(i > 0, prev_ref[...], 0.0)
    win_ref[halo:halo + tl, :] = u_ref[...]
    win_ref[halo + tl:, :] = jnp.where(i < n_i - 1, next_ref[...], 0.0)
    sh_rows = sh_ref.shape[1]
    for r in range(1, sub):
        for k0 in range(0, sh_rows, chunk):
            k1 = min(k0 + chunk, sh_rows)
            sh_ref[r - 1, k0:k1, :] = win_ref[r + k0:r + k1, :]
    seg = seg_ref[...]
    pad = CONV_K // 2
    for c in range(tl // chunk):
        acc = jnp.zeros((chunk, CONV_W), f32)
        for j in range(CONV_K):
            start = halo - pad + c * chunk + j
            r = start % sub
            src = win_ref if r == 0 else sh_ref.at[r - 1]
            acc = acc + src[start - r:start - r + chunk, :] * w_ref[j:j + 1, :]
        y = acc + cb_ref[...]
        y_hi, y_lo = _split_bf16(y)
        mu = (jnp.dot(y_hi, seg, preferred_element_type=f32)
              + jnp.dot(y_lo, seg, preferred_element_type=f32)) * (1.0 / CONV_GROUP)
        dlt = y - mu
        d_hi, d_lo = _split_bf16(dlt * dlt)
        var = (jnp.dot(d_hi, seg, preferred_element_type=f32)
               + jnp.dot(d_lo, seg, preferred_element_type=f32)) * (1.0 / CONV_GROUP)
        un = dlt * lax.rsqrt(var + EPS) * lg_ref[...] + lb_ref[...]
        o_ref[c * chunk:(c + 1) * chunk, :] = (un * jax.nn.sigmoid(un)).astype(bf16)


def _conv(u, conv_w, conv_b, ln_g, ln_b, seg):
    b, l, _ = u.shape
    tl = min(512, l)
    halo = 16
    nh = l // halo
    kern = functools.partial(_conv_kernel, tl=tl, chunk=min(128, tl))
    full = lambda a: pl.BlockSpec(a.shape, lambda bi, i: (0,) * a.ndim)
    return pl.pallas_call(
        kern,
        grid=(b, l // tl),
        in_specs=[pl.BlockSpec((None, tl, CONV_W), lambda bi, i: (bi, i, 0)),
                  pl.BlockSpec((None, halo, CONV_W),
                               lambda bi, i: (bi, jnp.maximum(i * (tl // halo) - 1, 0), 0)),
                  pl.BlockSpec((None, halo, CONV_W),
                               lambda bi, i: (bi, jnp.minimum((i + 1) * (tl // halo), nh - 1), 0)),
                  full(conv_w), full(conv_b), full(ln_g), full(ln_b), full(seg)],
        out_specs=pl.BlockSpec((None, tl, CONV_W), lambda bi, i: (bi, i, 0)),
        out_shape=jax.ShapeDtypeStruct((b, l, CONV_W), bf16),
        scratch_shapes=[pltpu.VMEM((tl + 2 * halo, CONV_W), f32),
                        pltpu.VMEM((7, tl + 2 * halo - 8, CONV_W), f32)],
        compiler_params=_params("arbitrary", "arbitrary"),
        name="conv",
    )(u, u, u, conv_w, conv_b, ln_g, ln_b, seg)


def _outproj_kernel(at_ref, cv_ref, x_ref, mod_ref, wo_ref, g2_ref, wr_ref,
                    x1_ref, h2_ref, aff_ref, *, nparts):
    tm, d = x_ref.shape
    rpt = d // LANES
    gate1 = mod_ref[:, 2 * d:3 * d]
    shift2 = mod_ref[:, 3 * d:4 * d]
    scale2 = mod_ref[:, 4 * d:5 * d]
    w_hi, w_lo = _split_bf16(wr_ref[...])
    parts = [slice(p * (tm // nparts), (p + 1) * (tm // nparts)) for p in range(nparts)]
    mixes = [jnp.dot(at_ref[rows, :], wo_ref[0:ATTN_W, :], preferred_element_type=f32)
             + jnp.dot(cv_ref[rows, :], wo_ref[ATTN_W:, :], preferred_element_type=f32)
             for rows in parts]
    splits = []
    for rows, mix in zip(parts, mixes):
        x1 = x_ref[rows, :] + gate1 * mix
        x1_ref[rows, :] = x1
        ms = jnp.mean(x1 * x1, axis=-1, keepdims=True)
        h2 = x1 * lax.rsqrt(ms + EPS) * g2_ref[...]
        h2 = h2 * (1.0 + scale2) + shift2
        h_hi, h_lo = _split_bf16(h2)
        h_rows = h_hi.astype(f32)
        nrows = rows.stop - rows.start
        for c in range(rpt):
            h2_ref[pl.ds(rows.start * rpt + c, nrows, stride=rpt), :] = h_rows[:, c * LANES:(c + 1) * LANES]
        splits.append((h_hi, h_lo))
    lgs = [jnp.dot(h_hi, w_hi, preferred_element_type=f32)
           + jnp.dot(h_lo, w_hi, preferred_element_type=f32)
           + jnp.dot(h_hi, w_lo, preferred_element_type=f32) for h_hi, h_lo in splits]
    for rows, lg in zip(parts, lgs):
        logits = lg.T[0:aff_ref.shape[0], :]
        mx = jnp.max(logits, axis=0, keepdims=True)
        ex = jnp.exp(logits - mx)
        aff_ref[:, rows] = ex / jnp.sum(ex, axis=0, keepdims=True)


def _outproj(attn, conv, x, mod, w_out_b, norm2_g, w_router):
    b, l, d = x.shape
    tm = min(512, l)
    nt = l // tm
    e = w_router.shape[1]
    w_router_t = jnp.pad(w_router.astype(f32), ((0, 0), (0, LANES - e)))
    tok = lambda w: pl.BlockSpec((None, tm, w), lambda bi, i: (bi, i, 0))
    full = lambda a: pl.BlockSpec(a.shape, lambda bi, i: (0,) * a.ndim)
    return pl.pallas_call(
        functools.partial(_outproj_kernel, nparts=2),
        grid=(b, nt),
        in_specs=[tok(ATTN_W), tok(CONV_W), tok(d),
                  pl.BlockSpec((None, 1, mod.shape[-1]), lambda bi, i: (bi, 0, 0)),
                  full(w_out_b), full(norm2_g), full(w_router_t)],
        out_specs=[tok(d),
                   pl.BlockSpec((tm * (d // LANES), LANES), lambda bi, i: (bi * nt + i, 0)),
                   pl.BlockSpec((e, tm), lambda bi, i: (0, bi * nt + i))],
        out_shape=[jax.ShapeDtypeStruct((b, l, d), f32),
                   jax.ShapeDtypeStruct((b * l * (d // LANES), LANES), f32),
                   jax.ShapeDtypeStruct((e, b * l), f32)],
        compiler_params=_params("arbitrary", "arbitrary"),
        name="outproj",
    )(attn, conv, x, mod, w_out_b, norm2_g, w_router_t)


def _route_kernel(aff_ref, cs_ref, tst_ref, cbk_ref, *, cap, tsc, tbc):
    a = aff_ref[...]
    ne, nc, _ = a.shape

    def enough(v):
        return jnp.sum((a >= v).astype(f32), axis=(1, 2), keepdims=True) >= cap

    def pow2(k):
        return lax.bitcast_convert_type(jnp.left_shift(k + 127, 23), f32)

    def exp_body(_, st):
        klo, khi = st
        kmid = jnp.right_shift(klo + khi, 1)
        ok = enough(pow2(kmid))
        return jnp.where(ok, kmid, klo), jnp.where(ok, khi, kmid)

    klo, khi = lax.fori_loop(0, 7, exp_body,
                             (jnp.full((ne, 1, 1), -127, i32), jnp.full((ne, 1, 1), 1, i32)))

    def lin_body(_, st):
        lo, hi = st
        mid = lo + (hi - lo) * 0.5
        ok = enough(mid)
        return jnp.where(ok, mid, lo), jnp.where(ok, hi, mid)

    lo, _ = lax.fori_loop(0, 48, lin_body, (pow2(klo), pow2(khi)))
    thr = jnp.min(jnp.where(a >= lo, a, jnp.inf), axis=(1, 2), keepdims=True)

    r_i = lax.broadcasted_iota(i32, (LANES, LANES), 0)
    c_i = lax.broadcasted_iota(i32, (LANES, LANES), 1)
    upper = (r_i <= c_i).astype(bf16)
    ones = jnp.ones((LANES, LANES), bf16)
    rr = lax.broadcasted_iota(i32, (nc, nc), 0)
    cc = lax.broadcasted_iota(i32, (nc, nc), 1)
    lower = (cc < rr).astype(bf16)
    blk_r = lax.broadcasted_iota(i32, (nc, LANES), 0)
    blk_c = lax.broadcasted_iota(i32, (nc, LANES), 1)
    before_blk = (blk_r < blk_c * (tbc // LANES)).astype(bf16)
    ones8 = jnp.ones((8, LANES), bf16)
    lane8 = lax.broadcasted_iota(i32, (1, LANES), 1)

    def cumsum(m):
        within = jnp.dot(m, upper, preferred_element_type=f32)
        prev = jnp.dot(lower, m, preferred_element_type=f32)
        offs = jnp.dot(prev.astype(bf16), ones, preferred_element_type=f32)
        return within + offs

    for e in range(ne):
        a_e = a[e]
        thr_e = thr[e]
        gt_e = a_e > thr_e
        eq_e = (a_e == thr_e).astype(f32)
        need = cap - jnp.sum(gt_e.astype(f32), keepdims=True)
        rank_eq = cumsum(eq_e.astype(bf16)) - eq_e
        sel = jnp.logical_or(gt_e, jnp.logical_and(eq_e > 0.0, rank_eq < need))
        sel_b = sel.astype(bf16)
        c_incl = cumsum(sel_b)
        cs_ref[e] = jnp.where(sel, c_incl, 0.0).astype(i32)
        tst = jnp.zeros((1, LANES), f32)
        for j in range(cap // tsc + 1):
            cnt_j = jnp.sum((c_incl <= float(j * tsc)).astype(f32), keepdims=True)
            tst = jnp.where(lane8 == j, cnt_j, tst)
        tst_ref[e:e + 1, :] = tst.astype(i32)
        per_chunk = lax.dot_general(ones8, sel_b, NT_DIMS, preferred_element_type=f32)
        cbk = jnp.dot(per_chunk.astype(bf16), before_blk, preferred_element_type=f32)
        cbk_ref[e:e + 1, :] = cbk[0:1].astype(i32)


def _route(aff_t, cap, tsc, tbc):
    e, n = aff_t.shape
    nc = n // LANES
    kern = functools.partial(_route_kernel, cap=cap, tsc=tsc, tbc=tbc)
    return pl.pallas_call(
        kern,
        out_shape=[jax.ShapeDtypeStruct((e, nc, LANES), i32),
                   jax.ShapeDtypeStruct((e, LANES), i32),
                   jax.ShapeDtypeStruct((e, LANES), i32)],
        compiler_params=pltpu.CompilerParams(vmem_limit_bytes=VMEM_LIMIT),
        name="route",
    )(aff_t.reshape(e, nc, LANES))


def _compact_kernel(tst_ref, cs_ref, aff_ref, idx_ref, g_ref, *, tsc, tbc):
    e = pl.program_id(0)
    j = pl.program_id(1)
    shift = tbc.bit_length() - 1
    f_lo = tst_ref[e * LANES + j]
    f_hi = tst_ref[e * LANES + j + 1]
    b_lo = jnp.right_shift(f_lo, shift)
    b_hi = jnp.right_shift(f_hi - 1, shift)
    slot1 = j * tsc + 1 + lax.broadcasted_iota(i32, (tsc, tbc), 0)
    sub = lax.broadcasted_iota(i32, (8, tbc), 0)
    lane = lax.broadcasted_iota(i32, (8, tbc), 1)

    def body(b, acc):
        row = cs_ref[e, pl.ds(b, 1), :]
        a = aff_ref[e, pl.ds(b, 1), :]
        onehot_t = (row == slot1).astype(f32).astype(bf16)
        t = b * tbc + lane
        a_hi = a.astype(bf16).astype(f32)
        r1 = a - a_hi
        a_mid = r1.astype(bf16).astype(f32)
        a_lo = r1 - a_mid
        pay = jnp.where(sub == 0, jnp.bitwise_and(t, 255).astype(f32),
              jnp.where(sub == 1, jnp.right_shift(t, 8).astype(f32),
              jnp.where(sub == 2, a_hi,
              jnp.where(sub == 3, a_mid,
              jnp.where(sub == 4, a_lo, 0.0)))))
        return acc + lax.dot_general(pay.astype(bf16), onehot_t, NT_DIMS, preferred_element_type=f32)

    acc = lax.fori_loop(b_lo, b_hi + 1, body, jnp.zeros((8, tsc), f32))
    idx = acc[1:2] * 256.0 + acc[0:1]
    idx_ref[...] = jnp.broadcast_to(idx, (8, tsc)).astype(i32)
    g_ref[...] = jnp.broadcast_to(acc[2:3] + acc[3:4] + acc[4:5], (8, tsc))


def _compact(tst, cs, aff_t, cap, tsc, tbc):
    e, n = aff_t.shape
    nb = n // tbc
    ntile = cap // tsc
    kern = functools.partial(_compact_kernel, tsc=tsc, tbc=tbc)
    res = pl.BlockSpec((e, nb, tbc), lambda ei, j, tst_r: (0, 0, 0))
    out = pl.BlockSpec((None, None, 8, tsc), lambda ei, j, tst_r: (ei, j, 0, 0))
    idx8, g8 = pl.pallas_call(
        kern,
        grid_spec=pltpu.PrefetchScalarGridSpec(
            num_scalar_prefetch=1, grid=(e, ntile), in_specs=[res, res], out_specs=[out, out]),
        out_shape=[jax.ShapeDtypeStruct((e, ntile, 8, tsc), i32),
                   jax.ShapeDtypeStruct((e, ntile, 8, tsc), f32)],
        compiler_params=_params("arbitrary", "arbitrary"),
        name="compact",
    )(tst.reshape(-1), cs.reshape(e, nb, tbc), aff_t.reshape(e, nb, tbc))
    return idx8[:, :, 0, :].reshape(e, cap), g8[:, :, 0, :].reshape(e, cap)


SC_CORES = 2
SC_SUBCORES = 16
SC_CHUNK = 32


def _sc_gather(h_tiles, idx, rpt):
    slots = idx.shape[0]
    workers = SC_CORES * SC_SUBCORES
    per_w = slots // workers
    assert per_w * workers == slots and per_w % SC_CHUNK == 0
    table = h_tiles.reshape(-1, rpt, LANES)
    mesh = plsc.VectorSubcoreMesh(core_axis_name="c", subcore_axis_name="s",
                                  num_cores=SC_CORES, num_subcores=SC_SUBCORES)

    @functools.partial(
        pl.kernel, mesh=mesh,
        out_type=jax.ShapeDtypeStruct((slots, rpt, LANES), f32),
        scratch_types=[pltpu.VMEM((per_w,), i32),
                       pltpu.VMEM((SC_CHUNK, rpt, LANES), f32),
                       pltpu.SemaphoreType.DMA],
        name="sc_gather")
    def gather(table_hbm, idx_hbm, out_hbm, idx_v, rows_v, sem):
        wid = lax.axis_index("s") * SC_CORES + lax.axis_index("c")
        base = wid * per_w
        pltpu.sync_copy(idx_hbm.at[pl.ds(base, per_w)], idx_v)

        @pl.loop(0, per_w // SC_CHUNK)
        def _(c):
            off = pl.multiple_of(c * SC_CHUNK, SC_CHUNK)
            pltpu.async_copy(table_hbm.at[idx_v.at[pl.ds(off, SC_CHUNK)]], rows_v, sem).wait()
            pltpu.sync_copy(rows_v, out_hbm.at[pl.ds(base + off, SC_CHUNK)])

    return gather(table, idx).reshape(slots * rpt, LANES)


def _expert_kernel(x_ref, g_ref, wg_ref, wu_ref, wd_ref, y_ref, xb_ref, acc_ref, gcol_ref):
    f = pl.program_id(2)

    @pl.when(f == 0)
    def _():
        tm, d = xb_ref.shape
        rpt = d // LANES
        for c in range(rpt):
            xb_ref[:, c * LANES:(c + 1) * LANES] = x_ref[pl.ds(c, tm, stride=rpt), :].astype(bf16)
        acc_ref[...] = jnp.zeros_like(acc_ref)
        gcol_ref[...] = jnp.broadcast_to(g_ref[...], (LANES, g_ref.shape[-1])).T

    xb = xb_ref[...]
    hg = jnp.dot(xb, wg_ref[...].astype(bf16), preferred_element_type=f32)
    hu = jnp.dot(xb, wu_ref[...].astype(bf16), preferred_element_type=f32)
    hid = (hg * jax.nn.sigmoid(hg) * hu).astype(bf16)
    acc_ref[...] += jnp.dot(hid, wd_ref[...].astype(bf16), preferred_element_type=f32)

    @pl.when(f == pl.num_programs(2) - 1)
    def _():
        gcol = gcol_ref[...]
        d = acc_ref.shape[-1]
        for c in range(d // LANES):
            cols = slice(c * LANES, (c + 1) * LANES)
            y_ref[:, cols] = (acc_ref[:, cols] * gcol).astype(bf16)


def _experts(xe, g_rows, wg_b, wu_b, wd_b, tm, fc):
    e, d, fdim = wg_b.shape
    rpt = d // LANES
    rows = xe.shape[0] // rpt
    per_e = rows // e
    nt = per_e // tm
    return pl.pallas_call(
        _expert_kernel,
        grid=(e, nt, fdim // fc),
        in_specs=[pl.BlockSpec((tm * rpt, LANES), lambda ei, t, f: (ei * nt + t, 0)),
                  pl.BlockSpec((None, 1, tm), lambda ei, t, f: (ei * nt + t, 0, 0)),
                  pl.BlockSpec((None, d, fc), lambda ei, t, f: (ei, 0, f)),
                  pl.BlockSpec((None, d, fc), lambda ei, t, f: (ei, 0, f)),
                  pl.BlockSpec((None, fc, d), lambda ei, t, f: (ei, f, 0))],
        out_specs=pl.BlockSpec((tm, d), lambda ei, t, f: (ei * nt + t, 0)),
        out_shape=jax.ShapeDtypeStruct((rows, d), bf16),
        scratch_shapes=[pltpu.VMEM((tm, d), bf16), pltpu.VMEM((tm, d), f32),
                        pltpu.VMEM((tm, LANES), f32)],
        compiler_params=_params("arbitrary", "arbitrary", "arbitrary"),
        name="experts",
    )(xe, g_rows.reshape(rows // tm, 1, tm), wg_b, wu_b, wd_b)


def _combine_kernel(cbk_ref, cs_ref, x1_ref, mod_ref, ye_ref, o_ref, win_ref, xwin_ref, sem, xsem,
                    acc_ref, *, cap, tbc, win, ne, total_rows):
    blk = pl.program_id(0)
    nblk = pl.num_programs(0)
    d = acc_ref.shape[-1]
    slot = blk % 2

    def region(ee):
        return ee * cap

    def first_row(bb, ee):
        st = cbk_ref[ee * LANES + bb]
        row = region(ee) + jnp.left_shift(jnp.right_shift(st, 4), 4)
        return pl.multiple_of(jnp.minimum(row, total_rows - win), BF16_ROWS)

    def start_block(bb, sl):
        for ee in range(ne):
            pltpu.make_async_copy(ye_ref.at[pl.ds(first_row(bb, ee), win)],
                                  win_ref.at[sl, pl.ds(ee * win, win)], sem.at[sl]).start()

    @pl.when(blk == 0)
    def _():
        start_block(blk, slot)

    @pl.when(blk + 1 < nblk)
    def _():
        start_block(blk + 1, (blk + 1) % 2)

    pltpu.make_async_copy(ye_ref.at[pl.ds(0, ne * win)], win_ref.at[slot], sem.at[slot]).wait()

    w_iota = lax.broadcasted_iota(i32, (win, tbc), 0)

    def onehot_of(cs_row, base_slot, done_slots):
        hit = jnp.logical_and(cs_row == base_slot + 1 + w_iota, cs_row > done_slots)
        return hit.astype(f32).T.astype(bf16)

    onehot = jnp.concatenate(
        [onehot_of(cs_ref[ee:ee + 1, :], first_row(blk, ee) - region(ee), 0) for ee in range(ne)],
        axis=1)
    acc_ref[...] = jnp.dot(onehot, win_ref[slot], preferred_element_type=f32)

    def extra_expert(ee, carry):
        row0 = first_row(blk, ee)
        base0 = row0 - region(ee)
        end = cbk_ref[ee * LANES + blk + 1]
        cs_row = cs_ref[pl.ds(ee, 1), :]

        def extra(w, c2):
            row_w = pl.multiple_of(jnp.minimum(row0 + w * win, total_rows - win), BF16_ROWS)
            cp = pltpu.make_async_copy(ye_ref.at[pl.ds(row_w, win)], xwin_ref, xsem)
            cp.start()
            cp.wait()
            oh = onehot_of(cs_row, row_w - region(ee), base0 + w * win)
            acc_ref[...] += jnp.dot(oh, xwin_ref[...], preferred_element_type=f32)
            return c2

        lax.fori_loop(1, (end - base0 + win - 1) // win, extra, 0)
        return carry

    lax.fori_loop(0, ne, extra_expert, 0)

    gate2 = mod_ref[:, 5 * d:6 * d]
    o_ref[...] = x1_ref[...] + gate2 * acc_ref[...]


def _combine(cbk, cs, x1, mod, ye, cap, tbc):
    b, l, d = x1.shape
    n = b * l
    e = cs.shape[0]
    nb = n // tbc
    win = min(128, cap)
    per_b = l // tbc
    kern = functools.partial(_combine_kernel, cap=cap, tbc=tbc, win=win, ne=e,
                             total_rows=ye.shape[0])
    return pl.pallas_call(
        kern,
        grid_spec=pltpu.PrefetchScalarGridSpec(
            num_scalar_prefetch=1, grid=(nb,),
            in_specs=[pl.BlockSpec((e, tbc), lambda bk, c: (0, bk)),
                      pl.BlockSpec((tbc, d), lambda bk, c: (bk, 0)),
                      pl.BlockSpec((None, 1, mod.shape[-1]), lambda bk, c: (bk // per_b, 0, 0)),
                      pl.BlockSpec(memory_space=pl.ANY)],
            out_specs=pl.BlockSpec((tbc, d), lambda bk, c: (bk, 0)),
            scratch_shapes=[pltpu.VMEM((2, e * win, d), bf16), pltpu.VMEM((win, d), bf16),
                            pltpu.SemaphoreType.DMA((2,)), pltpu.SemaphoreType.DMA,
                            pltpu.VMEM((tbc, d), f32)]),
        out_shape=jax.ShapeDtypeStruct((n, d), f32),
        compiler_params=_params("arbitrary"),
        name="combine",
    )(cbk.reshape(-1), cs, x1.reshape(n, d), mod, ye).reshape(b, l, d)


TSC = 512
TBC = 1024
TBF = 512
TM_EXPERT = 1024
FC_EXPERT = 256


def kernel(x_prompt, x_sample, c_prompt, c_sample, ada_w, ada_b, norm1_g, w_in, q_norm_g, k_norm_g, rpb,
           conv_w, conv_b, conv_ln_g, conv_ln_b, w_out, norm2_g, w_router, w_gate, w_up, w_down):
    xs = (x_prompt, x_sample)
    d = x_prompt.shape[-1]
    ne = w_router.shape[1]
    row = lambda a: a.reshape(1, -1).astype(f32)

    nb0, nb1 = c_prompt.shape[0], c_sample.shape[0]
    pad = (-(nb0 + nb1)) % 8
    c_all = jnp.concatenate([c_prompt, c_sample, jnp.zeros((pad, d), f32)], axis=0)
    mod_all = _ada(c_all, ada_w, ada_b)
    mods = (mod_all[:nb0].reshape(nb0, 1, -1), mod_all[nb0:nb0 + nb1].reshape(nb1, 1, -1))

    seg_i = jnp.arange(ATTN_W) // HEAD_DIM
    seg = (seg_i[:, None] == seg_i[None, :]).astype(bf16)
    gq_t = jnp.tile(q_norm_g.astype(f32), N_HEADS).reshape(1, ATTN_W)
    gk_t = jnp.tile(k_norm_g.astype(f32), N_HEADS).reshape(1, ATTN_W)
    bias_tab = _bias_table(rpb)
    w_in_b = w_in.astype(bf16)
    w_out_b = w_out.astype(bf16)

    staged = []
    for x, mod in zip(xs, mods):
        b, l, _ = x.shape
        n = b * l
        cap = max(1, EC_FACTOR * n // ne)
        q, k, v, u = _inproj(x, mod, row(norm1_g), w_in_b, seg, gq_t, gk_t)
        attn = _attention(q, k, v, bias_tab)
        conv = _conv(u, conv_w.astype(f32), row(conv_b), row(conv_ln_g), row(conv_ln_b), seg)
        x1, h2, aff_t = _outproj(attn, conv, x, mod, w_out_b, row(norm2_g), w_router)
        cs, tst, cbk = _route(aff_t, cap, TSC, TBF)
        idx, g = _compact(tst, cs, aff_t, cap, TSC, TBC)
        xe = _sc_gather(h2, idx.reshape(-1), d // LANES)
        staged.append((x1, mod, xe, g, cs.reshape(ne, n), cbk, cap))

    outs = []
    for x1, mod, xe, g, cs, cbk, cap in staged:
        ye = _experts(xe, g, w_gate, w_up, w_down, min(TM_EXPERT, cap), min(FC_EXPERT, w_gate.shape[2]))
        outs.append(_combine(cbk, cs, x1, mod, ye, cap, TBF))
    return tuple(outs)
```

```python
import functools

import jax
import jax.numpy as jnp
from jax import lax
from jax.experimental import pallas as pl
from jax.experimental.pallas import tpu as pltpu
from jax.experimental.pallas import tpu_sc as plsc

f32 = jnp.float32
bf16 = jnp.bfloat16
i32 = jnp.int32

GRID_W = 64
N_HEADS = 8
HEAD_DIM = 64
ATTN_W = N_HEADS * HEAD_DIM
CONV_W = 512
CONV_K = 31
CONV_GROUP = 64
WIN_ROWS = 8
WIN_COLS = 16
N_EXPERTS = 16
EC_FACTOR = 2
N_MOD = 6
EPS = 1e-6
NEG = -1e30

LANES = 128
BF16_ROWS = 16
VMEM_LIMIT = 56 * 1024 * 1024

NT_DIMS = (((1,), (1,)), ((), ()))


def _params(*sem):
    return pltpu.CompilerParams(dimension_semantics=sem, vmem_limit_bytes=VMEM_LIMIT)


def _split_bf16(x):
    hi = x.astype(bf16)
    lo = (x - hi.astype(f32)).astype(bf16)
    return hi, lo


def _ada_kernel(c_ref, w_ref, b_ref, o_ref):
    c = c_ref[...]
    s = c * jax.nn.sigmoid(c)
    s_hi, s_lo = _split_bf16(s)
    w_hi, w_lo = _split_bf16(w_ref[...])
    acc = jnp.dot(s_hi, w_hi, preferred_element_type=f32)
    acc += jnp.dot(s_hi, w_lo, preferred_element_type=f32)
    acc += jnp.dot(s_lo, w_hi, preferred_element_type=f32)
    o_ref[...] = acc + b_ref[...]


def _ada(c_all, ada_w, ada_b):
    rows, d = c_all.shape
    ncols = ada_w.shape[1]
    tn = next(t for t in (1024, 512, 256, 128) if ncols % t == 0)
    return pl.pallas_call(
        _ada_kernel,
        grid=(ncols // tn,),
        in_specs=[pl.BlockSpec((rows, d), lambda j: (0, 0)),
                  pl.BlockSpec((d, tn), lambda j: (0, j)),
                  pl.BlockSpec((1, tn), lambda j: (0, j))],
        out_specs=pl.BlockSpec((rows, tn), lambda j: (0, j)),
        out_shape=jax.ShapeDtypeStruct((rows, ncols), f32),
        compiler_params=_params("arbitrary"),
        name="ada",
    )(c_all, ada_w, ada_b.reshape(1, ncols))


def _inproj_kernel(x_ref, mod_ref, g1_ref, w_ref, seg_ref, gq_ref, gk_ref,
                   q_ref, k_ref, v_ref, u_ref):
    x = x_ref[...]
    d = x.shape[-1]
    ms = jnp.mean(x * x, axis=-1, keepdims=True)
    h = x * lax.rsqrt(ms + EPS) * g1_ref[...]
    shift = mod_ref[:, 0:d]
    scale = mod_ref[:, d:2 * d]
    h = h * (1.0 + scale) + shift
    z = jnp.dot(h.astype(bf16), w_ref[...], preferred_element_type=f32)
    zq = z[:, 0:ATTN_W]
    zk = z[:, ATTN_W:2 * ATTN_W]
    seg = seg_ref[...]
    ssq_q = jnp.dot((zq * zq).astype(bf16), seg, preferred_element_type=f32)
    ssq_k = jnp.dot((zk * zk).astype(bf16), seg, preferred_element_type=f32)
    qs = gq_ref[...] * gk_ref[...] * (HEAD_DIM ** -0.5)
    q_ref[...] = (zq * lax.rsqrt(ssq_q * (1.0 / HEAD_DIM) + EPS) * qs).astype(bf16)
    k_ref[...] = (zk * lax.rsqrt(ssq_k * (1.0 / HEAD_DIM) + EPS)).astype(bf16)
    v_ref[...] = z[:, 2 * ATTN_W:3 * ATTN_W].astype(bf16)
    a = z[:, 3 * ATTN_W:3 * ATTN_W + CONV_W]
    gt = z[:, 3 * ATTN_W + CONV_W:]
    u_ref[...] = a * jax.nn.sigmoid(gt)


def _inproj(x, mod, norm1_g, w_in_b, seg, gq_t, gk_t):
    b, l, d = x.shape
    tm = min(512, l)
    ncols = w_in_b.shape[1]
    tok = lambda w: pl.BlockSpec((None, tm, w), lambda bi, i: (bi, i, 0))
    full = lambda a: pl.BlockSpec(a.shape, lambda bi, i: (0,) * a.ndim)
    return pl.pallas_call(
        _inproj_kernel,
        grid=(b, l // tm),
        in_specs=[tok(d),
                  pl.BlockSpec((None, 1, mod.shape[-1]), lambda bi, i: (bi, 0, 0)),
                  full(norm1_g), full(w_in_b), full(seg), full(gq_t), full(gk_t)],
        out_specs=[tok(ATTN_W), tok(ATTN_W), tok(ATTN_W), tok(CONV_W)],
        out_shape=[jax.ShapeDtypeStruct((b, l, ATTN_W), bf16)] * 3
                  + [jax.ShapeDtypeStruct((b, l, CONV_W), f32)],
        compiler_params=_params("arbitrary", "arbitrary"),
        name="inproj",
    )(x, mod, norm1_g, w_in_b, seg, gq_t, gk_t)


def _attn_kernel(q_ref, k_ref, v_ref, bias_ref, o_ref, *, rows, rg, wr, nr):
    g = pl.program_id(1)
    lane = lax.broadcasted_iota(i32, (GRID_W, LANES), 1)
    first_head = lane < HEAD_DIM
    nkeys = wr * GRID_W

    pairs = [(rr, p) for rr in range(nr) for p in range(N_HEADS // 2)]

    def rows_body(it, carry):
        geo = []
        for rr in range(nr):
            lr = it * nr + rr
            r = g * rg + lr
            rs = jnp.clip(r - wr // 2, 0, rows - wr)
            geo.append((pl.multiple_of(lr * GRID_W, GRID_W),
                        pl.multiple_of(rs * GRID_W, GRID_W),
                        rs - r + (WIN_ROWS - 1)))
        scores = []
        for rr, p in pairs:
            q0, tok0, rb = geo[rr]
            cols = slice(p * LANES, (p + 1) * LANES)
            qp = q_ref[pl.ds(q0, GRID_W), cols]
            kp = k_ref[pl.ds(tok0, nkeys), cols]
            zero = jnp.zeros_like(qp)
            q2 = jnp.concatenate([jnp.where(first_head, qp, zero),
                                  jnp.where(first_head, zero, qp)], axis=0)
            s = lax.dot_general(q2, kp, NT_DIMS, preferred_element_type=f32)
            bias = jnp.concatenate(
                [jnp.concatenate([bias_ref[2 * p + j, rb + 2 * m] for m in range(nkeys // LANES)],
                                 axis=-1) for j in range(2)], axis=0)
            scores.append(s + bias)
        probs = []
        for s in scores:
            mx = jnp.max(s, axis=-1, keepdims=True)
            e = jnp.exp(s - mx)
            probs.append((e.astype(bf16), jnp.sum(e, axis=-1, keepdims=True)))
        for (rr, p), (e, den) in zip(pairs, probs):
            q0, tok0, rb = geo[rr]
            cols = slice(p * LANES, (p + 1) * LANES)
            vp = v_ref[pl.ds(tok0, nkeys), cols]
            o = jnp.dot(e, vp, preferred_element_type=f32) / den
            o_ref[pl.ds(q0, GRID_W), cols] = jnp.where(
                first_head, o[0:GRID_W], o[GRID_W:]).astype(bf16)
        return carry

    lax.fori_loop(0, rg // nr, rows_body, 0)


def _attention(q, k, v, bias_tab):
    b, l, _ = q.shape
    rows = l // GRID_W
    wr = min(WIN_ROWS, rows)
    rg = min(8, rows)
    once = pl.Buffered(1)
    kern = functools.partial(_attn_kernel, rows=rows, rg=rg, wr=wr, nr=4)
    return pl.pallas_call(
        kern,
        grid=(b, rows // rg),
        in_specs=[pl.BlockSpec((None, rg * GRID_W, ATTN_W), lambda bi, g: (bi, g, 0)),
                  pl.BlockSpec((None, l, ATTN_W), lambda bi, g: (bi, 0, 0), pipeline_mode=once),
                  pl.BlockSpec((None, l, ATTN_W), lambda bi, g: (bi, 0, 0), pipeline_mode=once),
                  pl.BlockSpec(bias_tab.shape, lambda bi, g: (0, 0, 0, 0), pipeline_mode=once)],
        out_specs=pl.BlockSpec((None, rg * GRID_W, ATTN_W), lambda bi, g: (bi, g, 0)),
        out_shape=jax.ShapeDtypeStruct((b, l, ATTN_W), bf16),
        compiler_params=_params("arbitrary", "arbitrary"),
        name="attention",
    )(q, k, v, bias_tab)


def _bias_table(rpb):
    cols = jnp.arange(GRID_W)
    col_start = jnp.clip(cols - WIN_COLS // 2, 0, GRID_W - WIN_COLS)
    kc = jnp.arange(GRID_W)[None, :]
    valid = (kc >= col_start[:, None]) & (kc < col_start[:, None] + WIN_COLS)
    off = jnp.clip(kc - cols[:, None] + (WIN_COLS - 1), 0, 2 * WIN_COLS - 2)
    tab = jnp.where(valid[None, None], rpb[:, :, off].astype(f32), NEG)
    return jnp.concatenate([tab[:, :-1], tab[:, 1:]], axis=-1)


def _conv_kernel(u_ref, prev_ref, next_ref, w_ref, cb_ref, lg_ref, lb_ref, seg_ref, o_ref,
                 win_ref, sh_ref, *, tl, chunk):
    i = pl.program_id(1)
    n_i = pl.num_programs(1)
    halo = prev_ref.shape[0]
    sub = 8
    win_ref[0:halo, :] = jnp.where(i > 0, prev_ref[...], 0.0)
    win_ref[halo:halo + tl, :] = u_ref[...]
    win_ref[halo + tl:, :] = jnp.where(i < n_i - 1, next_ref[...], 0.0)
    sh_rows = sh_ref.shape[1]
    for r in range(1, sub):
        for k0 in range(0, sh_rows, chunk):
            k1 = min(k0 + chunk, sh_rows)
            sh_ref[r - 1, k0:k1, :] = win_ref[r + k0:r + k1, :]
    seg = seg_ref[...]
    pad = CONV_K // 2
    for c in range(tl // chunk):
        acc = jnp.zeros((chunk, CONV_W), f32)
        for j in range(CONV_K):
            start = halo - pad + c * chunk + j
            r = start % sub
            src = win_ref if r == 0 else sh_ref.at[r - 1]
            acc = acc + src[start - r:start - r + chunk, :] * w_ref[j:j + 1, :]
        y = acc + cb_ref[...]
        y_hi, y_lo = _split_bf16(y)
        mu = (jnp.dot(y_hi, seg, preferred_element_type=f32)
              + jnp.dot(y_lo, seg, preferred_element_type=f32)) * (1.0 / CONV_GROUP)
        dlt = y - mu
        d_hi, d_lo = _split_bf16(dlt * dlt)
        var = (jnp.dot(d_hi, seg, preferred_element_type=f32)
               + jnp.dot(d_lo, seg, preferred_element_type=f32)) * (1.0 / CONV_GROUP)
        un = dlt * lax.rsqrt(var + EPS) * lg_ref[...] + lb_ref[...]
        o_ref[c * chunk:(c + 1) * chunk, :] = (un * jax.nn.sigmoid(un)).astype(bf16)


def _conv(u, conv_w, conv_b, ln_g, ln_b, seg):
    b, l, _ = u.shape
    tl = min(512, l)
    halo = 16
    nh = l // halo
    kern = functools.partial(_conv_kernel, tl=tl, chunk=min(128, tl))
    full = lambda a: pl.BlockSpec(a.shape, lambda bi, i: (0,) * a.ndim)
    return pl.pallas_call(
        kern,
        grid=(b, l // tl),
        in_specs=[pl.BlockSpec((None, tl, CONV_W), lambda bi, i: (bi, i, 0)),
                  pl.BlockSpec((None, halo, CONV_W),
                               lambda bi, i: (bi, jnp.maximum(i * (tl // halo) - 1, 0), 0)),
                  pl.BlockSpec((None, halo, CONV_W),
                               lambda bi, i: (bi, jnp.minimum((i + 1) * (tl // halo), nh - 1), 0)),
                  full(conv_w), full(conv_b), full(ln_g), full(ln_b), full(seg)],
        out_specs=pl.BlockSpec((None, tl, CONV_W), lambda bi, i: (bi, i, 0)),
        out_shape=jax.ShapeDtypeStruct((b, l, CONV_W), bf16),
        scratch_shapes=[pltpu.VMEM((tl + 2 * halo, CONV_W), f32),
                        pltpu.VMEM((7, tl + 2 * halo - 8, CONV_W), f32)],
        compiler_params=_params("arbitrary", "arbitrary"),
        name="conv",
    )(u, u, u, conv_w, conv_b, ln_g, ln_b, seg)


def _outproj_kernel(at_ref, cv_ref, x_ref, mod_ref, wo_ref, g2_ref, wr_ref,
                    x1_ref, h2_ref, aff_ref, *, nparts):
    tm, d = x_ref.shape
    rpt = d // LANES
    gate1 = mod_ref[:, 2 * d:3 * d]
    shift2 = mod_ref[:, 3 * d:4 * d]
    scale2 = mod_ref[:, 4 * d:5 * d]
    w_hi, w_lo = _split_bf16(wr_ref[...])
    parts = [slice(p * (tm // nparts), (p + 1) * (tm // nparts)) for p in range(nparts)]
    mixes = [jnp.dot(at_ref[rows, :], wo_ref[0:ATTN_W, :], preferred_element_type=f32)
             + jnp.dot(cv_ref[rows, :], wo_ref[ATTN_W:, :], preferred_element_type=f32)
             for rows in parts]
    splits = []
    for rows, mix in zip(parts, mixes):
        x1 = x_ref[rows, :] + gate1 * mix
        x1_ref[rows, :] = x1
        ms = jnp.mean(x1 * x1, axis=-1, keepdims=True)
        h2 = x1 * lax.rsqrt(ms + EPS) * g2_ref[...]
        h2 = h2 * (1.0 + scale2) + shift2
        h_hi, h_lo = _split_bf16(h2)
        h_rows = h_hi.astype(f32)
        nrows = rows.stop - rows.start
        for c in range(rpt):
            h2_ref[pl.ds(rows.start * rpt + c, nrows, stride=rpt), :] = h_rows[:, c * LANES:(c + 1) * LANES]
        splits.append((h_hi, h_lo))
    lgs = [jnp.dot(h_hi, w_hi, preferred_element_type=f32)
           + jnp.dot(h_lo, w_hi, preferred_element_type=f32)
           + jnp.dot(h_hi, w_lo, preferred_element_type=f32) for h_hi, h_lo in splits]
    for rows, lg in zip(parts, lgs):
        logits = lg.T[0:aff_ref.shape[0], :]
        mx = jnp.max(logits, axis=0, keepdims=True)
        ex = jnp.exp(logits - mx)
        aff_ref[:, rows] = ex / jnp.sum(ex, axis=0, keepdims=True)


def _outproj(attn, conv, x, mod, w_out_b, norm2_g, w_router):
    b, l, d = x.shape
    tm = min(512, l)
    nt = l // tm
    e = w_router.shape[1]
    w_router_t = jnp.pad(w_router.astype(f32), ((0, 0), (0, LANES - e)))
    tok = lambda w: pl.BlockSpec((None, tm, w), lambda bi, i: (bi, i, 0))
    full = lambda a: pl.BlockSpec(a.shape, lambda bi, i: (0,) * a.ndim)
    return pl.pallas_call(
        functools.partial(_outproj_kernel, nparts=2),
        grid=(b, nt),
        in_specs=[tok(ATTN_W), tok(CONV_W), tok(d),
                  pl.BlockSpec((None, 1, mod.shape[-1]), lambda bi, i: (bi, 0, 0)),
                  full(w_out_b), full(norm2_g), full(w_router_t)],
        out_specs=[tok(d),
                   pl.BlockSpec((tm * (d // LANES), LANES), lambda bi, i: (bi * nt + i, 0)),
                   pl.BlockSpec((e, tm), lambda bi, i: (0, bi * nt + i))],
        out_shape=[jax.ShapeDtypeStruct((b, l, d), f32),
                   jax.ShapeDtypeStruct((b * l * (d // LANES), LANES), f32),
                   jax.ShapeDtypeStruct((e, b * l), f32)],
        compiler_params=_params("arbitrary", "arbitrary"),
        name="outproj",
    )(attn, conv, x, mod, w_out_b, norm2_g, w_router_t)


def _route_kernel(aff_ref, cs_ref, cbk_ref, *, cap, tbc):
    a = aff_ref[...]
    ne, nc, _ = a.shape

    def enough(v):
        return jnp.sum((a >= v).astype(f32), axis=(1, 2), keepdims=True) >= cap

    def pow2(k):
        return lax.bitcast_convert_type(jnp.left_shift(k + 127, 23), f32)

    def exp_body(_, st):
        klo, khi = st
        kmid = jnp.right_shift(klo + khi, 1)
        ok = enough(pow2(kmid))
        return jnp.where(ok, kmid, klo), jnp.where(ok, khi, kmid)

    klo, khi = lax.fori_loop(0, 7, exp_body,
                             (jnp.full((ne, 1, 1), -127, i32), jnp.full((ne, 1, 1), 1, i32)))

    def lin_body(_, st):
        lo, hi = st
        mid = lo + (hi - lo) * 0.5
        ok = enough(mid)
        return jnp.where(ok, mid, lo), jnp.where(ok, hi, mid)

    lo, _ = lax.fori_loop(0, 48, lin_body, (pow2(klo), pow2(khi)))
    thr = jnp.min(jnp.where(a >= lo, a, jnp.inf), axis=(1, 2), keepdims=True)

    r_i = lax.broadcasted_iota(i32, (LANES, LANES), 0)
    c_i = lax.broadcasted_iota(i32, (LANES, LANES), 1)
    upper = (r_i <= c_i).astype(bf16)
    ones = jnp.ones((LANES, LANES), bf16)
    rr = lax.broadcasted_iota(i32, (nc, nc), 0)
    cc = lax.broadcasted_iota(i32, (nc, nc), 1)
    lower = (cc < rr).astype(bf16)
    blk_r = lax.broadcasted_iota(i32, (nc, LANES), 0)
    blk_c = lax.broadcasted_iota(i32, (nc, LANES), 1)
    before_blk = (blk_r < blk_c * (tbc // LANES)).astype(bf16)
    ones8 = jnp.ones((8, LANES), bf16)

    def cumsum(m):
        within = jnp.dot(m, upper, preferred_element_type=f32)
        prev = jnp.dot(lower, m, preferred_element_type=f32)
        offs = jnp.dot(prev.astype(bf16), ones, preferred_element_type=f32)
        return within + offs

    for e in range(ne):
        a_e = a[e]
        thr_e = thr[e]
        gt_e = a_e > thr_e
        eq_e = (a_e == thr_e).astype(f32)
        need = cap - jnp.sum(gt_e.astype(f32), keepdims=True)
        rank_eq = cumsum(eq_e.astype(bf16)) - eq_e
        sel = jnp.logical_or(gt_e, jnp.logical_and(eq_e > 0.0, rank_eq < need))
        sel_b = sel.astype(bf16)
        c_incl = cumsum(sel_b)
        cs_ref[e] = jnp.where(sel, c_incl, 0.0).astype(i32)
        per_chunk = lax.dot_general(ones8, sel_b, NT_DIMS, preferred_element_type=f32)
        cbk = jnp.dot(per_chunk.astype(bf16), before_blk, preferred_element_type=f32)
        cbk_ref[e:e + 1, :] = cbk[0:1].astype(i32)


def _route(aff3, cap, tbc):
    e, nc, _ = aff3.shape
    kern = functools.partial(_route_kernel, cap=cap, tbc=tbc)
    return pl.pallas_call(
        kern,
        out_shape=[jax.ShapeDtypeStruct((e, nc, LANES), i32),
                   jax.ShapeDtypeStruct((e, LANES), i32)],
        compiler_params=pltpu.CompilerParams(vmem_limit_bytes=VMEM_LIMIT),
        name="route",
    )(aff3)


SC_CORES = 2
SC_SUBCORES = 16
SC_LANES = 16


SC_CHUNK = 32


def _sc_dispatch(cs_flat, aff_flat, h_tiles, ne, cap, rpt):
    n = cs_flat.shape[0] // ne
    workers = SC_CORES * SC_SUBCORES
    per_e = workers // ne
    per_w = cap // per_e
    assert per_e * ne == workers and per_w * per_e == cap
    assert n % SC_LANES == 0 and per_w % SC_CHUNK == 0
    table = h_tiles.reshape(-1, rpt, LANES)
    mesh = plsc.VectorSubcoreMesh(core_axis_name="c", subcore_axis_name="s",
                                  num_cores=SC_CORES, num_subcores=SC_SUBCORES)

    @functools.partial(
        pl.kernel, mesh=mesh,
        out_type=[jax.ShapeDtypeStruct((ne * cap, rpt, LANES), f32),
                  jax.ShapeDtypeStruct((ne * cap,), f32)],
        scratch_types=[pltpu.VMEM((n,), i32), pltpu.VMEM((n,), f32),
                       pltpu.VMEM((per_w,), i32), pltpu.VMEM((per_w,), f32),
                       pltpu.VMEM((SC_CHUNK, rpt, LANES), f32), pltpu.SemaphoreType.DMA],
        compiler_params=pltpu.CompilerParams(needs_layout_passes=False),
        name="sc_dispatch")
    def dispatch(cs_hbm, aff_hbm, table_hbm, xe_hbm, g_hbm, cs_v, aff_v, idx_v, g_v, rows_v, sem):
        wid = lax.axis_index("s") * SC_CORES + lax.axis_index("c")
        e = wid // per_e
        lo = (wid - e * per_e) * per_w
        base = e * cap + lo
        pltpu.sync_copy(cs_hbm.at[pl.ds(e * n, n)], cs_v)
        pltpu.sync_copy(aff_hbm.at[pl.ds(e * n, n)], aff_v)
        lane = lax.iota(i32, SC_LANES)

        @pl.loop(0, n // SC_LANES)
        def _(i):
            off = pl.multiple_of(i * SC_LANES, SC_LANES)
            local = cs_v[pl.ds(off, SC_LANES)] - 1 - lo
            mine = jnp.logical_and(local >= 0, local < per_w)
            plsc.store_scatter(idx_v, [local], off + lane, mask=mine)
            plsc.store_scatter(g_v, [local], aff_v[pl.ds(off, SC_LANES)], mask=mine)

        pltpu.sync_copy(g_v, g_hbm.at[pl.ds(base, per_w)])

        @pl.loop(0, per_w // SC_CHUNK)
        def _(c):
            off = pl.multiple_of(c * SC_CHUNK, SC_CHUNK)
            pltpu.async_copy(table_hbm.at[idx_v.at[pl.ds(off, SC_CHUNK)]], rows_v, sem).wait()
            pltpu.sync_copy(rows_v, xe_hbm.at[pl.ds(base + off, SC_CHUNK)])

    xe, g = dispatch(cs_flat, aff_flat, table)
    return xe.reshape(ne * cap * rpt, LANES), g


def _expert_kernel(x_ref, g_ref, wg_ref, wu_ref, wd_ref, y_ref, xb_ref, acc_ref, gcol_ref):
    f = pl.program_id(2)

    @pl.when(f == 0)
    def _():
        tm, d = xb_ref.shape
        rpt = d // LANES
        for c in range(rpt):
            xb_ref[:, c * LANES:(c + 1) * LANES] = x_ref[pl.ds(c, tm, stride=rpt), :].astype(bf16)
        acc_ref[...] = jnp.zeros_like(acc_ref)
        gcol_ref[...] = jnp.broadcast_to(g_ref[...], (LANES, g_ref.shape[-1])).T

    xb = xb_ref[...]
    hg = jnp.dot(xb, wg_ref[...].astype(bf16), preferred_element_type=f32)
    hu = jnp.dot(xb, wu_ref[...].astype(bf16), preferred_element_type=f32)
    hid = (hg * jax.nn.sigmoid(hg) * hu).astype(bf16)
    acc_ref[...] += jnp.dot(hid, wd_ref[...].astype(bf16), preferred_element_type=f32)

    @pl.when(f == pl.num_programs(2) - 1)
    def _():
        gcol = gcol_ref[...]
        d = acc_ref.shape[-1]
        for c in range(d // LANES):
            cols = slice(c * LANES, (c + 1) * LANES)
            y_ref[:, cols] = (acc_ref[:, cols] * gcol).astype(bf16)


def _experts(xe, g_rows, wg_b, wu_b, wd_b, tm, fc):
    e, d, fdim = wg_b.shape
    rpt = d // LANES
    rows = xe.shape[0] // rpt
    per_e = rows // e
    nt = per_e // tm
    return pl.pallas_call(
        _expert_kernel,
        grid=(e, nt, fdim // fc),
        in_specs=[pl.BlockSpec((tm * rpt, LANES), lambda ei, t, f: (ei * nt + t, 0)),
                  pl.BlockSpec((None, 1, tm), lambda ei, t, f: (ei * nt + t, 0, 0)),
                  pl.BlockSpec((None, d, fc), lambda ei, t, f: (ei, 0, f)),
                  pl.BlockSpec((None, d, fc), lambda ei, t, f: (ei, 0, f)),
                  pl.BlockSpec((None, fc, d), lambda ei, t, f: (ei, f, 0))],
        out_specs=pl.BlockSpec((tm, d), lambda ei, t, f: (ei * nt + t, 0)),
        out_shape=jax.ShapeDtypeStruct((rows, d), bf16),
        scratch_shapes=[pltpu.VMEM((tm, d), bf16), pltpu.VMEM((tm, d), f32),
                        pltpu.VMEM((tm, LANES), f32)],
        compiler_params=_params("arbitrary", "arbitrary", "arbitrary"),
        name="experts",
    )(xe, g_rows.reshape(rows // tm, 1, tm), wg_b, wu_b, wd_b)


def _combine_kernel(cbk_ref, cs_ref, x1_ref, mod_ref, ye_ref, o_ref, win_ref, xwin_ref, sem, xsem,
                    acc_ref, *, cap, tbc, win, ne, total_rows):
    blk = pl.program_id(0)
    nblk = pl.num_programs(0)
    d = acc_ref.shape[-1]
    slot = blk % 2

    def region(ee):
        return ee * cap

    def first_row(bb, ee):
        st = cbk_ref[ee * LANES + bb]
        row = region(ee) + jnp.left_shift(jnp.right_shift(st, 4), 4)
        return pl.multiple_of(jnp.minimum(row, total_rows - win), BF16_ROWS)

    def start_block(bb, sl):
        for ee in range(ne):
            pltpu.make_async_copy(ye_ref.at[pl.ds(first_row(bb, ee), win)],
                                  win_ref.at[sl, pl.ds(ee * win, win)], sem.at[sl]).start()

    @pl.when(blk == 0)
    def _():
        start_block(blk, slot)

    @pl.when(blk + 1 < nblk)
    def _():
        start_block(blk + 1, (blk + 1) % 2)

    pltpu.make_async_copy(ye_ref.at[pl.ds(0, ne * win)], win_ref.at[slot], sem.at[slot]).wait()

    w_iota = lax.broadcasted_iota(i32, (win, tbc), 0)

    def onehot_of(cs_row, base_slot, done_slots):
        hit = jnp.logical_and(cs_row == base_slot + 1 + w_iota, cs_row > done_slots)
        return hit.astype(f32).T.astype(bf16)

    onehot = jnp.concatenate(
        [onehot_of(cs_ref[ee:ee + 1, :], first_row(blk, ee) - region(ee), 0) for ee in range(ne)],
        axis=1)
    acc_ref[...] = jnp.dot(onehot, win_ref[slot], preferred_element_type=f32)

    def extra_expert(ee, carry):
        row0 = first_row(blk, ee)
        base0 = row0 - region(ee)
        end = cbk_ref[ee * LANES + blk + 1]
        cs_row = cs_ref[pl.ds(ee, 1), :]

        def extra(w, c2):
            row_w = pl.multiple_of(jnp.minimum(row0 + w * win, total_rows - win), BF16_ROWS)
            cp = pltpu.make_async_copy(ye_ref.at[pl.ds(row_w, win)], xwin_ref, xsem)
            cp.start()
            cp.wait()
            oh = onehot_of(cs_row, row_w - region(ee), base0 + w * win)
            acc_ref[...] += jnp.dot(oh, xwin_ref[...], preferred_element_type=f32)
            return c2

        lax.fori_loop(1, (end - base0 + win - 1) // win, extra, 0)
        return carry

    lax.fori_loop(0, ne, extra_expert, 0)

    gate2 = mod_ref[:, 5 * d:6 * d]
    o_ref[...] = x1_ref[...] + gate2 * acc_ref[...]


def _combine(cbk, cs, x1, mod, ye, cap, tbc):
    b, l, d = x1.shape
    n = b * l
    e = cs.shape[0]
    nb = n // tbc
    win = min(128, cap)
    per_b = l // tbc
    kern = functools.partial(_combine_kernel, cap=cap, tbc=tbc, win=win, ne=e,
                             total_rows=ye.shape[0])
    return pl.pallas_call(
        kern,
        grid_spec=pltpu.PrefetchScalarGridSpec(
            num_scalar_prefetch=1, grid=(nb,),
            in_specs=[pl.BlockSpec((e, tbc), lambda bk, c: (0, bk)),
                      pl.BlockSpec((tbc, d), lambda bk, c: (bk, 0)),
                      pl.BlockSpec((None, 1, mod.shape[-1]), lambda bk, c: (bk // per_b, 0, 0)),
                      pl.BlockSpec(memory_space=pl.ANY)],
            out_specs=pl.BlockSpec((tbc, d), lambda bk, c: (bk, 0)),
            scratch_shapes=[pltpu.VMEM((2, e * win, d), bf16), pltpu.VMEM((win, d), bf16),
                            pltpu.SemaphoreType.DMA((2,)), pltpu.SemaphoreType.DMA,
                            pltpu.VMEM((tbc, d), f32)]),
        out_shape=jax.ShapeDtypeStruct((n, d), f32),
        compiler_params=_params("arbitrary"),
        name="combine",
    )(cbk.reshape(-1), cs, x1.reshape(n, d), mod, ye).reshape(b, l, d)


TBF = 512
TM_EXPERT = 2048
FC_EXPERT = 256


def kernel(x_prompt, x_sample, c_prompt, c_sample, ada_w, ada_b, norm1_g, w_in, q_norm_g, k_norm_g, rpb,
           conv_w, conv_b, conv_ln_g, conv_ln_b, w_out, norm2_g, w_router, w_gate, w_up, w_down):
    xs = (x_prompt, x_sample)
    d = x_prompt.shape[-1]
    ne = w_router.shape[1]
    row = lambda a: a.reshape(1, -1).astype(f32)

    nb0, nb1 = c_prompt.shape[0], c_sample.shape[0]
    pad = (-(nb0 + nb1)) % 8
    c_all = jnp.concatenate([c_prompt, c_sample, jnp.zeros((pad, d), f32)], axis=0)
    mod_all = _ada(c_all, ada_w, ada_b)
    mods = (mod_all[:nb0].reshape(nb0, 1, -1), mod_all[nb0:nb0 + nb1].reshape(nb1, 1, -1))

    seg_i = jnp.arange(ATTN_W) // HEAD_DIM
    seg = (seg_i[:, None] == seg_i[None, :]).astype(bf16)
    gq_t = jnp.tile(q_norm_g.astype(f32), N_HEADS).reshape(1, ATTN_W)
    gk_t = jnp.tile(k_norm_g.astype(f32), N_HEADS).reshape(1, ATTN_W)
    bias_tab = _bias_table(rpb)
    w_in_b = w_in.astype(bf16)
    w_out_b = w_out.astype(bf16)

    staged = []
    for x, mod in zip(xs, mods):
        b, l, _ = x.shape
        n = b * l
        cap = max(1, EC_FACTOR * n // ne)
        q, k, v, u = _inproj(x, mod, row(norm1_g), w_in_b, seg, gq_t, gk_t)
        attn = _attention(q, k, v, bias_tab)
        conv = _conv(u, conv_w.astype(f32), row(conv_b), row(conv_ln_g), row(conv_ln_b), seg)
        x1, h2, aff_t = _outproj(attn, conv, x, mod, w_out_b, row(norm2_g), w_router)
        aff3 = aff_t.reshape(ne, n // LANES, LANES)
        cs, cbk = _route(aff3, cap, TBF)
        xe, g = _sc_dispatch(cs.reshape(-1), aff3.reshape(-1), h2, ne, cap, d // LANES)
        staged.append((x1, mod, xe, g.reshape(ne, cap), cs.reshape(ne, n), cbk, cap))

    outs = []
    for x1, mod, xe, g, cs, cbk, cap in staged:
        ye = _experts(xe, g, w_gate, w_up, w_down, min(TM_EXPERT, cap), min(FC_EXPERT, w_gate.shape[2]))
        outs.append(_combine(cbk, cs, x1, mod, ye, cap, TBF))
    return tuple(outs)
```

```python
import functools

import jax
import jax.numpy as jnp
from jax import lax
from jax.experimental import pallas as pl
from jax.experimental.pallas import tpu as pltpu
from jax.experimental.pallas import tpu_sc as plsc

f32 = jnp.float32
bf16 = jnp.bfloat16
i32 = jnp.int32

GRID_W = 64
N_HEADS = 8
HEAD_DIM = 64
ATTN_W = N_HEADS * HEAD_DIM
CONV_W = 512
CONV_K = 31
CONV_GROUP = 64
WIN_ROWS = 8
WIN_COLS = 16
N_EXPERTS = 16
EC_FACTOR = 2
N_MOD = 6
EPS = 1e-6
NEG = -1e30

LANES = 128
BF16_ROWS = 16
VMEM_LIMIT = 56 * 1024 * 1024

NT_DIMS = (((1,), (1,)), ((), ()))


def _params(*sem):
    return pltpu.CompilerParams(dimension_semantics=sem, vmem_limit_bytes=VMEM_LIMIT)


def _split_bf16(x):
    hi = x.astype(bf16)
    lo = (x - hi.astype(f32)).astype(bf16)
    return hi, lo


def _ada_kernel(c_ref, w_ref, b_ref, o_ref):
    c = c_ref[...]
    s = c * jax.nn.sigmoid(c)
    s_hi, s_lo = _split_bf16(s)
    w_hi, w_lo = _split_bf16(w_ref[...])
    acc = jnp.dot(s_hi, w_hi, preferred_element_type=f32)
    acc += jnp.dot(s_hi, w_lo, preferred_element_type=f32)
    acc += jnp.dot(s_lo, w_hi, preferred_element_type=f32)
    o_ref[...] = acc + b_ref[...]


def _ada(c_all, ada_w, ada_b):
    rows, d = c_all.shape
    ncols = ada_w.shape[1]
    tn = next(t for t in (1024, 512, 256, 128) if ncols % t == 0)
    return pl.pallas_call(
        _ada_kernel,
        grid=(ncols // tn,),
        in_specs=[pl.BlockSpec((rows, d), lambda j: (0, 0)),
                  pl.BlockSpec((d, tn), lambda j: (0, j)),
                  pl.BlockSpec((1, tn), lambda j: (0, j))],
        out_specs=pl.BlockSpec((rows, tn), lambda j: (0, j)),
        out_shape=jax.ShapeDtypeStruct((rows, ncols), f32),
        compiler_params=_params("arbitrary"),
        name="ada",
    )(c_all, ada_w, ada_b.reshape(1, ncols))


def _inproj_kernel(x_ref, mod_ref, g1_ref, w_ref, seg_ref, gq_ref, gk_ref,
                   q_ref, k_ref, v_ref, u_ref):
    x = x_ref[...]
    d = x.shape[-1]
    ms = jnp.mean(x * x, axis=-1, keepdims=True)
    h = x * lax.rsqrt(ms + EPS) * g1_ref[...]
    shift = mod_ref[:, 0:d]
    scale = mod_ref[:, d:2 * d]
    h = h * (1.0 + scale) + shift
    z = jnp.dot(h.astype(bf16), w_ref[...], preferred_element_type=f32)
    zq = z[:, 0:ATTN_W]
    zk = z[:, ATTN_W:2 * ATTN_W]
    seg = seg_ref[...]
    ssq_q = jnp.dot((zq * zq).astype(bf16), seg, preferred_element_type=f32)
    ssq_k = jnp.dot((zk * zk).astype(bf16), seg, preferred_element_type=f32)
    qs = gq_ref[...] * gk_ref[...] * (HEAD_DIM ** -0.5)
    q_ref[...] = (zq * lax.rsqrt(ssq_q * (1.0 / HEAD_DIM) + EPS) * qs).astype(bf16)
    k_ref[...] = (zk * lax.rsqrt(ssq_k * (1.0 / HEAD_DIM) + EPS)).astype(bf16)
    v_ref[...] = z[:, 2 * ATTN_W:3 * ATTN_W].astype(bf16)
    a = z[:, 3 * ATTN_W:3 * ATTN_W + CONV_W]
    gt = z[:, 3 * ATTN_W + CONV_W:]
    u_ref[...] = a * jax.nn.sigmoid(gt)


def _inproj(x, mod, norm1_g, w_in_b, seg, gq_t, gk_t):
    b, l, d = x.shape
    tm = min(512, l)
    ncols = w_in_b.shape[1]
    tok = lambda w: pl.BlockSpec((None, tm, w), lambda bi, i: (bi, i, 0))
    full = lambda a: pl.BlockSpec(a.shape, lambda bi, i: (0,) * a.ndim)
    return pl.pallas_call(
        _inproj_kernel,
        grid=(b, l // tm),
        in_specs=[tok(d),
                  pl.BlockSpec((None, 1, mod.shape[-1]), lambda bi, i: (bi, 0, 0)),
                  full(norm1_g), full(w_in_b), full(seg), full(gq_t), full(gk_t)],
        out_specs=[tok(ATTN_W), tok(ATTN_W), tok(ATTN_W), tok(CONV_W)],
        out_shape=[jax.ShapeDtypeStruct((b, l, ATTN_W), bf16)] * 3
                  + [jax.ShapeDtypeStruct((b, l, CONV_W), f32)],
        compiler_params=_params("arbitrary", "arbitrary"),
        name="inproj",
    )(x, mod, norm1_g, w_in_b, seg, gq_t, gk_t)


def _attn_kernel(q_ref, k_ref, v_ref, bias_ref, o_ref, *, rows, rg, wr, nr):
    g = pl.program_id(1)
    lane = lax.broadcasted_iota(i32, (GRID_W, LANES), 1)
    first_head = lane < HEAD_DIM
    nkeys = wr * GRID_W

    pairs = [(rr, p) for rr in range(nr) for p in range(N_HEADS // 2)]

    def rows_body(it, carry):
        geo = []
        for rr in range(nr):
            lr = it * nr + rr
            r = g * rg + lr
            rs = jnp.clip(r - wr // 2, 0, rows - wr)
            geo.append((pl.multiple_of(lr * GRID_W, GRID_W),
                        pl.multiple_of(rs * GRID_W, GRID_W),
                        rs - r + (WIN_ROWS - 1)))
        scores = []
        for rr, p in pairs:
            q0, tok0, rb = geo[rr]
            cols = slice(p * LANES, (p + 1) * LANES)
            qp = q_ref[pl.ds(q0, GRID_W), cols]
            kp = k_ref[pl.ds(tok0, nkeys), cols]
            zero = jnp.zeros_like(qp)
            q2 = jnp.concatenate([jnp.where(first_head, qp, zero),
                                  jnp.where(first_head, zero, qp)], axis=0)
            s = lax.dot_general(q2, kp, NT_DIMS, preferred_element_type=f32)
            bias = jnp.concatenate(
                [jnp.concatenate([bias_ref[2 * p + j, rb + 2 * m] for m in range(nkeys // LANES)],
                                 axis=-1) for j in range(2)], axis=0)
            scores.append(s + bias)
        probs = []
        for s in scores:
            mx = jnp.max(s, axis=-1, keepdims=True)
            e = jnp.exp(s - mx)
            probs.append((e.astype(bf16), jnp.sum(e, axis=-1, keepdims=True)))
        for (rr, p), (e, den) in zip(pairs, probs):
            q0, tok0, rb = geo[rr]
            cols = slice(p * LANES, (p + 1) * LANES)
            vp = v_ref[pl.ds(tok0, nkeys), cols]
            o = jnp.dot(e, vp, preferred_element_type=f32) / den
            o_ref[pl.ds(q0, GRID_W), cols] = jnp.where(
                first_head, o[0:GRID_W], o[GRID_W:]).astype(bf16)
        return carry

    lax.fori_loop(0, rg // nr, rows_body, 0)


def _attention(q, k, v, bias_tab):
    b, l, _ = q.shape
    rows = l // GRID_W
    wr = min(WIN_ROWS, rows)
    rg = min(8, rows)
    once = pl.Buffered(1)
    kern = functools.partial(_attn_kernel, rows=rows, rg=rg, wr=wr, nr=4)
    return pl.pallas_call(
        kern,
        grid=(b, rows // rg),
        in_specs=[pl.BlockSpec((None, rg * GRID_W, ATTN_W), lambda bi, g: (bi, g, 0)),
                  pl.BlockSpec((None, l, ATTN_W), lambda bi, g: (bi, 0, 0), pipeline_mode=once),
                  pl.BlockSpec((None, l, ATTN_W), lambda bi, g: (bi, 0, 0), pipeline_mode=once),
                  pl.BlockSpec(bias_tab.shape, lambda bi, g: (0, 0, 0, 0), pipeline_mode=once)],
        out_specs=pl.BlockSpec((None, rg * GRID_W, ATTN_W), lambda bi, g: (bi, g, 0)),
        out_shape=jax.ShapeDtypeStruct((b, l, ATTN_W), bf16),
        compiler_params=_params("arbitrary", "arbitrary"),
        name="attention",
    )(q, k, v, bias_tab)


def _bias_table(rpb):
    cols = jnp.arange(GRID_W)
    col_start = jnp.clip(cols - WIN_COLS // 2, 0, GRID_W - WIN_COLS)
    kc = jnp.arange(GRID_W)[None, :]
    valid = (kc >= col_start[:, None]) & (kc < col_start[:, None] + WIN_COLS)
    off = jnp.clip(kc - cols[:, None] + (WIN_COLS - 1), 0, 2 * WIN_COLS - 2)
    tab = jnp.where(valid[None, None], rpb[:, :, off].astype(f32), NEG)
    return jnp.concatenate([tab[:, :-1], tab[:, 1:]], axis=-1)


def _conv_kernel(u_ref, prev_ref, next_ref, w_ref, cb_ref, lg_ref, lb_ref, seg_ref, o_ref,
                 win_ref, sh_ref, *, tl, chunk):
    i = pl.program_id(1)
    n_i = pl.num_programs(1)
    halo = prev_ref.shape[0]
    sub = 8
    win_ref[0:halo, :] = jnp.where(i > 0, prev_ref[...], 0.0)
    win_ref[halo:halo + tl, :] = u_ref[...]
    win_ref[halo + tl:, :] = jnp.where(i < n_i - 1, next_ref[...], 0.0)
    sh_rows = sh_ref.shape[1]
    for r in range(1, sub):
        for k0 in range(0, sh_rows, chunk):
            k1 = min(k0 + chunk, sh_rows)
            sh_ref[r - 1, k0:k1, :] = win_ref[r + k0:r + k1, :]
    seg = seg_ref[...]
    pad = CONV_K // 2
    for c in range(tl // chunk):
        acc = jnp.zeros((chunk, CONV_W), f32)
        for j in range(CONV_K):
            start = halo - pad + c * chunk + j
            r = start % sub
            src = win_ref if r == 0 else sh_ref.at[r - 1]
            acc = acc + src[start - r:start - r + chunk, :] * w_ref[j:j + 1, :]
        y = acc + cb_ref[...]
        y_hi, y_lo = _split_bf16(y)
        mu = (jnp.dot(y_hi, seg, preferred_element_type=f32)
              + jnp.dot(y_lo, seg, preferred_element_type=f32)) * (1.0 / CONV_GROUP)
        dlt = y - mu
        d_hi, d_lo = _split_bf16(dlt * dlt)
        var = (jnp.dot(d_hi, seg, preferred_element_type=f32)
               + jnp.dot(d_lo, seg, preferred_element_type=f32)) * (1.0 / CONV_GROUP)
        un = dlt * lax.rsqrt(var + EPS) * lg_ref[...] + lb_ref[...]
        o_ref[c * chunk:(c + 1) * chunk, :] = (un * jax.nn.sigmoid(un)).astype(bf16)


def _conv(u, conv_w, conv_b, ln_g, ln_b, seg):
    b, l, _ = u.shape
    tl = min(512, l)
    halo = 16
    nh = l // halo
    kern = functools.partial(_conv_kernel, tl=tl, chunk=min(128, tl))
    full = lambda a: pl.BlockSpec(a.shape, lambda bi, i: (0,) * a.ndim)
    return pl.pallas_call(
        kern,
        grid=(b, l // tl),
        in_specs=[pl.BlockSpec((None, tl, CONV_W), lambda bi, i: (bi, i, 0)),
                  pl.BlockSpec((None, halo, CONV_W),
                               lambda bi, i: (bi, jnp.maximum(i * (tl // halo) - 1, 0), 0)),
                  pl.BlockSpec((None, halo, CONV_W),
                               lambda bi, i: (bi, jnp.minimum((i + 1) * (tl // halo), nh - 1), 0)),
                  full(conv_w), full(conv_b), full(ln_g), full(ln_b), full(seg)],
        out_specs=pl.BlockSpec((None, tl, CONV_W), lambda bi, i: (bi, i, 0)),
        out_shape=jax.ShapeDtypeStruct((b, l, CONV_W), bf16),
        scratch_shapes=[pltpu.VMEM((tl + 2 * halo, CONV_W), f32),
                        pltpu.VMEM((7, tl + 2 * halo - 8, CONV_W), f32)],
        compiler_params=_params("arbitrary", "arbitrary"),
        name="conv",
    )(u, u, u, conv_w, conv_b, ln_g, ln_b, seg)


def _outproj_kernel(at_ref, cv_ref, x_ref, mod_ref, wo_ref, g2_ref, wr_ref,
                    x1_ref, h2_ref, aff_ref, *, nparts):
    tm, d = x_ref.shape
    rpt = d // LANES
    gate1 = mod_ref[:, 2 * d:3 * d]
    shift2 = mod_ref[:, 3 * d:4 * d]
    scale2 = mod_ref[:, 4 * d:5 * d]
    w_hi, w_lo = _split_bf16(wr_ref[...])
    parts = [slice(p * (tm // nparts), (p + 1) * (tm // nparts)) for p in range(nparts)]
    mixes = [jnp.dot(at_ref[rows, :], wo_ref[0:ATTN_W, :], preferred_element_type=f32)
             + jnp.dot(cv_ref[rows, :], wo_ref[ATTN_W:, :], preferred_element_type=f32)
             for rows in parts]
    splits = []
    for rows, mix in zip(parts, mixes):
        x1 = x_ref[rows, :] + gate1 * mix
        x1_ref[rows, :] = x1
        ms = jnp.mean(x1 * x1, axis=-1, keepdims=True)
        h2 = x1 * lax.rsqrt(ms + EPS) * g2_ref[...]
        h2 = h2 * (1.0 + scale2) + shift2
        h_hi, h_lo = _split_bf16(h2)
        h2_ref[rows, :] = h_hi.astype(f32)
        splits.append((h_hi, h_lo))
    lgs = [jnp.dot(h_hi, w_hi, preferred_element_type=f32)
           + jnp.dot(h_lo, w_hi, preferred_element_type=f32)
           + jnp.dot(h_hi, w_lo, preferred_element_type=f32) for h_hi, h_lo in splits]
    for rows, lg in zip(parts, lgs):
        logits = lg.T[0:aff_ref.shape[0], :]
        mx = jnp.max(logits, axis=0, keepdims=True)
        ex = jnp.exp(logits - mx)
        aff_ref[:, rows] = ex / jnp.sum(ex, axis=0, keepdims=True)


def _outproj(attn, conv, x, mod, w_out_b, norm2_g, w_router):
    b, l, d = x.shape
    tm = min(512, l)
    nt = l // tm
    e = w_router.shape[1]
    w_router_t = jnp.pad(w_router.astype(f32), ((0, 0), (0, LANES - e)))
    tok = lambda w: pl.BlockSpec((None, tm, w), lambda bi, i: (bi, i, 0))
    full = lambda a: pl.BlockSpec(a.shape, lambda bi, i: (0,) * a.ndim)
    return pl.pallas_call(
        functools.partial(_outproj_kernel, nparts=2),
        grid=(b, nt),
        in_specs=[tok(ATTN_W), tok(CONV_W), tok(d),
                  pl.BlockSpec((None, 1, mod.shape[-1]), lambda bi, i: (bi, 0, 0)),
                  full(w_out_b), full(norm2_g), full(w_router_t)],
        out_specs=[tok(d),
                   pl.BlockSpec((tm, d), lambda bi, i: (bi * nt + i, 0)),
                   pl.BlockSpec((e, tm), lambda bi, i: (0, bi * nt + i))],
        out_shape=[jax.ShapeDtypeStruct((b, l, d), f32),
                   jax.ShapeDtypeStruct((b * l, d), f32),
                   jax.ShapeDtypeStruct((e, b * l), f32)],
        compiler_params=_params("arbitrary", "arbitrary"),
        name="outproj",
    )(attn, conv, x, mod, w_out_b, norm2_g, w_router_t)


def _route_kernel(aff_ref, cs_ref, cbk_ref, *, cap, tbc):
    a = aff_ref[...]
    ne, nc, _ = a.shape

    def enough(v):
        return jnp.sum((a >= v).astype(f32), axis=(1, 2), keepdims=True) >= cap

    def pow2(k):
        return lax.bitcast_convert_type(jnp.left_shift(k + 127, 23), f32)

    def exp_body(_, st):
        klo, khi = st
        kmid = jnp.right_shift(klo + khi, 1)
        ok = enough(pow2(kmid))
        return jnp.where(ok, kmid, klo), jnp.where(ok, khi, kmid)

    klo, khi = lax.fori_loop(0, 7, exp_body,
                             (jnp.full((ne, 1, 1), -127, i32), jnp.full((ne, 1, 1), 1, i32)))

    def lin_body(_, st):
        lo, hi = st
        mid = lo + (hi - lo) * 0.5
        ok = enough(mid)
        return jnp.where(ok, mid, lo), jnp.where(ok, hi, mid)

    lo, _ = lax.fori_loop(0, 48, lin_body, (pow2(klo), pow2(khi)))
    thr = jnp.min(jnp.where(a >= lo, a, jnp.inf), axis=(1, 2), keepdims=True)

    r_i = lax.broadcasted_iota(i32, (LANES, LANES), 0)
    c_i = lax.broadcasted_iota(i32, (LANES, LANES), 1)
    upper = (r_i <= c_i).astype(bf16)
    ones = jnp.ones((LANES, LANES), bf16)
    rr = lax.broadcasted_iota(i32, (nc, nc), 0)
    cc = lax.broadcasted_iota(i32, (nc, nc), 1)
    lower = (cc < rr).astype(bf16)
    blk_r = lax.broadcasted_iota(i32, (nc, LANES), 0)
    blk_c = lax.broadcasted_iota(i32, (nc, LANES), 1)
    before_blk = (blk_r < blk_c * (tbc // LANES)).astype(bf16)
    ones8 = jnp.ones((8, LANES), bf16)

    def cumsum(m):
        within = jnp.dot(m, upper, preferred_element_type=f32)
        prev = jnp.dot(lower, m, preferred_element_type=f32)
        offs = jnp.dot(prev.astype(bf16), ones, preferred_element_type=f32)
        return within + offs

    for e in range(ne):
        a_e = a[e]
        thr_e = thr[e]
        gt_e = a_e > thr_e
        eq_e = (a_e == thr_e).astype(f32)
        need = cap - jnp.sum(gt_e.astype(f32), keepdims=True)
        rank_eq = cumsum(eq_e.astype(bf16)) - eq_e
        sel = jnp.logical_or(gt_e, jnp.logical_and(eq_e > 0.0, rank_eq < need))
        sel_b = sel.astype(bf16)
        c_incl = cumsum(sel_b)
        cs_ref[e] = jnp.where(sel, c_incl, 0.0).astype(i32)
        per_chunk = lax.dot_general(ones8, sel_b, NT_DIMS, preferred_element_type=f32)
        cbk = jnp.dot(per_chunk.astype(bf16), before_blk, preferred_element_type=f32)
        cbk_ref[e:e + 1, :] = cbk[0:1].astype(i32)


def _route(aff3, cap, tbc):
    e, nc, _ = aff3.shape
    kern = functools.partial(_route_kernel, cap=cap, tbc=tbc)
    return pl.pallas_call(
        kern,
        out_shape=[jax.ShapeDtypeStruct((e, nc, LANES), i32),
                   jax.ShapeDtypeStruct((e, LANES), i32)],
        compiler_params=pltpu.CompilerParams(vmem_limit_bytes=VMEM_LIMIT),
        name="route",
    )(aff3)


SC_CORES = 2
SC_SUBCORES = 16
SC_LANES = 16


SC_CHUNK = 32


def _sc_dispatch(cs_flat, aff_flat, h_tiles, ne, cap):
    n = cs_flat.shape[0] // ne
    workers = SC_CORES * SC_SUBCORES
    per_e = workers // ne
    per_w = cap // per_e
    assert per_e * ne == workers and per_w * per_e == cap
    assert n % SC_LANES == 0 and per_w % SC_CHUNK == 0
    table = h_tiles
    d_model = table.shape[1]
    mesh = plsc.VectorSubcoreMesh(core_axis_name="c", subcore_axis_name="s",
                                  num_cores=SC_CORES, num_subcores=SC_SUBCORES)

    @functools.partial(
        pl.kernel, mesh=mesh,
        out_type=[jax.ShapeDtypeStruct((ne * cap, d_model), f32),
                  jax.ShapeDtypeStruct((ne * cap,), f32)],
        scratch_types=[pltpu.VMEM((n,), i32), pltpu.VMEM((n,), f32),
                       pltpu.VMEM((per_w,), i32), pltpu.VMEM((per_w,), f32),
                       pltpu.VMEM((SC_CHUNK, d_model), f32), pltpu.SemaphoreType.DMA],
        compiler_params=pltpu.CompilerParams(needs_layout_passes=False, use_tc_tiling_on_sc=True),
        name="sc_dispatch")
    def dispatch(cs_hbm, aff_hbm, table_hbm, xe_hbm, g_hbm, cs_v, aff_v, idx_v, g_v, rows_v, sem):
        wid = lax.axis_index("s") * SC_CORES + lax.axis_index("c")
        e = wid // per_e
        lo = (wid - e * per_e) * per_w
        base = e * cap + lo
        pltpu.sync_copy(cs_hbm.at[pl.ds(e * n, n)], cs_v)
        pltpu.sync_copy(aff_hbm.at[pl.ds(e * n, n)], aff_v)
        lane = lax.iota(i32, SC_LANES)

        @pl.loop(0, n // SC_LANES)
        def _(i):
            off = pl.multiple_of(i * SC_LANES, SC_LANES)
            local = cs_v[pl.ds(off, SC_LANES)] - 1 - lo
            mine = jnp.logical_and(local >= 0, local < per_w)
            plsc.store_scatter(idx_v, [local], off + lane, mask=mine)
            plsc.store_scatter(g_v, [local], aff_v[pl.ds(off, SC_LANES)], mask=mine)

        pltpu.sync_copy(g_v, g_hbm.at[pl.ds(base, per_w)])

        @pl.loop(0, per_w // SC_CHUNK)
        def _(c):
            off = pl.multiple_of(c * SC_CHUNK, SC_CHUNK)
            pltpu.async_copy(table_hbm.at[idx_v.at[pl.ds(off, SC_CHUNK)]], rows_v, sem).wait()
            pltpu.sync_copy(rows_v, xe_hbm.at[pl.ds(base + off, SC_CHUNK)])

    return dispatch(cs_flat, aff_flat, table)


def _expert_kernel(x_ref, g_ref, wg_ref, wu_ref, wd_ref, y_ref, xb_ref, acc_ref, gcol_ref):
    f = pl.program_id(2)

    @pl.when(f == 0)
    def _():
        xb_ref[...] = x_ref[...].astype(bf16)
        acc_ref[...] = jnp.zeros_like(acc_ref)
        gcol_ref[...] = jnp.broadcast_to(g_ref[...], (LANES, g_ref.shape[-1])).T

    xb = xb_ref[...]
    hg = jnp.dot(xb, wg_ref[...].astype(bf16), preferred_element_type=f32)
    hu = jnp.dot(xb, wu_ref[...].astype(bf16), preferred_element_type=f32)
    hid = (hg * jax.nn.sigmoid(hg) * hu).astype(bf16)
    acc_ref[...] += jnp.dot(hid, wd_ref[...].astype(bf16), preferred_element_type=f32)

    @pl.when(f == pl.num_programs(2) - 1)
    def _():
        gcol = gcol_ref[...]
        d = acc_ref.shape[-1]
        for c in range(d // LANES):
            cols = slice(c * LANES, (c + 1) * LANES)
            y_ref[:, cols] = (acc_ref[:, cols] * gcol).astype(bf16)


def _experts(xe, g_rows, wg_b, wu_b, wd_b, tm, fc):
    e, d, fdim = wg_b.shape
    rows = xe.shape[0]
    per_e = rows // e
    nt = per_e // tm
    return pl.pallas_call(
        _expert_kernel,
        grid=(e, nt, fdim // fc),
        in_specs=[pl.BlockSpec((tm, d), lambda ei, t, f: (ei * nt + t, 0)),
                  pl.BlockSpec((None, 1, tm), lambda ei, t, f: (ei * nt + t, 0, 0)),
                  pl.BlockSpec((None, d, fc), lambda ei, t, f: (ei, 0, f)),
                  pl.BlockSpec((None, d, fc), lambda ei, t, f: (ei, 0, f)),
                  pl.BlockSpec((None, fc, d), lambda ei, t, f: (ei, f, 0))],
        out_specs=pl.BlockSpec((tm, d), lambda ei, t, f: (ei * nt + t, 0)),
        out_shape=jax.ShapeDtypeStruct((rows, d), bf16),
        scratch_shapes=[pltpu.VMEM((tm, d), bf16), pltpu.VMEM((tm, d), f32),
                        pltpu.VMEM((tm, LANES), f32)],
        compiler_params=_params("arbitrary", "arbitrary", "arbitrary"),
        name="experts",
    )(xe, g_rows.reshape(rows // tm, 1, tm), wg_b, wu_b, wd_b)


def _combine_kernel(cbk_ref, cs_ref, x1_ref, mod_ref, ye_ref, o_ref, win_ref, xwin_ref, sem, xsem,
                    acc_ref, *, cap, tbc, win, ne, total_rows):
    blk = pl.program_id(0)
    nblk = pl.num_programs(0)
    d = acc_ref.shape[-1]
    slot = blk % 2

    def region(ee):
        return ee * cap

    def first_row(bb, ee):
        st = cbk_ref[ee * LANES + bb]
        row = region(ee) + jnp.left_shift(jnp.right_shift(st, 4), 4)
        return pl.multiple_of(jnp.minimum(row, total_rows - win), BF16_ROWS)

    def start_block(bb, sl):
        for ee in range(ne):
            pltpu.make_async_copy(ye_ref.at[pl.ds(first_row(bb, ee), win)],
                                  win_ref.at[sl, pl.ds(ee * win, win)], sem.at[sl]).start()

    @pl.when(blk == 0)
    def _():
        start_block(blk, slot)

    @pl.when(blk + 1 < nblk)
    def _():
        start_block(blk + 1, (blk + 1) % 2)

    pltpu.make_async_copy(ye_ref.at[pl.ds(0, ne * win)], win_ref.at[slot], sem.at[slot]).wait()

    w_iota = lax.broadcasted_iota(i32, (win, tbc), 0)

    def onehot_of(cs_row, base_slot, done_slots):
        hit = jnp.logical_and(cs_row == base_slot + 1 + w_iota, cs_row > done_slots)
        return hit.astype(f32).T.astype(bf16)

    onehot = jnp.concatenate(
        [onehot_of(cs_ref[ee:ee + 1, :], first_row(blk, ee) - region(ee), 0) for ee in range(ne)],
        axis=1)
    acc_ref[...] = jnp.dot(onehot, win_ref[slot], preferred_element_type=f32)

    def extra_expert(ee, carry):
        row0 = first_row(blk, ee)
        base0 = row0 - region(ee)
        end = cbk_ref[ee * LANES + blk + 1]
        cs_row = cs_ref[pl.ds(ee, 1), :]

        def extra(w, c2):
            row_w = pl.multiple_of(jnp.minimum(row0 + w * win, total_rows - win), BF16_ROWS)
            cp = pltpu.make_async_copy(ye_ref.at[pl.ds(row_w, win)], xwin_ref, xsem)
            cp.start()
            cp.wait()
            oh = onehot_of(cs_row, row_w - region(ee), base0 + w * win)
            acc_ref[...] += jnp.dot(oh, xwin_ref[...], preferred_element_type=f32)
            return c2

        lax.fori_loop(1, (end - base0 + win - 1) // win, extra, 0)
        return carry

    lax.fori_loop(0, ne, extra_expert, 0)

    gate2 = mod_ref[:, 5 * d:6 * d]
    o_ref[...] = x1_ref[...] + gate2 * acc_ref[...]


def _combine(cbk, cs, x1, mod, ye, cap, tbc):
    b, l, d = x1.shape
    n = b * l
    e = cs.shape[0]
    nb = n // tbc
    win = min(128, cap)
    per_b = l // tbc
    kern = functools.partial(_combine_kernel, cap=cap, tbc=tbc, win=win, ne=e,
                             total_rows=ye.shape[0])
    return pl.pallas_call(
        kern,
        grid_spec=pltpu.PrefetchScalarGridSpec(
            num_scalar_prefetch=1, grid=(nb,),
            in_specs=[pl.BlockSpec((e, tbc), lambda bk, c: (0, bk)),
                      pl.BlockSpec((tbc, d), lambda bk, c: (bk, 0)),
                      pl.BlockSpec((None, 1, mod.shape[-1]), lambda bk, c: (bk // per_b, 0, 0)),
                      pl.BlockSpec(memory_space=pl.ANY)],
            out_specs=pl.BlockSpec((tbc, d), lambda bk, c: (bk, 0)),
            scratch_shapes=[pltpu.VMEM((2, e * win, d), bf16), pltpu.VMEM((win, d), bf16),
                            pltpu.SemaphoreType.DMA((2,)), pltpu.SemaphoreType.DMA,
                            pltpu.VMEM((tbc, d), f32)]),
        out_shape=jax.ShapeDtypeStruct((n, d), f32),
        compiler_params=_params("arbitrary"),
        name="combine",
    )(cbk.reshape(-1), cs, x1.reshape(n, d), mod, ye).reshape(b, l, d)


TBF = 512
TM_EXPERT = 2048
FC_EXPERT = 256


def kernel(x_prompt, x_sample, c_prompt, c_sample, ada_w, ada_b, norm1_g, w_in, q_norm_g, k_norm_g, rpb,
           conv_w, conv_b, conv_ln_g, conv_ln_b, w_out, norm2_g, w_router, w_gate, w_up, w_down):
    xs = (x_prompt, x_sample)
    d = x_prompt.shape[-1]
    ne = w_router.shape[1]
    row = lambda a: a.reshape(1, -1).astype(f32)

    nb0, nb1 = c_prompt.shape[0], c_sample.shape[0]
    pad = (-(nb0 + nb1)) % 8
    c_all = jnp.concatenate([c_prompt, c_sample, jnp.zeros((pad, d), f32)], axis=0)
    mod_all = _ada(c_all, ada_w, ada_b)
    mods = (mod_all[:nb0].reshape(nb0, 1, -1), mod_all[nb0:nb0 + nb1].reshape(nb1, 1, -1))

    seg_i = jnp.arange(ATTN_W) // HEAD_DIM
    seg = (seg_i[:, None] == seg_i[None, :]).astype(bf16)
    gq_t = jnp.tile(q_norm_g.astype(f32), N_HEADS).reshape(1, ATTN_W)
    gk_t = jnp.tile(k_norm_g.astype(f32), N_HEADS).reshape(1, ATTN_W)
    bias_tab = _bias_table(rpb)
    w_in_b = w_in.astype(bf16)
    w_out_b = w_out.astype(bf16)

    staged = []
    for x, mod in zip(xs, mods):
        b, l, _ = x.shape
        n = b * l
        cap = max(1, EC_FACTOR * n // ne)
        q, k, v, u = _inproj(x, mod, row(norm1_g), w_in_b, seg, gq_t, gk_t)
        attn = _attention(q, k, v, bias_tab)
        conv = _conv(u, conv_w.astype(f32), row(conv_b), row(conv_ln_g), row(conv_ln_b), seg)
        x1, h2, aff_t = _outproj(attn, conv, x, mod, w_out_b, row(norm2_g), w_router)
        aff3 = aff_t.reshape(ne, n // LANES, LANES)
        cs, cbk = _route(aff3, cap, TBF)
        xe, g = _sc_dispatch(cs.reshape(-1), aff3.reshape(-1), h2, ne, cap)
        staged.append((x1, mod, xe, g.reshape(ne, cap), cs.reshape(ne, n), cbk, cap))

    outs = []
    for x1, mod, xe, g, cs, cbk, cap in staged:
        ye = _experts(xe, g, w_gate, w_up, w_down, min(TM_EXPERT, cap), min(FC_EXPERT, w_gate.shape[2]))
        outs.append(_combine(cbk, cs, x1, mod, ye, cap, TBF))
    return tuple(outs)
```

```python
import functools

import jax
import jax.numpy as jnp
import numpy as np
from jax import lax
from jax.experimental import pallas as pl
from jax.experimental.pallas import tpu as pltpu
from jax.experimental.pallas import tpu_sc as plsc

f32 = jnp.float32
bf16 = jnp.bfloat16
i32 = jnp.int32

GRID_W = 64
N_HEADS = 8
HEAD_DIM = 64
ATTN_W = N_HEADS * HEAD_DIM
CONV_W = 512
CONV_K = 31
CONV_GROUP = 64
WIN_ROWS = 8
WIN_COLS = 16
EC_FACTOR = 2
EPS = 1e-6
NEG = -1e30

LANES = 128
BF16_ROWS = 16
VMEM_LIMIT = 56 * 1024 * 1024

NT_DIMS = (((1,), (1,)), ((), ()))


def _params(*sem):
    return pltpu.CompilerParams(dimension_semantics=sem, vmem_limit_bytes=VMEM_LIMIT)


def _split_bf16(x):
    hi = x.astype(bf16)
    lo = (x - hi.astype(f32)).astype(bf16)
    return hi, lo


def _ada_kernel(c_ref, w_ref, b_ref, o_ref):
    c = c_ref[...]
    s = c * jax.nn.sigmoid(c)
    s_hi, s_lo = _split_bf16(s)
    w_hi, w_lo = _split_bf16(w_ref[...])
    acc = jnp.dot(s_hi, w_hi, preferred_element_type=f32)
    acc += jnp.dot(s_hi, w_lo, preferred_element_type=f32)
    acc += jnp.dot(s_lo, w_hi, preferred_element_type=f32)
    o_ref[...] = acc + b_ref[...]


def _ada(c_all, ada_w, ada_b):
    rows, d = c_all.shape
    ncols = ada_w.shape[1]
    tn = next(t for t in (1024, 512, 256, 128) if ncols % t == 0)
    return pl.pallas_call(
        _ada_kernel,
        grid=(ncols // tn,),
        in_specs=[pl.BlockSpec((rows, d), lambda j: (0, 0)),
                  pl.BlockSpec((d, tn), lambda j: (0, j)),
                  pl.BlockSpec((1, tn), lambda j: (0, j))],
        out_specs=pl.BlockSpec((rows, tn), lambda j: (0, j)),
        out_shape=jax.ShapeDtypeStruct((rows, ncols), f32),
        compiler_params=_params("arbitrary"),
        name="ada",
    )(c_all, ada_w, ada_b.reshape(1, ncols))


def _inproj_kernel(x_ref, mod_ref, g1_ref, w_ref, seg_ref, gq_ref, gk_ref,
                   q_ref, k_ref, v_ref, u_ref):
    x = x_ref[...]
    d = x.shape[-1]
    ms = jnp.mean(x * x, axis=-1, keepdims=True)
    h = x * lax.rsqrt(ms + EPS) * g1_ref[...]
    shift = mod_ref[:, 0:d]
    scale = mod_ref[:, d:2 * d]
    h = h * (1.0 + scale) + shift
    z = jnp.dot(h.astype(bf16), w_ref[...], preferred_element_type=f32)
    zq = z[:, 0:ATTN_W]
    zk = z[:, ATTN_W:2 * ATTN_W]
    seg = seg_ref[...]
    ssq_q = jnp.dot((zq * zq).astype(bf16), seg, preferred_element_type=f32)
    ssq_k = jnp.dot((zk * zk).astype(bf16), seg, preferred_element_type=f32)
    qs = gq_ref[...] * gk_ref[...] * (HEAD_DIM ** -0.5)
    q_ref[...] = (zq * lax.rsqrt(ssq_q * (1.0 / HEAD_DIM) + EPS) * qs).astype(bf16)
    k_ref[...] = (zk * lax.rsqrt(ssq_k * (1.0 / HEAD_DIM) + EPS)).astype(bf16)
    v_ref[...] = z[:, 2 * ATTN_W:3 * ATTN_W].astype(bf16)
    a = z[:, 3 * ATTN_W:3 * ATTN_W + CONV_W]
    gt = z[:, 3 * ATTN_W + CONV_W:]
    u_ref[...] = a * jax.nn.sigmoid(gt)


def _inproj(x, mod, norm1_g, w_in_b, seg, gq_t, gk_t):
    b, l, d = x.shape
    tm = min(512, l)
    ncols = w_in_b.shape[1]
    tok = lambda w: pl.BlockSpec((None, tm, w), lambda bi, i: (bi, i, 0))
    full = lambda a: pl.BlockSpec(a.shape, lambda bi, i: (0,) * a.ndim)
    return pl.pallas_call(
        _inproj_kernel,
        grid=(b, l // tm),
        in_specs=[tok(d),
                  pl.BlockSpec((None, 1, mod.shape[-1]), lambda bi, i: (bi, 0, 0)),
                  full(norm1_g), full(w_in_b), full(seg), full(gq_t), full(gk_t)],
        out_specs=[tok(ATTN_W), tok(ATTN_W), tok(ATTN_W), tok(CONV_W)],
        out_shape=[jax.ShapeDtypeStruct((b, l, ATTN_W), bf16)] * 3
                  + [jax.ShapeDtypeStruct((b, l, CONV_W), f32)],
        compiler_params=_params("arbitrary", "arbitrary"),
        name="inproj",
    )(x, mod, norm1_g, w_in_b, seg, gq_t, gk_t)


def _attn_kernel(q_ref, k_ref, v_ref, bias_ref, o_ref, *, rows, rg, wr, nr):
    g = pl.program_id(1)
    lane = lax.broadcasted_iota(i32, (GRID_W, LANES), 1)
    first_head = lane < HEAD_DIM
    nkeys = wr * GRID_W

    pairs = [(rr, p) for rr in range(nr) for p in range(N_HEADS // 2)]

    def rows_body(it, carry):
        geo = []
        for rr in range(nr):
            lr = it * nr + rr
            r = g * rg + lr
            rs = jnp.clip(r - wr // 2, 0, rows - wr)
            geo.append((pl.multiple_of(lr * GRID_W, GRID_W),
                        pl.multiple_of(rs * GRID_W, GRID_W),
                        rs - r + (WIN_ROWS - 1)))
        scores = []
        for rr, p in pairs:
            q0, tok0, rb = geo[rr]
            cols = slice(p * LANES, (p + 1) * LANES)
            qp = q_ref[pl.ds(q0, GRID_W), cols]
            kp = k_ref[pl.ds(tok0, nkeys), cols]
            zero = jnp.zeros_like(qp)
            q2 = jnp.concatenate([jnp.where(first_head, qp, zero),
                                  jnp.where(first_head, zero, qp)], axis=0)
            s = lax.dot_general(q2, kp, NT_DIMS, preferred_element_type=f32)
            bias = jnp.concatenate(
                [jnp.concatenate([bias_ref[2 * p + j, rb + 2 * m] for m in range(nkeys // LANES)],
                                 axis=-1) for j in range(2)], axis=0)
            scores.append(s + bias)
        probs = []
        for s in scores:
            mx = jnp.max(s, axis=-1, keepdims=True)
            e = jnp.exp(s - mx)
            probs.append((e.astype(bf16), jnp.sum(e, axis=-1, keepdims=True)))
        for (rr, p), (e, den) in zip(pairs, probs):
            q0, tok0, rb = geo[rr]
            cols = slice(p * LANES, (p + 1) * LANES)
            vp = v_ref[pl.ds(tok0, nkeys), cols]
            o = jnp.dot(e, vp, preferred_element_type=f32) / den
            o_ref[pl.ds(q0, GRID_W), cols] = jnp.where(
                first_head, o[0:GRID_W], o[GRID_W:]).astype(bf16)
        return carry

    lax.fori_loop(0, rg // nr, rows_body, 0)


def _attention(q, k, v, bias_tab):
    b, l, _ = q.shape
    rows = l // GRID_W
    wr = min(WIN_ROWS, rows)
    rg = min(8, rows)
    once = pl.Buffered(1)
    kern = functools.partial(_attn_kernel, rows=rows, rg=rg, wr=wr, nr=4)
    return pl.pallas_call(
        kern,
        grid=(b, rows // rg),
        in_specs=[pl.BlockSpec((None, rg * GRID_W, ATTN_W), lambda bi, g: (bi, g, 0)),
                  pl.BlockSpec((None, l, ATTN_W), lambda bi, g: (bi, 0, 0), pipeline_mode=once),
                  pl.BlockSpec((None, l, ATTN_W), lambda bi, g: (bi, 0, 0), pipeline_mode=once),
                  pl.BlockSpec(bias_tab.shape, lambda bi, g: (0, 0, 0, 0), pipeline_mode=once)],
        out_specs=pl.BlockSpec((None, rg * GRID_W, ATTN_W), lambda bi, g: (bi, g, 0)),
        out_shape=jax.ShapeDtypeStruct((b, l, ATTN_W), bf16),
        compiler_params=_params("arbitrary", "arbitrary"),
        name="attention",
    )(q, k, v, bias_tab)


def _bias_table(rpb):
    cols = np.arange(GRID_W)
    col_start = np.clip(cols - WIN_COLS // 2, 0, GRID_W - WIN_COLS)
    kc = np.arange(GRID_W)[None, :]
    valid = (kc >= col_start[:, None]) & (kc < col_start[:, None] + WIN_COLS)
    off = kc - cols[:, None] + (WIN_COLS - 1)
    pick = (off[None] == np.arange(2 * WIN_COLS - 1)[:, None, None]) & valid[None]
    tab = jnp.einsum('hrk,kcd->hrcd', rpb.astype(f32), jnp.asarray(pick, f32),
                     precision=lax.Precision.HIGHEST)
    tab = jnp.where(jnp.asarray(valid)[None, None], tab, NEG)
    return jnp.concatenate([tab[:, :-1], tab[:, 1:]], axis=-1)


def _conv_kernel(u_ref, prev_ref, next_ref, w_ref, cb_ref, lg_ref, lb_ref, seg_ref, o_ref,
                 win_ref, sh_ref, *, tl, chunk):
    i = pl.program_id(1)
    n_i = pl.num_programs(1)
    halo = prev_ref.shape[0]
    sub = 8
    win_ref[0:halo, :] = jnp.where(i > 0, prev_ref[...], 0.0)
    win_ref[halo:halo + tl, :] = u_ref[...]
    win_ref[halo + tl:, :] = jnp.where(i < n_i - 1, next_ref[...], 0.0)
    sh_rows = sh_ref.shape[1]
    for r in range(1, sub):
        for k0 in range(0, sh_rows, chunk):
            k1 = min(k0 + chunk, sh_rows)
            sh_ref[r - 1, k0:k1, :] = win_ref[r + k0:r + k1, :]
    seg = seg_ref[...]
    pad = CONV_K // 2
    for c in range(tl // chunk):
        acc = jnp.zeros((chunk, CONV_W), f32)
        for j in range(CONV_K):
            start = halo - pad + c * chunk + j
            r = start % sub
            src = win_ref if r == 0 else sh_ref.at[r - 1]
            acc = acc + src[start - r:start - r + chunk, :] * w_ref[j:j + 1, :]
        y = acc + cb_ref[...]
        y_hi, y_lo = _split_bf16(y)
        mu = (jnp.dot(y_hi, seg, preferred_element_type=f32)
              + jnp.dot(y_lo, seg, preferred_element_type=f32)) * (1.0 / CONV_GROUP)
        dlt = y - mu
        var = jnp.dot((dlt * dlt).astype(bf16), seg, preferred_element_type=f32) * (1.0 / CONV_GROUP)
        un = dlt * lax.rsqrt(var + EPS) * lg_ref[...] + lb_ref[...]
        o_ref[c * chunk:(c + 1) * chunk, :] = (un * jax.nn.sigmoid(un)).astype(bf16)


def _conv(u, conv_w, conv_b, ln_g, ln_b, seg):
    b, l, _ = u.shape
    tl = min(512, l)
    halo = 16
    nh = l // halo
    kern = functools.partial(_conv_kernel, tl=tl, chunk=min(128, tl))
    full = lambda a: pl.BlockSpec(a.shape, lambda bi, i: (0,) * a.ndim)
    return pl.pallas_call(
        kern,
        grid=(b, l // tl),
        in_specs=[pl.BlockSpec((None, tl, CONV_W), lambda bi, i: (bi, i, 0)),
                  pl.BlockSpec((None, halo, CONV_W),
                               lambda bi, i: (bi, jnp.maximum(i * (tl // halo) - 1, 0), 0)),
                  pl.BlockSpec((None, halo, CONV_W),
                               lambda bi, i: (bi, jnp.minimum((i + 1) * (tl // halo), nh - 1), 0)),
                  full(conv_w), full(conv_b), full(ln_g), full(ln_b), full(seg)],
        out_specs=pl.BlockSpec((None, tl, CONV_W), lambda bi, i: (bi, i, 0)),
        out_shape=jax.ShapeDtypeStruct((b, l, CONV_W), bf16),
        scratch_shapes=[pltpu.VMEM((tl + 2 * halo, CONV_W), f32),
                        pltpu.VMEM((7, tl + 2 * halo - 8, CONV_W), f32)],
        compiler_params=_params("arbitrary", "arbitrary"),
        name="conv",
    )(u, u, u, conv_w, conv_b, ln_g, ln_b, seg)


def _outproj_kernel(at_ref, cv_ref, x_ref, mod_ref, wo_ref, g2_ref, wr_ref,
                    x1_ref, h2_ref, aff_ref, *, nparts):
    tm, d = x_ref.shape
    rpt = d // LANES
    gate1 = mod_ref[:, 2 * d:3 * d]
    shift2 = mod_ref[:, 3 * d:4 * d]
    scale2 = mod_ref[:, 4 * d:5 * d]
    w_hi, w_lo = _split_bf16(wr_ref[...])
    parts = [slice(p * (tm // nparts), (p + 1) * (tm // nparts)) for p in range(nparts)]
    mixes = [jnp.dot(at_ref[rows, :], wo_ref[0:ATTN_W, :], preferred_element_type=f32)
             + jnp.dot(cv_ref[rows, :], wo_ref[ATTN_W:, :], preferred_element_type=f32)
             for rows in parts]
    splits = []
    for rows, mix in zip(parts, mixes):
        x1 = x_ref[rows, :] + gate1 * mix
        x1_ref[rows, :] = x1
        ms = jnp.mean(x1 * x1, axis=-1, keepdims=True)
        h2 = x1 * lax.rsqrt(ms + EPS) * g2_ref[...]
        h2 = h2 * (1.0 + scale2) + shift2
        h_hi, h_lo = _split_bf16(h2)
        h2_ref[rows, :] = h_hi.astype(f32)
        splits.append((h_hi, h_lo))
    lgs = [jnp.dot(h_hi, w_hi, preferred_element_type=f32)
           + jnp.dot(h_lo, w_hi, preferred_element_type=f32)
           + jnp.dot(h_hi, w_lo, preferred_element_type=f32) for h_hi, h_lo in splits]
    for rows, lg in zip(parts, lgs):
        logits = lg.T[0:aff_ref.shape[0], :]
        mx = jnp.max(logits, axis=0, keepdims=True)
        ex = jnp.exp(logits - mx)
        aff_ref[:, rows] = ex / jnp.sum(ex, axis=0, keepdims=True)


def _outproj(attn, conv, x, mod, w_out_b, norm2_g, w_router):
    b, l, d = x.shape
    tm = min(512, l)
    nt = l // tm
    e = w_router.shape[1]
    w_router_t = jnp.pad(w_router.astype(f32), ((0, 0), (0, LANES - e)))
    tok = lambda w: pl.BlockSpec((None, tm, w), lambda bi, i: (bi, i, 0))
    full = lambda a: pl.BlockSpec(a.shape, lambda bi, i: (0,) * a.ndim)
    return pl.pallas_call(
        functools.partial(_outproj_kernel, nparts=2),
        grid=(b, nt),
        in_specs=[tok(ATTN_W), tok(CONV_W), tok(d),
                  pl.BlockSpec((None, 1, mod.shape[-1]), lambda bi, i: (bi, 0, 0)),
                  full(w_out_b), full(norm2_g), full(w_router_t)],
        out_specs=[tok(d),
                   pl.BlockSpec((tm, d), lambda bi, i: (bi * nt + i, 0)),
                   pl.BlockSpec((e, tm), lambda bi, i: (0, bi * nt + i))],
        out_shape=[jax.ShapeDtypeStruct((b, l, d), f32),
                   jax.ShapeDtypeStruct((b * l, d), f32),
                   jax.ShapeDtypeStruct((e, b * l), f32)],
        compiler_params=_params("arbitrary", "arbitrary"),
        name="outproj",
    )(attn, conv, x, mod, w_out_b, norm2_g, w_router_t)


def _route_kernel(aff_ref, cs_ref, cbk_ref, *, cap, tbc):
    a = aff_ref[...]
    ne, nc, _ = a.shape

    def enough(v):
        return jnp.sum((a >= v).astype(f32), axis=(1, 2), keepdims=True) >= cap

    def pow2(k):
        return lax.bitcast_convert_type(jnp.left_shift(k + 127, 23), f32)

    def exp_body(_, st):
        klo, khi = st
        kmid = jnp.right_shift(klo + khi, 1)
        ok = enough(pow2(kmid))
        return jnp.where(ok, kmid, klo), jnp.where(ok, khi, kmid)

    klo, khi = lax.fori_loop(0, 7, exp_body,
                             (jnp.full((ne, 1, 1), -127, i32), jnp.full((ne, 1, 1), 1, i32)))

    def lin_body(_, st):
        lo, hi = st
        mid = lo + (hi - lo) * 0.5
        ok = enough(mid)
        return jnp.where(ok, mid, lo), jnp.where(ok, hi, mid)

    lo, _ = lax.fori_loop(0, 48, lin_body, (pow2(klo), pow2(khi)))
    thr = jnp.min(jnp.where(a >= lo, a, jnp.inf), axis=(1, 2), keepdims=True)

    r_i = lax.broadcasted_iota(i32, (LANES, LANES), 0)
    c_i = lax.broadcasted_iota(i32, (LANES, LANES), 1)
    upper = (r_i <= c_i).astype(bf16)
    ones = jnp.ones((LANES, LANES), bf16)
    rr = lax.broadcasted_iota(i32, (nc, nc), 0)
    cc = lax.broadcasted_iota(i32, (nc, nc), 1)
    lower = (cc < rr).astype(bf16)
    blk_r = lax.broadcasted_iota(i32, (nc, LANES), 0)
    blk_c = lax.broadcasted_iota(i32, (nc, LANES), 1)
    before_blk = (blk_r < blk_c * (tbc // LANES)).astype(bf16)
    ones8 = jnp.ones((8, LANES), bf16)

    def cumsum(m):
        within = jnp.dot(m, upper, preferred_element_type=f32)
        prev = jnp.dot(lower, m, preferred_element_type=f32)
        offs = jnp.dot(prev.astype(bf16), ones, preferred_element_type=f32)
        return within + offs

    for e in range(ne):
        a_e = a[e]
        thr_e = thr[e]
        gt_e = a_e > thr_e
        eq_e = (a_e == thr_e).astype(f32)
        need = cap - jnp.sum(gt_e.astype(f32), keepdims=True)
        rank_eq = cumsum(eq_e.astype(bf16)) - eq_e
        sel = jnp.logical_or(gt_e, jnp.logical_and(eq_e > 0.0, rank_eq < need))
        sel_b = sel.astype(bf16)
        c_incl = cumsum(sel_b)
        cs_ref[e] = jnp.where(sel, c_incl, 0.0).astype(i32)
        per_chunk = lax.dot_general(ones8, sel_b, NT_DIMS, preferred_element_type=f32)
        cbk = jnp.dot(per_chunk.astype(bf16), before_blk, preferred_element_type=f32)
        cbk_ref[e:e + 1, :] = cbk[0:1].astype(i32)


def _route(aff3, cap, tbc):
    e, nc, _ = aff3.shape
    kern = functools.partial(_route_kernel, cap=cap, tbc=tbc)
    return pl.pallas_call(
        kern,
        out_shape=[jax.ShapeDtypeStruct((e, nc, LANES), i32),
                   jax.ShapeDtypeStruct((e, LANES), i32)],
        compiler_params=pltpu.CompilerParams(vmem_limit_bytes=VMEM_LIMIT),
        name="route",
    )(aff3)


SC_CORES = 2
SC_SUBCORES = 16
SC_LANES = 16


SC_CHUNK = 32


def _sc_dispatch(cs_flat, aff_flat, h_tiles, ne, cap):
    n = cs_flat.shape[0] // ne
    workers = SC_CORES * SC_SUBCORES
    per_e = workers // ne
    per_w = cap // per_e
    assert per_e * ne == workers and per_w * per_e == cap
    assert n % SC_LANES == 0 and per_w % SC_CHUNK == 0
    table = h_tiles
    d_model = table.shape[1]
    mesh = plsc.VectorSubcoreMesh(core_axis_name="c", subcore_axis_name="s",
                                  num_cores=SC_CORES, num_subcores=SC_SUBCORES)

    @functools.partial(
        pl.kernel, mesh=mesh,
        out_type=[jax.ShapeDtypeStruct((ne * cap, d_model), f32),
                  jax.ShapeDtypeStruct((ne * cap,), f32)],
        scratch_types=[pltpu.VMEM((n,), i32), pltpu.VMEM((n,), f32),
                       pltpu.VMEM((per_w,), i32), pltpu.VMEM((per_w,), f32),
                       pltpu.VMEM((SC_CHUNK, d_model), f32), pltpu.SemaphoreType.DMA],
        compiler_params=pltpu.CompilerParams(needs_layout_passes=False, use_tc_tiling_on_sc=True),
        name="sc_dispatch")
    def dispatch(cs_hbm, aff_hbm, table_hbm, xe_hbm, g_hbm, cs_v, aff_v, idx_v, g_v, rows_v, sem):
        wid = lax.axis_index("s") * SC_CORES + lax.axis_index("c")
        e = wid // per_e
        lo = (wid - e * per_e) * per_w
        base = e * cap + lo
        pltpu.sync_copy(cs_hbm.at[pl.ds(e * n, n)], cs_v)
        pltpu.sync_copy(aff_hbm.at[pl.ds(e * n, n)], aff_v)
        lane = lax.iota(i32, SC_LANES)

        @pl.loop(0, n // SC_LANES)
        def _(i):
            off = pl.multiple_of(i * SC_LANES, SC_LANES)
            local = cs_v[pl.ds(off, SC_LANES)] - 1 - lo
            mine = jnp.logical_and(local >= 0, local < per_w)
            plsc.store_scatter(idx_v, [local], off + lane, mask=mine)
            plsc.store_scatter(g_v, [local], aff_v[pl.ds(off, SC_LANES)], mask=mine)

        pltpu.sync_copy(g_v, g_hbm.at[pl.ds(base, per_w)])

        @pl.loop(0, per_w // SC_CHUNK)
        def _(c):
            off = pl.multiple_of(c * SC_CHUNK, SC_CHUNK)
            pltpu.async_copy(table_hbm.at[idx_v.at[pl.ds(off, SC_CHUNK)]], rows_v, sem).wait()
            pltpu.sync_copy(rows_v, xe_hbm.at[pl.ds(base + off, SC_CHUNK)])

    return dispatch(cs_flat, aff_flat, table)


def _expert_kernel(x_ref, g_ref, wg_ref, wu_ref, wd_ref, y_ref, xb_ref, acc_ref, gcol_ref):
    f = pl.program_id(2)

    @pl.when(f == 0)
    def _():
        xb_ref[...] = x_ref[...].astype(bf16)
        acc_ref[...] = jnp.zeros_like(acc_ref)
        gcol_ref[...] = jnp.broadcast_to(g_ref[...], (LANES, g_ref.shape[-1])).T

    xb = xb_ref[...]
    hg = jnp.dot(xb, wg_ref[...].astype(bf16), preferred_element_type=f32)
    hu = jnp.dot(xb, wu_ref[...].astype(bf16), preferred_element_type=f32)
    hid = (hg * jax.nn.sigmoid(hg) * hu).astype(bf16)
    acc_ref[...] += jnp.dot(hid, wd_ref[...].astype(bf16), preferred_element_type=f32)

    @pl.when(f == pl.num_programs(2) - 1)
    def _():
        gcol = gcol_ref[...]
        d = acc_ref.shape[-1]
        for c in range(d // LANES):
            cols = slice(c * LANES, (c + 1) * LANES)
            y_ref[:, cols] = (acc_ref[:, cols] * gcol).astype(bf16)


def _experts(xe, g_rows, wg_b, wu_b, wd_b, tm, fc):
    e, d, fdim = wg_b.shape
    rows = xe.shape[0]
    per_e = rows // e
    nt = per_e // tm
    return pl.pallas_call(
        _expert_kernel,
        grid=(e, nt, fdim // fc),
        in_specs=[pl.BlockSpec((tm, d), lambda ei, t, f: (ei * nt + t, 0)),
                  pl.BlockSpec((None, 1, tm), lambda ei, t, f: (ei * nt + t, 0, 0)),
                  pl.BlockSpec((None, d, fc), lambda ei, t, f: (ei, 0, f)),
                  pl.BlockSpec((None, d, fc), lambda ei, t, f: (ei, 0, f)),
                  pl.BlockSpec((None, fc, d), lambda ei, t, f: (ei, f, 0))],
        out_specs=pl.BlockSpec((tm, d), lambda ei, t, f: (ei * nt + t, 0)),
        out_shape=jax.ShapeDtypeStruct((rows, d), bf16),
        scratch_shapes=[pltpu.VMEM((tm, d), bf16), pltpu.VMEM((tm, d), f32),
                        pltpu.VMEM((tm, LANES), f32)],
        compiler_params=_params("arbitrary", "arbitrary", "arbitrary"),
        name="experts",
    )(xe, g_rows.reshape(rows // tm, 1, tm), wg_b, wu_b, wd_b)


def _combine_kernel(cbk_ref, cs_ref, x1_ref, mod_ref, ye_ref, o_ref, win_ref, xwin_ref, sem, xsem,
                    acc_ref, *, cap, tbc, win, ne, total_rows):
    blk = pl.program_id(0)
    nblk = pl.num_programs(0)
    d = acc_ref.shape[-1]
    slot = blk % 2

    def region(ee):
        return ee * cap

    def first_row(bb, ee):
        st = cbk_ref[ee * LANES + bb]
        row = region(ee) + jnp.left_shift(jnp.right_shift(st, 4), 4)
        return pl.multiple_of(jnp.minimum(row, total_rows - win), BF16_ROWS)

    def start_block(bb, sl):
        for ee in range(ne):
            pltpu.make_async_copy(ye_ref.at[pl.ds(first_row(bb, ee), win)],
                                  win_ref.at[sl, pl.ds(ee * win, win)], sem.at[sl]).start()

    @pl.when(blk == 0)
    def _():
        start_block(blk, slot)

    @pl.when(blk + 1 < nblk)
    def _():
        start_block(blk + 1, (blk + 1) % 2)

    pltpu.make_async_copy(ye_ref.at[pl.ds(0, ne * win)], win_ref.at[slot], sem.at[slot]).wait()

    w_iota = lax.broadcasted_iota(i32, (win, tbc), 0)

    def onehot_of(cs_row, base_slot, done_slots):
        hit = jnp.logical_and(cs_row == base_slot + 1 + w_iota, cs_row > done_slots)
        return hit.astype(f32).T.astype(bf16)

    onehot = jnp.concatenate(
        [onehot_of(cs_ref[ee:ee + 1, :], first_row(blk, ee) - region(ee), 0) for ee in range(ne)],
        axis=1)
    acc_ref[...] = jnp.dot(onehot, win_ref[slot], preferred_element_type=f32)

    def extra_expert(ee, carry):
        row0 = first_row(blk, ee)
        base0 = row0 - region(ee)
        end = cbk_ref[ee * LANES + blk + 1]
        cs_row = cs_ref[pl.ds(ee, 1), :]

        def extra(w, c2):
            row_w = pl.multiple_of(jnp.minimum(row0 + w * win, total_rows - win), BF16_ROWS)
            cp = pltpu.make_async_copy(ye_ref.at[pl.ds(row_w, win)], xwin_ref, xsem)
            cp.start()
            cp.wait()
            oh = onehot_of(cs_row, row_w - region(ee), base0 + w * win)
            acc_ref[...] += jnp.dot(oh, xwin_ref[...], preferred_element_type=f32)
            return c2

        lax.fori_loop(1, (end - base0 + win - 1) // win, extra, 0)
        return carry

    lax.fori_loop(0, ne, extra_expert, 0)

    gate2 = mod_ref[:, 5 * d:6 * d]
    o_ref[...] = x1_ref[...] + gate2 * acc_ref[...]


def _combine(cbk, cs, x1, mod, ye, cap, tbc):
    b, l, d = x1.shape
    n = b * l
    e = cs.shape[0]
    nb = n // tbc
    win = min(128, cap)
    per_b = l // tbc
    kern = functools.partial(_combine_kernel, cap=cap, tbc=tbc, win=win, ne=e,
                             total_rows=ye.shape[0])
    return pl.pallas_call(
        kern,
        grid_spec=pltpu.PrefetchScalarGridSpec(
            num_scalar_prefetch=1, grid=(nb,),
            in_specs=[pl.BlockSpec((e, tbc), lambda bk, c: (0, bk)),
                      pl.BlockSpec((tbc, d), lambda bk, c: (bk, 0)),
                      pl.BlockSpec((None, 1, mod.shape[-1]), lambda bk, c: (bk // per_b, 0, 0)),
                      pl.BlockSpec(memory_space=pl.ANY)],
            out_specs=pl.BlockSpec((tbc, d), lambda bk, c: (bk, 0)),
            scratch_shapes=[pltpu.VMEM((2, e * win, d), bf16), pltpu.VMEM((win, d), bf16),
                            pltpu.SemaphoreType.DMA((2,)), pltpu.SemaphoreType.DMA,
                            pltpu.VMEM((tbc, d), f32)]),
        out_shape=jax.ShapeDtypeStruct((n, d), f32),
        compiler_params=_params("arbitrary"),
        name="combine",
    )(cbk.reshape(-1), cs, x1.reshape(n, d), mod, ye).reshape(b, l, d)


TBF = 512
TM_EXPERT = 2048
FC_EXPERT = 256


def kernel(x_prompt, x_sample, c_prompt, c_sample, ada_w, ada_b, norm1_g, w_in, q_norm_g, k_norm_g, rpb,
           conv_w, conv_b, conv_ln_g, conv_ln_b, w_out, norm2_g, w_router, w_gate, w_up, w_down):
    xs = (x_prompt, x_sample)
    d = x_prompt.shape[-1]
    ne = w_router.shape[1]
    row = lambda a: a.reshape(1, -1).astype(f32)

    nb0, nb1 = c_prompt.shape[0], c_sample.shape[0]
    pad = (-(nb0 + nb1)) % 8
    c_all = jnp.concatenate([c_prompt, c_sample, jnp.zeros((pad, d), f32)], axis=0)
    mod_all = _ada(c_all, ada_w, ada_b)
    mods = (mod_all[:nb0].reshape(nb0, 1, -1), mod_all[nb0:nb0 + nb1].reshape(nb1, 1, -1))

    seg_i = jnp.arange(ATTN_W) // HEAD_DIM
    seg = (seg_i[:, None] == seg_i[None, :]).astype(bf16)
    gq_t = jnp.tile(q_norm_g.astype(f32), N_HEADS).reshape(1, ATTN_W)
    gk_t = jnp.tile(k_norm_g.astype(f32), N_HEADS).reshape(1, ATTN_W)
    bias_tab = _bias_table(rpb)
    w_in_b = w_in.astype(bf16)
    w_out_b = w_out.astype(bf16)

    staged = []
    for x, mod in zip(xs, mods):
        b, l, _ = x.shape
        n = b * l
        cap = max(1, EC_FACTOR * n // ne)
        q, k, v, u = _inproj(x, mod, row(norm1_g), w_in_b, seg, gq_t, gk_t)
        attn = _attention(q, k, v, bias_tab)
        conv = _conv(u, conv_w.astype(f32), row(conv_b), row(conv_ln_g), row(conv_ln_b), seg)
        x1, h2, aff_t = _outproj(attn, conv, x, mod, w_out_b, row(norm2_g), w_router)
        aff3 = aff_t.reshape(ne, n // LANES, LANES)
        cs, cbk = _route(aff3, cap, TBF)
        xe, g = _sc_dispatch(cs.reshape(-1), aff3.reshape(-1), h2, ne, cap)
        staged.append((x1, mod, xe, g.reshape(ne, cap), cs.reshape(ne, n), cbk, cap))

    outs = []
    for x1, mod, xe, g, cs, cbk, cap in staged:
        ye = _experts(xe, g, w_gate, w_up, w_down, min(TM_EXPERT, cap), min(FC_EXPERT, w_gate.shape[2]))
        outs.append(_combine(cbk, cs, x1, mod, ye, cap, TBF))
    return tuple(outs)
```

```python
import functools

import jax
import jax.numpy as jnp
import numpy as np
from jax import lax
from jax.experimental import pallas as pl
from jax.experimental.pallas import tpu as pltpu
from jax.experimental.pallas import tpu_sc as plsc

f32 = jnp.float32
bf16 = jnp.bfloat16
i32 = jnp.int32

GRID_W = 64
N_HEADS = 8
HEAD_DIM = 64
ATTN_W = N_HEADS * HEAD_DIM
CONV_W = 512
CONV_K = 31
CONV_GROUP = 64
WIN_ROWS = 8
WIN_COLS = 16
EC_FACTOR = 2
EPS = 1e-6
NEG = -1e30

LANES = 128
BF16_ROWS = 16
VMEM_LIMIT = 56 * 1024 * 1024

NT_DIMS = (((1,), (1,)), ((), ()))


def _params(*sem):
    return pltpu.CompilerParams(dimension_semantics=sem, vmem_limit_bytes=VMEM_LIMIT)


def _split_bf16(x):
    hi = x.astype(bf16)
    lo = (x - hi.astype(f32)).astype(bf16)
    return hi, lo


def _ada_kernel(c_ref, w_ref, b_ref, o_ref):
    c = c_ref[...]
    s = c * jax.nn.sigmoid(c)
    s_hi, s_lo = _split_bf16(s)
    w_hi, w_lo = _split_bf16(w_ref[...])
    acc = jnp.dot(s_hi, w_hi, preferred_element_type=f32)
    acc += jnp.dot(s_hi, w_lo, preferred_element_type=f32)
    acc += jnp.dot(s_lo, w_hi, preferred_element_type=f32)
    o_ref[...] = acc + b_ref[...]


def _ada(c_all, ada_w, ada_b):
    rows, d = c_all.shape
    ncols = ada_w.shape[1]
    tn = next(t for t in (1024, 512, 256, 128) if ncols % t == 0)
    return pl.pallas_call(
        _ada_kernel,
        grid=(ncols // tn,),
        in_specs=[pl.BlockSpec((rows, d), lambda j: (0, 0)),
                  pl.BlockSpec((d, tn), lambda j: (0, j)),
                  pl.BlockSpec((1, tn), lambda j: (0, j))],
        out_specs=pl.BlockSpec((rows, tn), lambda j: (0, j)),
        out_shape=jax.ShapeDtypeStruct((rows, ncols), f32),
        compiler_params=_params("arbitrary"),
        name="ada",
    )(c_all, ada_w, ada_b.reshape(1, ncols))


def _inproj_kernel(x_ref, mod_ref, g1_ref, w_ref, seg_ref, gq_ref, gk_ref,
                   q_ref, k_ref, v_ref, u_ref, *, nparts):
    tm, d = x_ref.shape
    shift = mod_ref[:, 0:d]
    scale = mod_ref[:, d:2 * d]
    seg = seg_ref[...]
    qs = gq_ref[...] * gk_ref[...] * (HEAD_DIM ** -0.5)
    parts = [slice(p * (tm // nparts), (p + 1) * (tm // nparts)) for p in range(nparts)]
    hs = []
    for rows in parts:
        x = x_ref[rows, :]
        ms = jnp.mean(x * x, axis=-1, keepdims=True)
        h = x * lax.rsqrt(ms + EPS) * g1_ref[...]
        hs.append((h * (1.0 + scale) + shift).astype(bf16))
    zs = [jnp.dot(h, w_ref[...], preferred_element_type=f32) for h in hs]
    for rows, z in zip(parts, zs):
        zq = z[:, 0:ATTN_W]
        zk = z[:, ATTN_W:2 * ATTN_W]
        ssq_q = jnp.dot((zq * zq).astype(bf16), seg, preferred_element_type=f32)
        ssq_k = jnp.dot((zk * zk).astype(bf16), seg, preferred_element_type=f32)
        q_ref[rows, :] = (zq * lax.rsqrt(ssq_q * (1.0 / HEAD_DIM) + EPS) * qs).astype(bf16)
        k_ref[rows, :] = (zk * lax.rsqrt(ssq_k * (1.0 / HEAD_DIM) + EPS)).astype(bf16)
        v_ref[rows, :] = z[:, 2 * ATTN_W:3 * ATTN_W].astype(bf16)
        a = z[:, 3 * ATTN_W:3 * ATTN_W + CONV_W]
        gt = z[:, 3 * ATTN_W + CONV_W:]
        u_ref[rows, :] = a * jax.nn.sigmoid(gt)


def _inproj(x, mod, norm1_g, w_in_b, seg, gq_t, gk_t):
    b, l, d = x.shape
    tm = min(1024, l)
    ncols = w_in_b.shape[1]
    tok = lambda w: pl.BlockSpec((None, tm, w), lambda bi, i: (bi, i, 0))
    full = lambda a: pl.BlockSpec(a.shape, lambda bi, i: (0,) * a.ndim)
    return pl.pallas_call(
        functools.partial(_inproj_kernel, nparts=2),
        grid=(b, l // tm),
        in_specs=[tok(d),
                  pl.BlockSpec((None, 1, mod.shape[-1]), lambda bi, i: (bi, 0, 0)),
                  full(norm1_g), full(w_in_b), full(seg), full(gq_t), full(gk_t)],
        out_specs=[tok(ATTN_W), tok(ATTN_W), tok(ATTN_W), tok(CONV_W)],
        out_shape=[jax.ShapeDtypeStruct((b, l, ATTN_W), bf16)] * 3
                  + [jax.ShapeDtypeStruct((b, l, CONV_W), f32)],
        compiler_params=_params("arbitrary", "arbitrary"),
        name="inproj",
    )(x, mod, norm1_g, w_in_b, seg, gq_t, gk_t)


def _attn_kernel(q_ref, k_ref, v_ref, bias_ref, o_ref, *, rows, rg, wr, nr):
    g = pl.program_id(1)
    lane = lax.broadcasted_iota(i32, (GRID_W, LANES), 1)
    first_head = lane < HEAD_DIM
    nkeys = wr * GRID_W

    pairs = [(rr, p) for rr in range(nr) for p in range(N_HEADS // 2)]

    def rows_body(it, carry):
        geo = []
        for rr in range(nr):
            lr = it * nr + rr
            r = g * rg + lr
            rs = jnp.clip(r - wr // 2, 0, rows - wr)
            geo.append((pl.multiple_of(lr * GRID_W, GRID_W),
                        pl.multiple_of(rs * GRID_W, GRID_W),
                        rs - r + (WIN_ROWS - 1)))
        scores = []
        for rr, p in pairs:
            q0, tok0, rb = geo[rr]
            cols = slice(p * LANES, (p + 1) * LANES)
            qp = q_ref[pl.ds(q0, GRID_W), cols]
            kp = k_ref[pl.ds(tok0, nkeys), cols]
            zero = jnp.zeros_like(qp)
            q2 = jnp.concatenate([jnp.where(first_head, qp, zero),
                                  jnp.where(first_head, zero, qp)], axis=0)
            s = lax.dot_general(q2, kp, NT_DIMS, preferred_element_type=f32)
            bias = jnp.concatenate(
                [jnp.concatenate([bias_ref[2 * p + j, rb + 2 * m] for m in range(nkeys // LANES)],
                                 axis=-1) for j in range(2)], axis=0)
            scores.append(s + bias)
        probs = []
        for s in scores:
            mx = jnp.max(s, axis=-1, keepdims=True)
            e = jnp.exp(s - mx)
            probs.append((e.astype(bf16), jnp.sum(e, axis=-1, keepdims=True)))
        for (rr, p), (e, den) in zip(pairs, probs):
            q0, tok0, rb = geo[rr]
            cols = slice(p * LANES, (p + 1) * LANES)
            vp = v_ref[pl.ds(tok0, nkeys), cols]
            o = jnp.dot(e, vp, preferred_element_type=f32) / den
            o_ref[pl.ds(q0, GRID_W), cols] = jnp.where(
                first_head, o[0:GRID_W], o[GRID_W:]).astype(bf16)
        return carry

    lax.fori_loop(0, rg // nr, rows_body, 0)


def _attention(q, k, v, bias_tab):
    b, l, _ = q.shape
    rows = l // GRID_W
    wr = min(WIN_ROWS, rows)
    rg = min(8, rows)
    once = pl.Buffered(1)
    kern = functools.partial(_attn_kernel, rows=rows, rg=rg, wr=wr, nr=4)
    return pl.pallas_call(
        kern,
        grid=(b, rows // rg),
        in_specs=[pl.BlockSpec((None, rg * GRID_W, ATTN_W), lambda bi, g: (bi, g, 0)),
                  pl.BlockSpec((None, l, ATTN_W), lambda bi, g: (bi, 0, 0), pipeline_mode=once),
                  pl.BlockSpec((None, l, ATTN_W), lambda bi, g: (bi, 0, 0), pipeline_mode=once),
                  pl.BlockSpec(bias_tab.shape, lambda bi, g: (0, 0, 0, 0), pipeline_mode=once)],
        out_specs=pl.BlockSpec((None, rg * GRID_W, ATTN_W), lambda bi, g: (bi, g, 0)),
        out_shape=jax.ShapeDtypeStruct((b, l, ATTN_W), bf16),
        compiler_params=_params("arbitrary", "arbitrary"),
        name="attention",
    )(q, k, v, bias_tab)


def _bias_table(rpb):
    cols = np.arange(GRID_W)
    col_start = np.clip(cols - WIN_COLS // 2, 0, GRID_W - WIN_COLS)
    kc = np.arange(GRID_W)[None, :]
    valid = (kc >= col_start[:, None]) & (kc < col_start[:, None] + WIN_COLS)
    off = kc - cols[:, None] + (WIN_COLS - 1)
    pick = (off[None] == np.arange(2 * WIN_COLS - 1)[:, None, None]) & valid[None]
    tab = jnp.einsum('hrk,kcd->hrcd', rpb.astype(f32), jnp.asarray(pick, f32),
                     precision=lax.Precision.HIGHEST)
    tab = jnp.where(jnp.asarray(valid)[None, None], tab, NEG)
    return jnp.concatenate([tab[:, :-1], tab[:, 1:]], axis=-1)


def _conv_kernel(u_ref, prev_ref, next_ref, w_ref, cb_ref, lg_ref, lb_ref, seg_ref, o_ref,
                 win_ref, sh_ref, *, tl, chunk):
    i = pl.program_id(1)
    n_i = pl.num_programs(1)
    halo = prev_ref.shape[0]
    sub = 8
    win_ref[0:halo, :] = jnp.where(i > 0, prev_ref[...], 0.0)
    win_ref[halo:halo + tl, :] = u_ref[...]
    win_ref[halo + tl:, :] = jnp.where(i < n_i - 1, next_ref[...], 0.0)
    sh_rows = sh_ref.shape[1]
    for r in range(1, sub):
        for k0 in range(0, sh_rows, chunk):
            k1 = min(k0 + chunk, sh_rows)
            sh_ref[r - 1, k0:k1, :] = win_ref[r + k0:r + k1, :]
    seg = seg_ref[...]
    pad = CONV_K // 2
    for c in range(tl // chunk):
        acc = jnp.zeros((chunk, CONV_W), f32)
        for j in range(CONV_K):
            start = halo - pad + c * chunk + j
            r = start % sub
            src = win_ref if r == 0 else sh_ref.at[r - 1]
            acc = acc + src[start - r:start - r + chunk, :] * w_ref[j:j + 1, :]
        y = acc + cb_ref[...]
        y_hi, y_lo = _split_bf16(y)
        mu = (jnp.dot(y_hi, seg, preferred_element_type=f32)
              + jnp.dot(y_lo, seg, preferred_element_type=f32)) * (1.0 / CONV_GROUP)
        dlt = y - mu
        var = jnp.dot((dlt * dlt).astype(bf16), seg, preferred_element_type=f32) * (1.0 / CONV_GROUP)
        un = dlt * lax.rsqrt(var + EPS) * lg_ref[...] + lb_ref[...]
        o_ref[c * chunk:(c + 1) * chunk, :] = (un * jax.nn.sigmoid(un)).astype(bf16)


def _conv(u, conv_w, conv_b, ln_g, ln_b, seg):
    b, l, _ = u.shape
    tl = min(512, l)
    halo = 16
    nh = l // halo
    kern = functools.partial(_conv_kernel, tl=tl, chunk=min(128, tl))
    full = lambda a: pl.BlockSpec(a.shape, lambda bi, i: (0,) * a.ndim)
    return pl.pallas_call(
        kern,
        grid=(b, l // tl),
        in_specs=[pl.BlockSpec((None, tl, CONV_W), lambda bi, i: (bi, i, 0)),
                  pl.BlockSpec((None, halo, CONV_W),
                               lambda bi, i: (bi, jnp.maximum(i * (tl // halo) - 1, 0), 0)),
                  pl.BlockSpec((None, halo, CONV_W),
                               lambda bi, i: (bi, jnp.minimum((i + 1) * (tl // halo), nh - 1), 0)),
                  full(conv_w), full(conv_b), full(ln_g), full(ln_b), full(seg)],
        out_specs=pl.BlockSpec((None, tl, CONV_W), lambda bi, i: (bi, i, 0)),
        out_shape=jax.ShapeDtypeStruct((b, l, CONV_W), bf16),
        scratch_shapes=[pltpu.VMEM((tl + 2 * halo, CONV_W), f32),
                        pltpu.VMEM((7, tl + 2 * halo - 8, CONV_W), f32)],
        compiler_params=_params("arbitrary", "arbitrary"),
        name="conv",
    )(u, u, u, conv_w, conv_b, ln_g, ln_b, seg)


def _outproj_kernel(at_ref, cv_ref, x_ref, mod_ref, wo_ref, g2_ref, wr_ref,
                    x1_ref, h2_ref, aff_ref, *, nparts):
    tm, d = x_ref.shape
    gate1 = mod_ref[:, 2 * d:3 * d]
    shift2 = mod_ref[:, 3 * d:4 * d]
    scale2 = mod_ref[:, 4 * d:5 * d]
    w_hi, w_lo = _split_bf16(wr_ref[...])
    parts = [slice(p * (tm // nparts), (p + 1) * (tm // nparts)) for p in range(nparts)]
    mixes = [jnp.dot(at_ref[rows, :], wo_ref[0:ATTN_W, :], preferred_element_type=f32)
             + jnp.dot(cv_ref[rows, :], wo_ref[ATTN_W:, :], preferred_element_type=f32)
             for rows in parts]
    splits = []
    for rows, mix in zip(parts, mixes):
        x1 = x_ref[rows, :] + gate1 * mix
        x1_ref[rows, :] = x1
        ms = jnp.mean(x1 * x1, axis=-1, keepdims=True)
        h2 = x1 * lax.rsqrt(ms + EPS) * g2_ref[...]
        h2 = h2 * (1.0 + scale2) + shift2
        h_hi, h_lo = _split_bf16(h2)
        h2_ref[rows, :] = h_hi.astype(f32)
        splits.append((h_hi, h_lo))
    lgs = [jnp.dot(h_hi, w_hi, preferred_element_type=f32)
           + jnp.dot(h_lo, w_hi, preferred_element_type=f32)
           + jnp.dot(h_hi, w_lo, preferred_element_type=f32) for h_hi, h_lo in splits]
    for rows, lg in zip(parts, lgs):
        logits = lg.T[0:aff_ref.shape[0], :]
        mx = jnp.max(logits, axis=0, keepdims=True)
        ex = jnp.exp(logits - mx)
        aff_ref[:, rows] = ex / jnp.sum(ex, axis=0, keepdims=True)


def _outproj(attn, conv, x, mod, w_out_b, norm2_g, w_router):
    b, l, d = x.shape
    tm = min(1024, l)
    nt = l // tm
    e = w_router.shape[1]
    w_router_t = jnp.pad(w_router.astype(f32), ((0, 0), (0, LANES - e)))
    tok = lambda w: pl.BlockSpec((None, tm, w), lambda bi, i: (bi, i, 0))
    full = lambda a: pl.BlockSpec(a.shape, lambda bi, i: (0,) * a.ndim)
    return pl.pallas_call(
        functools.partial(_outproj_kernel, nparts=4),
        grid=(b, nt),
        in_specs=[tok(ATTN_W), tok(CONV_W), tok(d),
                  pl.BlockSpec((None, 1, mod.shape[-1]), lambda bi, i: (bi, 0, 0)),
                  full(w_out_b), full(norm2_g), full(w_router_t)],
        out_specs=[tok(d),
                   pl.BlockSpec((tm, d), lambda bi, i: (bi * nt + i, 0)),
                   pl.BlockSpec((e, tm), lambda bi, i: (0, bi * nt + i))],
        out_shape=[jax.ShapeDtypeStruct((b, l, d), f32),
                   jax.ShapeDtypeStruct((b * l, d), f32),
                   jax.ShapeDtypeStruct((e, b * l), f32)],
        compiler_params=_params("arbitrary", "arbitrary"),
        name="outproj",
    )(attn, conv, x, mod, w_out_b, norm2_g, w_router_t)


def _route_kernel(aff_ref, cs_ref, cbk_ref, *, cap, tbc):
    a = aff_ref[...]
    ne, nc, _ = a.shape

    def enough(v):
        return jnp.sum((a >= v).astype(f32), axis=(1, 2), keepdims=True) >= cap

    def pow2(k):
        return lax.bitcast_convert_type(jnp.left_shift(k + 127, 23), f32)

    def exp_body(_, st):
        klo, khi = st
        kmid = jnp.right_shift(klo + khi, 1)
        ok = enough(pow2(kmid))
        return jnp.where(ok, kmid, klo), jnp.where(ok, khi, kmid)

    klo, khi = lax.fori_loop(0, 7, exp_body,
                             (jnp.full((ne, 1, 1), -127, i32), jnp.full((ne, 1, 1), 1, i32)))

    def lin_body(_, st):
        lo, hi = st
        mid = lo + (hi - lo) * 0.5
        ok = enough(mid)
        return jnp.where(ok, mid, lo), jnp.where(ok, hi, mid)

    lo, _ = lax.fori_loop(0, 48, lin_body, (pow2(klo), pow2(khi)))
    thr = jnp.min(jnp.where(a >= lo, a, jnp.inf), axis=(1, 2), keepdims=True)

    r_i = lax.broadcasted_iota(i32, (LANES, LANES), 0)
    c_i = lax.broadcasted_iota(i32, (LANES, LANES), 1)
    upper = (r_i <= c_i).astype(bf16)
    ones = jnp.ones((LANES, LANES), bf16)
    rr = lax.broadcasted_iota(i32, (nc, nc), 0)
    cc = lax.broadcasted_iota(i32, (nc, nc), 1)
    lower = (cc < rr).astype(bf16)
    blk_r = lax.broadcasted_iota(i32, (nc, LANES), 0)
    blk_c = lax.broadcasted_iota(i32, (nc, LANES), 1)
    before_blk = (blk_r < blk_c * (tbc // LANES)).astype(bf16)
    ones8 = jnp.ones((8, LANES), bf16)

    def cumsum(m):
        within = jnp.dot(m, upper, preferred_element_type=f32)
        prev = jnp.dot(lower, m, preferred_element_type=f32)
        offs = jnp.dot(prev.astype(bf16), ones, preferred_element_type=f32)
        return within + offs

    for e in range(ne):
        a_e = a[e]
        thr_e = thr[e]
        gt_e = a_e > thr_e
        eq_e = (a_e == thr_e).astype(f32)
        need = cap - jnp.sum(gt_e.astype(f32), keepdims=True)
        rank_eq = cumsum(eq_e.astype(bf16)) - eq_e
        sel = jnp.logical_or(gt_e, jnp.logical_and(eq_e > 0.0, rank_eq < need))
        sel_b = sel.astype(bf16)
        c_incl = cumsum(sel_b)
        cs_ref[e] = jnp.where(sel, c_incl, 0.0).astype(i32)
        per_chunk = lax.dot_general(ones8, sel_b, NT_DIMS, preferred_element_type=f32)
        cbk = jnp.dot(per_chunk.astype(bf16), before_blk, preferred_element_type=f32)
        cbk_ref[e:e + 1, :] = cbk[0:1].astype(i32)


def _route(aff3, cap, tbc):
    e, nc, _ = aff3.shape
    kern = functools.partial(_route_kernel, cap=cap, tbc=tbc)
    return pl.pallas_call(
        kern,
        out_shape=[jax.ShapeDtypeStruct((e, nc, LANES), i32),
                   jax.ShapeDtypeStruct((e, LANES), i32)],
        compiler_params=pltpu.CompilerParams(vmem_limit_bytes=VMEM_LIMIT),
        name="route",
    )(aff3)


SC_CORES = 2
SC_SUBCORES = 16
SC_LANES = 16


SC_CHUNK = 32


def _sc_dispatch(cs_flat, aff_flat, h_tiles, ne, cap):
    n = cs_flat.shape[0] // ne
    workers = SC_CORES * SC_SUBCORES
    per_e = workers // ne
    per_w = cap // per_e
    assert per_e * ne == workers and per_w * per_e == cap
    assert n % SC_LANES == 0 and per_w % SC_CHUNK == 0
    table = h_tiles
    d_model = table.shape[1]
    mesh = plsc.VectorSubcoreMesh(core_axis_name="c", subcore_axis_name="s",
                                  num_cores=SC_CORES, num_subcores=SC_SUBCORES)

    @functools.partial(
        pl.kernel, mesh=mesh,
        out_type=[jax.ShapeDtypeStruct((ne * cap, d_model), f32),
                  jax.ShapeDtypeStruct((ne * cap,), f32)],
        scratch_types=[pltpu.VMEM((n,), i32), pltpu.VMEM((n,), f32),
                       pltpu.VMEM((per_w,), i32), pltpu.VMEM((per_w,), f32),
                       pltpu.VMEM((SC_CHUNK, d_model), f32), pltpu.SemaphoreType.DMA],
        compiler_params=pltpu.CompilerParams(needs_layout_passes=False, use_tc_tiling_on_sc=True),
        name="sc_dispatch")
    def dispatch(cs_hbm, aff_hbm, table_hbm, xe_hbm, g_hbm, cs_v, aff_v, idx_v, g_v, rows_v, sem):
        wid = lax.axis_index("s") * SC_CORES + lax.axis_index("c")
        e = wid // per_e
        lo = (wid - e * per_e) * per_w
        base = e * cap + lo
        pltpu.sync_copy(cs_hbm.at[pl.ds(e * n, n)], cs_v)
        pltpu.sync_copy(aff_hbm.at[pl.ds(e * n, n)], aff_v)
        lane = lax.iota(i32, SC_LANES)

        @pl.loop(0, n // SC_LANES)
        def _(i):
            off = pl.multiple_of(i * SC_LANES, SC_LANES)
            local = cs_v[pl.ds(off, SC_LANES)] - 1 - lo
            mine = jnp.logical_and(local >= 0, local < per_w)
            plsc.store_scatter(idx_v, [local], off + lane, mask=mine)
            plsc.store_scatter(g_v, [local], aff_v[pl.ds(off, SC_LANES)], mask=mine)

        pltpu.sync_copy(g_v, g_hbm.at[pl.ds(base, per_w)])

        @pl.loop(0, per_w // SC_CHUNK)
        def _(c):
            off = pl.multiple_of(c * SC_CHUNK, SC_CHUNK)
            pltpu.async_copy(table_hbm.at[idx_v.at[pl.ds(off, SC_CHUNK)]], rows_v, sem).wait()
            pltpu.sync_copy(rows_v, xe_hbm.at[pl.ds(base + off, SC_CHUNK)])

    return dispatch(cs_flat, aff_flat, table)


def _expert_kernel(x_ref, g_ref, wg_ref, wu_ref, wd_ref, y_ref, xb_ref, acc_ref, gcol_ref):
    f = pl.program_id(2)

    @pl.when(f == 0)
    def _():
        xb_ref[...] = x_ref[...].astype(bf16)
        acc_ref[...] = jnp.zeros_like(acc_ref)
        gcol_ref[...] = jnp.broadcast_to(g_ref[...], (LANES, g_ref.shape[-1])).T

    xb = xb_ref[...]
    hg = jnp.dot(xb, wg_ref[...].astype(bf16), preferred_element_type=f32)
    hu = jnp.dot(xb, wu_ref[...].astype(bf16), preferred_element_type=f32)
    hid = (hg * jax.nn.sigmoid(hg) * hu).astype(bf16)
    acc_ref[...] += jnp.dot(hid, wd_ref[...].astype(bf16), preferred_element_type=f32)

    @pl.when(f == pl.num_programs(2) - 1)
    def _():
        gcol = gcol_ref[...]
        d = acc_ref.shape[-1]
        for c in range(d // LANES):
            cols = slice(c * LANES, (c + 1) * LANES)
            y_ref[:, cols] = (acc_ref[:, cols] * gcol).astype(bf16)


def _experts(xe, g_rows, wg_b, wu_b, wd_b, tm, fc):
    e, d, fdim = wg_b.shape
    rows = xe.shape[0]
    per_e = rows // e
    nt = per_e // tm
    return pl.pallas_call(
        _expert_kernel,
        grid=(e, nt, fdim // fc),
        in_specs=[pl.BlockSpec((tm, d), lambda ei, t, f: (ei * nt + t, 0)),
                  pl.BlockSpec((None, 1, tm), lambda ei, t, f: (ei * nt + t, 0, 0)),
                  pl.BlockSpec((None, d, fc), lambda ei, t, f: (ei, 0, f)),
                  pl.BlockSpec((None, d, fc), lambda ei, t, f: (ei, 0, f)),
                  pl.BlockSpec((None, fc, d), lambda ei, t, f: (ei, f, 0))],
        out_specs=pl.BlockSpec((tm, d), lambda ei, t, f: (ei * nt + t, 0)),
        out_shape=jax.ShapeDtypeStruct((rows, d), bf16),
        scratch_shapes=[pltpu.VMEM((tm, d), bf16), pltpu.VMEM((tm, d), f32),
                        pltpu.VMEM((tm, LANES), f32)],
        compiler_params=_params("arbitrary", "arbitrary", "arbitrary"),
        name="experts",
    )(xe, g_rows.reshape(rows // tm, 1, tm), wg_b, wu_b, wd_b)


def _combine_kernel(cbk_ref, cs_ref, x1_ref, mod_ref, ye_ref, o_ref, win_ref, xwin_ref, sem, xsem,
                    acc_ref, *, cap, tbc, win, ne, total_rows):
    blk = pl.program_id(0)
    nblk = pl.num_programs(0)
    d = acc_ref.shape[-1]
    slot = blk % 2

    def region(ee):
        return ee * cap

    def first_row(bb, ee):
        st = cbk_ref[ee * LANES + bb]
        row = region(ee) + jnp.left_shift(jnp.right_shift(st, 4), 4)
        return pl.multiple_of(jnp.minimum(row, total_rows - win), BF16_ROWS)

    def start_block(bb, sl):
        for ee in range(ne):
            pltpu.make_async_copy(ye_ref.at[pl.ds(first_row(bb, ee), win)],
                                  win_ref.at[sl, pl.ds(ee * win, win)], sem.at[sl]).start()

    @pl.when(blk == 0)
    def _():
        start_block(blk, slot)

    @pl.when(blk + 1 < nblk)
    def _():
        start_block(blk + 1, (blk + 1) % 2)

    pltpu.make_async_copy(ye_ref.at[pl.ds(0, ne * win)], win_ref.at[slot], sem.at[slot]).wait()

    w_iota = lax.broadcasted_iota(i32, (win, tbc), 0)

    def onehot_of(cs_row, base_slot, done_slots):
        hit = jnp.logical_and(cs_row == base_slot + 1 + w_iota, cs_row > done_slots)
        return hit.astype(f32).T.astype(bf16)

    onehot = jnp.concatenate(
        [onehot_of(cs_ref[ee:ee + 1, :], first_row(blk, ee) - region(ee), 0) for ee in range(ne)],
        axis=1)
    acc_ref[...] = jnp.dot(onehot, win_ref[slot], preferred_element_type=f32)

    def extra_expert(ee, carry):
        row0 = first_row(blk, ee)
        base0 = row0 - region(ee)
        end = cbk_ref[ee * LANES + blk + 1]
        cs_row = cs_ref[pl.ds(ee, 1), :]

        def extra(w, c2):
            row_w = pl.multiple_of(jnp.minimum(row0 + w * win, total_rows - win), BF16_ROWS)
            cp = pltpu.make_async_copy(ye_ref.at[pl.ds(row_w, win)], xwin_ref, xsem)
            cp.start()
            cp.wait()
            oh = onehot_of(cs_row, row_w - region(ee), base0 + w * win)
            acc_ref[...] += jnp.dot(oh, xwin_ref[...], preferred_element_type=f32)
            return c2

        lax.fori_loop(1, (end - base0 + win - 1) // win, extra, 0)
        return carry

    lax.fori_loop(0, ne, extra_expert, 0)

    gate2 = mod_ref[:, 5 * d:6 * d]
    o_ref[...] = x1_ref[...] + gate2 * acc_ref[...]


def _combine(cbk, cs, x1, mod, ye, cap, tbc):
    b, l, d = x1.shape
    n = b * l
    e = cs.shape[0]
    nb = n // tbc
    win = min(128, cap)
    per_b = l // tbc
    kern = functools.partial(_combine_kernel, cap=cap, tbc=tbc, win=win, ne=e,
                             total_rows=ye.shape[0])
    return pl.pallas_call(
        kern,
        grid_spec=pltpu.PrefetchScalarGridSpec(
            num_scalar_prefetch=1, grid=(nb,),
            in_specs=[pl.BlockSpec((e, tbc), lambda bk, c: (0, bk)),
                      pl.BlockSpec((tbc, d), lambda bk, c: (bk, 0)),
                      pl.BlockSpec((None, 1, mod.shape[-1]), lambda bk, c: (bk // per_b, 0, 0)),
                      pl.BlockSpec(memory_space=pl.ANY)],
            out_specs=pl.BlockSpec((tbc, d), lambda bk, c: (bk, 0)),
            scratch_shapes=[pltpu.VMEM((2, e * win, d), bf16), pltpu.VMEM((win, d), bf16),
                            pltpu.SemaphoreType.DMA((2,)), pltpu.SemaphoreType.DMA,
                            pltpu.VMEM((tbc, d), f32)]),
        out_shape=jax.ShapeDtypeStruct((n, d), f32),
        compiler_params=_params("arbitrary"),
        name="combine",
    )(cbk.reshape(-1), cs, x1.reshape(n, d), mod, ye).reshape(b, l, d)


TBF = 512
TM_EXPERT = 2048
FC_EXPERT = 256


def kernel(x_prompt, x_sample, c_prompt, c_sample, ada_w, ada_b, norm1_g, w_in, q_norm_g, k_norm_g, rpb,
           conv_w, conv_b, conv_ln_g, conv_ln_b, w_out, norm2_g, w_router, w_gate, w_up, w_down):
    xs = (x_prompt, x_sample)
    d = x_prompt.shape[-1]
    ne = w_router.shape[1]
    row = lambda a: a.reshape(1, -1).astype(f32)

    nb0, nb1 = c_prompt.shape[0], c_sample.shape[0]
    pad = (-(nb0 + nb1)) % 8
    c_all = jnp.concatenate([c_prompt, c_sample, jnp.zeros((pad, d), f32)], axis=0)
    mod_all = _ada(c_all, ada_w, ada_b)
    mods = (mod_all[:nb0].reshape(nb0, 1, -1), mod_all[nb0:nb0 + nb1].reshape(nb1, 1, -1))

    seg_i = jnp.arange(ATTN_W) // HEAD_DIM
    seg = (seg_i[:, None] == seg_i[None, :]).astype(bf16)
    gq_t = jnp.tile(q_norm_g.astype(f32), N_HEADS).reshape(1, ATTN_W)
    gk_t = jnp.tile(k_norm_g.astype(f32), N_HEADS).reshape(1, ATTN_W)
    bias_tab = _bias_table(rpb)
    w_in_b = w_in.astype(bf16)
    w_out_b = w_out.astype(bf16)

    staged = []
    for x, mod in zip(xs, mods):
        b, l, _ = x.shape
        n = b * l
        cap = max(1, EC_FACTOR * n // ne)
        q, k, v, u = _inproj(x, mod, row(norm1_g), w_in_b, seg, gq_t, gk_t)
        attn = _attention(q, k, v, bias_tab)
        conv = _conv(u, conv_w.astype(f32), row(conv_b), row(conv_ln_g), row(conv_ln_b), seg)
        x1, h2, aff_t = _outproj(attn, conv, x, mod, w_out_b, row(norm2_g), w_router)
        aff3 = aff_t.reshape(ne, n // LANES, LANES)
        cs, cbk = _route(aff3, cap, TBF)
        xe, g = _sc_dispatch(cs.reshape(-1), aff3.reshape(-1), h2, ne, cap)
        staged.append((x1, mod, xe, g.reshape(ne, cap), cs.reshape(ne, n), cbk, cap))

    outs = []
    for x1, mod, xe, g, cs, cbk, cap in staged:
        ye = _experts(xe, g, w_gate, w_up, w_down, min(TM_EXPERT, cap), min(FC_EXPERT, w_gate.shape[2]))
        outs.append(_combine(cbk, cs, x1, mod, ye, cap, TBF))
    return tuple(outs)
```

```python
import functools

import jax
import jax.numpy as jnp
import numpy as np
from jax import lax
from jax.experimental import pallas as pl
from jax.experimental.pallas import tpu as pltpu
from jax.experimental.pallas import tpu_sc as plsc

f32 = jnp.float32
bf16 = jnp.bfloat16
i32 = jnp.int32

GRID_W = 64
N_HEADS = 8
HEAD_DIM = 64
ATTN_W = N_HEADS * HEAD_DIM
CONV_W = 512
CONV_K = 31
CONV_GROUP = 64
WIN_ROWS = 8
WIN_COLS = 16
EC_FACTOR = 2
EPS = 1e-6
NEG = -1e30

LANES = 128
BF16_ROWS = 16
VMEM_LIMIT = 56 * 1024 * 1024

NT_DIMS = (((1,), (1,)), ((), ()))


def _params(*sem):
    return pltpu.CompilerParams(dimension_semantics=sem, vmem_limit_bytes=VMEM_LIMIT)


def _split_bf16(x):
    hi = x.astype(bf16)
    lo = (x - hi.astype(f32)).astype(bf16)
    return hi, lo


def _ada_kernel(c_ref, w_ref, b_ref, o_ref):
    c = c_ref[...]
    s = c * jax.nn.sigmoid(c)
    s_hi, s_lo = _split_bf16(s)
    w_hi, w_lo = _split_bf16(w_ref[...])
    acc = jnp.dot(s_hi, w_hi, preferred_element_type=f32)
    acc += jnp.dot(s_hi, w_lo, preferred_element_type=f32)
    acc += jnp.dot(s_lo, w_hi, preferred_element_type=f32)
    o_ref[...] = acc + b_ref[...]


def _ada(c_all, ada_w, ada_b):
    rows, d = c_all.shape
    ncols = ada_w.shape[1]
    tn = next(t for t in (1024, 512, 256, 128) if ncols % t == 0)
    return pl.pallas_call(
        _ada_kernel,
        grid=(ncols // tn,),
        in_specs=[pl.BlockSpec((rows, d), lambda j: (0, 0)),
                  pl.BlockSpec((d, tn), lambda j: (0, j)),
                  pl.BlockSpec((1, tn), lambda j: (0, j))],
        out_specs=pl.BlockSpec((rows, tn), lambda j: (0, j)),
        out_shape=jax.ShapeDtypeStruct((rows, ncols), f32),
        compiler_params=_params("arbitrary"),
        name="ada",
    )(c_all, ada_w, ada_b.reshape(1, ncols))


def _inproj_kernel(x_ref, mod_ref, g1_ref, w_ref, seg_ref, gq_ref, gk_ref,
                   q_ref, k_ref, v_ref, u_ref, *, nparts):
    tm, d = x_ref.shape
    shift = mod_ref[:, 0:d]
    scale = mod_ref[:, d:2 * d]
    seg = seg_ref[...]
    qs = gq_ref[...] * gk_ref[...] * (HEAD_DIM ** -0.5)
    parts = [slice(p * (tm // nparts), (p + 1) * (tm // nparts)) for p in range(nparts)]
    hs = []
    for rows in parts:
        x = x_ref[rows, :]
        ms = jnp.mean(x * x, axis=-1, keepdims=True)
        h = x * lax.rsqrt(ms + EPS) * g1_ref[...]
        hs.append((h * (1.0 + scale) + shift).astype(bf16))
    zs = [jnp.dot(h, w_ref[...], preferred_element_type=f32) for h in hs]
    for rows, z in zip(parts, zs):
        zq = z[:, 0:ATTN_W]
        zk = z[:, ATTN_W:2 * ATTN_W]
        ssq_q = jnp.dot((zq * zq).astype(bf16), seg, preferred_element_type=f32)
        ssq_k = jnp.dot((zk * zk).astype(bf16), seg, preferred_element_type=f32)
        q_ref[rows, :] = (zq * lax.rsqrt(ssq_q * (1.0 / HEAD_DIM) + EPS) * qs).astype(bf16)
        k_ref[rows, :] = (zk * lax.rsqrt(ssq_k * (1.0 / HEAD_DIM) + EPS)).astype(bf16)
        v_ref[rows, :] = z[:, 2 * ATTN_W:3 * ATTN_W].astype(bf16)
        a = z[:, 3 * ATTN_W:3 * ATTN_W + CONV_W]
        gt = z[:, 3 * ATTN_W + CONV_W:]
        u_ref[rows, :] = a * jax.nn.sigmoid(gt)


def _inproj(x, mod, norm1_g, w_in_b, seg, gq_t, gk_t):
    b, l, d = x.shape
    tm = min(1024, l)
    ncols = w_in_b.shape[1]
    tok = lambda w: pl.BlockSpec((None, tm, w), lambda bi, i: (bi, i, 0))
    full = lambda a: pl.BlockSpec(a.shape, lambda bi, i: (0,) * a.ndim)
    return pl.pallas_call(
        functools.partial(_inproj_kernel, nparts=2),
        grid=(b, l // tm),
        in_specs=[tok(d),
                  pl.BlockSpec((None, 1, mod.shape[-1]), lambda bi, i: (bi, 0, 0)),
                  full(norm1_g), full(w_in_b), full(seg), full(gq_t), full(gk_t)],
        out_specs=[tok(ATTN_W), tok(ATTN_W), tok(ATTN_W), tok(CONV_W)],
        out_shape=[jax.ShapeDtypeStruct((b, l, ATTN_W), bf16)] * 3
                  + [jax.ShapeDtypeStruct((b, l, CONV_W), f32)],
        compiler_params=_params("arbitrary", "arbitrary"),
        name="inproj",
    )(x, mod, norm1_g, w_in_b, seg, gq_t, gk_t)


def _attn_kernel(q_ref, k_ref, v_ref, bias_ref, o_ref, *, rows, rg, wr, nr):
    g = pl.program_id(1)
    lane = lax.broadcasted_iota(i32, (GRID_W, LANES), 1)
    first_head = lane < HEAD_DIM
    nkeys = wr * GRID_W

    pairs = [(rr, p) for rr in range(nr) for p in range(N_HEADS // 2)]

    def rows_body(it, carry):
        geo = []
        for rr in range(nr):
            lr = it * nr + rr
            r = g * rg + lr
            rs = jnp.clip(r - wr // 2, 0, rows - wr)
            geo.append((pl.multiple_of(lr * GRID_W, GRID_W),
                        pl.multiple_of(rs * GRID_W, GRID_W),
                        rs - r + (WIN_ROWS - 1)))
        scores = []
        for rr, p in pairs:
            q0, tok0, rb = geo[rr]
            cols = slice(p * LANES, (p + 1) * LANES)
            qp = q_ref[pl.ds(q0, GRID_W), cols]
            kp = k_ref[pl.ds(tok0, nkeys), cols]
            zero = jnp.zeros_like(qp)
            q2 = jnp.concatenate([jnp.where(first_head, qp, zero),
                                  jnp.where(first_head, zero, qp)], axis=0)
            s = lax.dot_general(q2, kp, NT_DIMS, preferred_element_type=f32)
            bias = jnp.concatenate(
                [jnp.concatenate([bias_ref[2 * p + j, rb + 2 * m] for m in range(nkeys // LANES)],
                                 axis=-1) for j in range(2)], axis=0)
            scores.append(s + bias)
        probs = []
        for s in scores:
            mx = jnp.max(s, axis=-1, keepdims=True)
            e = jnp.exp(s - mx)
            probs.append((e.astype(bf16), jnp.sum(e, axis=-1, keepdims=True)))
        for (rr, p), (e, den) in zip(pairs, probs):
            q0, tok0, rb = geo[rr]
            cols = slice(p * LANES, (p + 1) * LANES)
            vp = v_ref[pl.ds(tok0, nkeys), cols]
            o = jnp.dot(e, vp, preferred_element_type=f32) / den
            o_ref[pl.ds(q0, GRID_W), cols] = jnp.where(
                first_head, o[0:GRID_W], o[GRID_W:]).astype(bf16)
        return carry

    lax.fori_loop(0, rg // nr, rows_body, 0)


def _attention(q, k, v, bias_tab):
    b, l, _ = q.shape
    rows = l // GRID_W
    wr = min(WIN_ROWS, rows)
    rg = min(16, rows)
    once = pl.Buffered(1)
    kern = functools.partial(_attn_kernel, rows=rows, rg=rg, wr=wr, nr=4)
    return pl.pallas_call(
        kern,
        grid=(b, rows // rg),
        in_specs=[pl.BlockSpec((None, rg * GRID_W, ATTN_W), lambda bi, g: (bi, g, 0)),
                  pl.BlockSpec((None, l, ATTN_W), lambda bi, g: (bi, 0, 0), pipeline_mode=once),
                  pl.BlockSpec((None, l, ATTN_W), lambda bi, g: (bi, 0, 0), pipeline_mode=once),
                  pl.BlockSpec(bias_tab.shape, lambda bi, g: (0, 0, 0, 0), pipeline_mode=once)],
        out_specs=pl.BlockSpec((None, rg * GRID_W, ATTN_W), lambda bi, g: (bi, g, 0)),
        out_shape=jax.ShapeDtypeStruct((b, l, ATTN_W), bf16),
        compiler_params=_params("arbitrary", "arbitrary"),
        name="attention",
    )(q, k, v, bias_tab)


def _bias_table(rpb):
    cols = np.arange(GRID_W)
    col_start = np.clip(cols - WIN_COLS // 2, 0, GRID_W - WIN_COLS)
    kc = np.arange(GRID_W)[None, :]
    valid = (kc >= col_start[:, None]) & (kc < col_start[:, None] + WIN_COLS)
    off = kc - cols[:, None] + (WIN_COLS - 1)
    pick = (off[None] == np.arange(2 * WIN_COLS - 1)[:, None, None]) & valid[None]
    tab = jnp.einsum('hrk,kcd->hrcd', rpb.astype(f32), jnp.asarray(pick, f32),
                     precision=lax.Precision.HIGHEST)
    tab = jnp.where(jnp.asarray(valid)[None, None], tab, NEG)
    return jnp.concatenate([tab[:, :-1], tab[:, 1:]], axis=-1)


def _conv_kernel(u_ref, prev_ref, next_ref, w_ref, cb_ref, lg_ref, lb_ref, seg_ref, o_ref,
                 win_ref, sh_ref, *, tl, chunk):
    i = pl.program_id(1)
    n_i = pl.num_programs(1)
    halo = prev_ref.shape[0]
    sub = 8
    win_ref[0:halo, :] = jnp.where(i > 0, prev_ref[...], 0.0)
    win_ref[halo:halo + tl, :] = u_ref[...]
    win_ref[halo + tl:, :] = jnp.where(i < n_i - 1, next_ref[...], 0.0)
    sh_rows = sh_ref.shape[1]
    for r in range(1, sub):
        for k0 in range(0, sh_rows, chunk):
            k1 = min(k0 + chunk, sh_rows)
            sh_ref[r - 1, k0:k1, :] = win_ref[r + k0:r + k1, :]
    seg = seg_ref[...]
    pad = CONV_K // 2
    for c in range(tl // chunk):
        acc = jnp.zeros((chunk, CONV_W), f32)
        for j in range(CONV_K):
            start = halo - pad + c * chunk + j
            r = start % sub
            src = win_ref if r == 0 else sh_ref.at[r - 1]
            acc = acc + src[start - r:start - r + chunk, :] * w_ref[j:j + 1, :]
        y = acc + cb_ref[...]
        y_hi, y_lo = _split_bf16(y)
        mu = (jnp.dot(y_hi, seg, preferred_element_type=f32)
              + jnp.dot(y_lo, seg, preferred_element_type=f32)) * (1.0 / CONV_GROUP)
        dlt = y - mu
        var = jnp.dot((dlt * dlt).astype(bf16), seg, preferred_element_type=f32) * (1.0 / CONV_GROUP)
        un = dlt * lax.rsqrt(var + EPS) * lg_ref[...] + lb_ref[...]
        o_ref[c * chunk:(c + 1) * chunk, :] = (un * jax.nn.sigmoid(un)).astype(bf16)


def _conv(u, conv_w, conv_b, ln_g, ln_b, seg):
    b, l, _ = u.shape
    tl = min(1024, l)
    halo = 16
    nh = l // halo
    kern = functools.partial(_conv_kernel, tl=tl, chunk=min(128, tl))
    full = lambda a: pl.BlockSpec(a.shape, lambda bi, i: (0,) * a.ndim)
    return pl.pallas_call(
        kern,
        grid=(b, l // tl),
        in_specs=[pl.BlockSpec((None, tl, CONV_W), lambda bi, i: (bi, i, 0)),
                  pl.BlockSpec((None, halo, CONV_W),
                               lambda bi, i: (bi, jnp.maximum(i * (tl // halo) - 1, 0), 0)),
                  pl.BlockSpec((None, halo, CONV_W),
                               lambda bi, i: (bi, jnp.minimum((i + 1) * (tl // halo), nh - 1), 0)),
                  full(conv_w), full(conv_b), full(ln_g), full(ln_b), full(seg)],
        out_specs=pl.BlockSpec((None, tl, CONV_W), lambda bi, i: (bi, i, 0)),
        out_shape=jax.ShapeDtypeStruct((b, l, CONV_W), bf16),
        scratch_shapes=[pltpu.VMEM((tl + 2 * halo, CONV_W), f32),
                        pltpu.VMEM((7, tl + 2 * halo - 8, CONV_W), f32)],
        compiler_params=_params("arbitrary", "arbitrary"),
        name="conv",
    )(u, u, u, conv_w, conv_b, ln_g, ln_b, seg)


def _outproj_kernel(at_ref, cv_ref, x_ref, mod_ref, wo_ref, g2_ref, wr_ref,
                    x1_ref, h2_ref, aff_ref, *, nparts):
    tm, d = x_ref.shape
    gate1 = mod_ref[:, 2 * d:3 * d]
    shift2 = mod_ref[:, 3 * d:4 * d]
    scale2 = mod_ref[:, 4 * d:5 * d]
    w_hi, w_lo = _split_bf16(wr_ref[...])
    parts = [slice(p * (tm // nparts), (p + 1) * (tm // nparts)) for p in range(nparts)]
    mixes = [jnp.dot(at_ref[rows, :], wo_ref[0:ATTN_W, :], preferred_element_type=f32)
             + jnp.dot(cv_ref[rows, :], wo_ref[ATTN_W:, :], preferred_element_type=f32)
             for rows in parts]
    splits = []
    for rows, mix in zip(parts, mixes):
        x1 = x_ref[rows, :] + gate1 * mix
        x1_ref[rows, :] = x1
        ms = jnp.mean(x1 * x1, axis=-1, keepdims=True)
        h2 = x1 * lax.rsqrt(ms + EPS) * g2_ref[...]
        h2 = h2 * (1.0 + scale2) + shift2
        h_hi, h_lo = _split_bf16(h2)
        h2_ref[rows, :] = h_hi.astype(f32)
        splits.append((h_hi, h_lo))
    lgs = [jnp.dot(h_hi, w_hi, preferred_element_type=f32)
           + jnp.dot(h_lo, w_hi, preferred_element_type=f32)
           + jnp.dot(h_hi, w_lo, preferred_element_type=f32) for h_hi, h_lo in splits]
    for rows, lg in zip(parts, lgs):
        logits = lg.T[0:aff_ref.shape[0], :]
        mx = jnp.max(logits, axis=0, keepdims=True)
        ex = jnp.exp(logits - mx)
        aff_ref[:, rows] = ex / jnp.sum(ex, axis=0, keepdims=True)


def _outproj(attn, conv, x, mod, w_out_b, norm2_g, w_router):
    b, l, d = x.shape
    tm = min(1024, l)
    nt = l // tm
    e = w_router.shape[1]
    w_router_t = jnp.pad(w_router.astype(f32), ((0, 0), (0, LANES - e)))
    tok = lambda w: pl.BlockSpec((None, tm, w), lambda bi, i: (bi, i, 0))
    full = lambda a: pl.BlockSpec(a.shape, lambda bi, i: (0,) * a.ndim)
    return pl.pallas_call(
        functools.partial(_outproj_kernel, nparts=4),
        grid=(b, nt),
        in_specs=[tok(ATTN_W), tok(CONV_W), tok(d),
                  pl.BlockSpec((None, 1, mod.shape[-1]), lambda bi, i: (bi, 0, 0)),
                  full(w_out_b), full(norm2_g), full(w_router_t)],
        out_specs=[tok(d),
                   pl.BlockSpec((tm, d), lambda bi, i: (bi * nt + i, 0)),
                   pl.BlockSpec((e, tm), lambda bi, i: (0, bi * nt + i))],
        out_shape=[jax.ShapeDtypeStruct((b, l, d), f32),
                   jax.ShapeDtypeStruct((b * l, d), f32),
                   jax.ShapeDtypeStruct((e, b * l), f32)],
        compiler_params=_params("arbitrary", "arbitrary"),
        name="outproj",
    )(attn, conv, x, mod, w_out_b, norm2_g, w_router_t)


def _route_kernel(aff_ref, cs_ref, cbk_ref, *, cap, tbc):
    a = aff_ref[...]
    ne, nc, _ = a.shape

    def enough(v):
        return jnp.sum((a >= v).astype(f32), axis=(1, 2), keepdims=True) >= cap

    def pow2(k):
        return lax.bitcast_convert_type(jnp.left_shift(k + 127, 23), f32)

    def exp_body(_, st):
        klo, khi = st
        kmid = jnp.right_shift(klo + khi, 1)
        ok = enough(pow2(kmid))
        return jnp.where(ok, kmid, klo), jnp.where(ok, khi, kmid)

    klo, khi = lax.fori_loop(0, 7, exp_body,
                             (jnp.full((ne, 1, 1), -127, i32), jnp.full((ne, 1, 1), 1, i32)))

    def lin_body(_, st):
        lo, hi = st
        mid = lo + (hi - lo) * 0.5
        ok = enough(mid)
        return jnp.where(ok, mid, lo), jnp.where(ok, hi, mid)

    lo, _ = lax.fori_loop(0, 48, lin_body, (pow2(klo), pow2(khi)))
    thr = jnp.min(jnp.where(a >= lo, a, jnp.inf), axis=(1, 2), keepdims=True)

    r_i = lax.broadcasted_iota(i32, (LANES, LANES), 0)
    c_i = lax.broadcasted_iota(i32, (LANES, LANES), 1)
    upper = (r_i <= c_i).astype(bf16)
    ones = jnp.ones((LANES, LANES), bf16)
    rr = lax.broadcasted_iota(i32, (nc, nc), 0)
    cc = lax.broadcasted_iota(i32, (nc, nc), 1)
    lower = (cc < rr).astype(bf16)
    blk_r = lax.broadcasted_iota(i32, (nc, LANES), 0)
    blk_c = lax.broadcasted_iota(i32, (nc, LANES), 1)
    before_blk = (blk_r < blk_c * (tbc // LANES)).astype(bf16)
    ones8 = jnp.ones((8, LANES), bf16)

    def cumsum(m):
        within = jnp.dot(m, upper, preferred_element_type=f32)
        prev = jnp.dot(lower, m, preferred_element_type=f32)
        offs = jnp.dot(prev.astype(bf16), ones, preferred_element_type=f32)
        return within + offs

    for e in range(ne):
        a_e = a[e]
        thr_e = thr[e]
        gt_e = a_e > thr_e
        eq_e = (a_e == thr_e).astype(f32)
        need = cap - jnp.sum(gt_e.astype(f32), keepdims=True)
        rank_eq = cumsum(eq_e.astype(bf16)) - eq_e
        sel = jnp.logical_or(gt_e, jnp.logical_and(eq_e > 0.0, rank_eq < need))
        sel_b = sel.astype(bf16)
        c_incl = cumsum(sel_b)
        cs_ref[e] = jnp.where(sel, c_incl, 0.0).astype(i32)
        per_chunk = lax.dot_general(ones8, sel_b, NT_DIMS, preferred_element_type=f32)
        cbk = jnp.dot(per_chunk.astype(bf16), before_blk, preferred_element_type=f32)
        cbk_ref[e:e + 1, :] = cbk[0:1].astype(i32)


def _route(aff3, cap, tbc):
    e, nc, _ = aff3.shape
    kern = functools.partial(_route_kernel, cap=cap, tbc=tbc)
    return pl.pallas_call(
        kern,
        out_shape=[jax.ShapeDtypeStruct((e, nc, LANES), i32),
                   jax.ShapeDtypeStruct((e, LANES), i32)],
        compiler_params=pltpu.CompilerParams(vmem_limit_bytes=VMEM_LIMIT),
        name="route",
    )(aff3)


SC_CORES = 2
SC_SUBCORES = 16
SC_LANES = 16


SC_CHUNK = 32


def _sc_dispatch(cs_flat, aff_flat, h_tiles, ne, cap):
    n = cs_flat.shape[0] // ne
    workers = SC_CORES * SC_SUBCORES
    per_e = workers // ne
    per_w = cap // per_e
    assert per_e * ne == workers and per_w * per_e == cap
    assert n % SC_LANES == 0 and per_w % SC_CHUNK == 0
    table = h_tiles
    d_model = table.shape[1]
    mesh = plsc.VectorSubcoreMesh(core_axis_name="c", subcore_axis_name="s",
                                  num_cores=SC_CORES, num_subcores=SC_SUBCORES)

    @functools.partial(
        pl.kernel, mesh=mesh,
        out_type=[jax.ShapeDtypeStruct((ne * cap, d_model), f32),
                  jax.ShapeDtypeStruct((ne * cap,), f32)],
        scratch_types=[pltpu.VMEM((n,), i32), pltpu.VMEM((n,), f32),
                       pltpu.VMEM((per_w,), i32), pltpu.VMEM((per_w,), f32),
                       pltpu.VMEM((SC_CHUNK, d_model), f32), pltpu.SemaphoreType.DMA],
        compiler_params=pltpu.CompilerParams(needs_layout_passes=False, use_tc_tiling_on_sc=True),
        name="sc_dispatch")
    def dispatch(cs_hbm, aff_hbm, table_hbm, xe_hbm, g_hbm, cs_v, aff_v, idx_v, g_v, rows_v, sem):
        wid = lax.axis_index("s") * SC_CORES + lax.axis_index("c")
        e = wid // per_e
        lo = (wid - e * per_e) * per_w
        base = e * cap + lo
        pltpu.sync_copy(cs_hbm.at[pl.ds(e * n, n)], cs_v)
        pltpu.sync_copy(aff_hbm.at[pl.ds(e * n, n)], aff_v)
        lane = lax.iota(i32, SC_LANES)

        @pl.loop(0, n // SC_LANES)
        def _(i):
            off = pl.multiple_of(i * SC_LANES, SC_LANES)
            local = cs_v[pl.ds(off, SC_LANES)] - 1 - lo
            mine = jnp.logical_and(local >= 0, local < per_w)
            plsc.store_scatter(idx_v, [local], off + lane, mask=mine)
            plsc.store_scatter(g_v, [local], aff_v[pl.ds(off, SC_LANES)], mask=mine)

        pltpu.sync_copy(g_v, g_hbm.at[pl.ds(base, per_w)])

        @pl.loop(0, per_w // SC_CHUNK)
        def _(c):
            off = pl.multiple_of(c * SC_CHUNK, SC_CHUNK)
            pltpu.async_copy(table_hbm.at[idx_v.at[pl.ds(off, SC_CHUNK)]], rows_v, sem).wait()
            pltpu.sync_copy(rows_v, xe_hbm.at[pl.ds(base + off, SC_CHUNK)])

    return dispatch(cs_flat, aff_flat, table)


def _expert_kernel(x_ref, g_ref, wg_ref, wu_ref, wd_ref, y_ref, xb_ref, acc_ref, gcol_ref):
    f = pl.program_id(2)

    @pl.when(f == 0)
    def _():
        xb_ref[...] = x_ref[...].astype(bf16)
        acc_ref[...] = jnp.zeros_like(acc_ref)
        gcol_ref[...] = jnp.broadcast_to(g_ref[...], (LANES, g_ref.shape[-1])).T

    xb = xb_ref[...]
    hg = jnp.dot(xb, wg_ref[...].astype(bf16), preferred_element_type=f32)
    hu = jnp.dot(xb, wu_ref[...].astype(bf16), preferred_element_type=f32)
    hid = (hg * jax.nn.sigmoid(hg) * hu).astype(bf16)
    acc_ref[...] += jnp.dot(hid, wd_ref[...].astype(bf16), preferred_element_type=f32)

    @pl.when(f == pl.num_programs(2) - 1)
    def _():
        gcol = gcol_ref[...]
        d = acc_ref.shape[-1]
        for c in range(d // LANES):
            cols = slice(c * LANES, (c + 1) * LANES)
            y_ref[:, cols] = (acc_ref[:, cols] * gcol).astype(bf16)


def _experts(xe, g_rows, wg_b, wu_b, wd_b, tm, fc):
    e, d, fdim = wg_b.shape
    rows = xe.shape[0]
    per_e = rows // e
    nt = per_e // tm
    return pl.pallas_call(
        _expert_kernel,
        grid=(e, nt, fdim // fc),
        in_specs=[pl.BlockSpec((tm, d), lambda ei, t, f: (ei * nt + t, 0)),
                  pl.BlockSpec((None, 1, tm), lambda ei, t, f: (ei * nt + t, 0, 0)),
                  pl.BlockSpec((None, d, fc), lambda ei, t, f: (ei, 0, f)),
                  pl.BlockSpec((None, d, fc), lambda ei, t, f: (ei, 0, f)),
                  pl.BlockSpec((None, fc, d), lambda ei, t, f: (ei, f, 0))],
        out_specs=pl.BlockSpec((tm, d), lambda ei, t, f: (ei * nt + t, 0)),
        out_shape=jax.ShapeDtypeStruct((rows, d), bf16),
        scratch_shapes=[pltpu.VMEM((tm, d), bf16), pltpu.VMEM((tm, d), f32),
                        pltpu.VMEM((tm, LANES), f32)],
        compiler_params=_params("arbitrary", "arbitrary", "arbitrary"),
        name="experts",
    )(xe, g_rows.reshape(rows // tm, 1, tm), wg_b, wu_b, wd_b)


def _combine_kernel(cbk_ref, cs_ref, x1_ref, mod_ref, ye_ref, o_ref, win_ref, xwin_ref, sem, xsem,
                    acc_ref, *, cap, tbc, win, ne, total_rows):
    blk = pl.program_id(0)
    nblk = pl.num_programs(0)
    d = acc_ref.shape[-1]
    slot = blk % 2

    def region(ee):
        return ee * cap

    def first_row(bb, ee):
        st = cbk_ref[ee * LANES + bb]
        row = region(ee) + jnp.left_shift(jnp.right_shift(st, 4), 4)
        return pl.multiple_of(jnp.minimum(row, total_rows - win), BF16_ROWS)

    def start_block(bb, sl):
        for ee in range(ne):
            pltpu.make_async_copy(ye_ref.at[pl.ds(first_row(bb, ee), win)],
                                  win_ref.at[sl, pl.ds(ee * win, win)], sem.at[sl]).start()

    @pl.when(blk == 0)
    def _():
        start_block(blk, slot)

    @pl.when(blk + 1 < nblk)
    def _():
        start_block(blk + 1, (blk + 1) % 2)

    pltpu.make_async_copy(ye_ref.at[pl.ds(0, ne * win)], win_ref.at[slot], sem.at[slot]).wait()

    w_iota = lax.broadcasted_iota(i32, (win, tbc), 0)

    def onehot_of(cs_row, base_slot, done_slots):
        hit = jnp.logical_and(cs_row == base_slot + 1 + w_iota, cs_row > done_slots)
        return hit.astype(f32).T.astype(bf16)

    onehot = jnp.concatenate(
        [onehot_of(cs_ref[ee:ee + 1, :], first_row(blk, ee) - region(ee), 0) for ee in range(ne)],
        axis=1)
    acc_ref[...] = jnp.dot(onehot, win_ref[slot], preferred_element_type=f32)

    def extra_expert(ee, carry):
        row0 = first_row(blk, ee)
        base0 = row0 - region(ee)
        end = cbk_ref[ee * LANES + blk + 1]
        cs_row = cs_ref[pl.ds(ee, 1), :]

        def extra(w, c2):
            row_w = pl.multiple_of(jnp.minimum(row0 + w * win, total_rows - win), BF16_ROWS)
            cp = pltpu.make_async_copy(ye_ref.at[pl.ds(row_w, win)], xwin_ref, xsem)
            cp.start()
            cp.wait()
            oh = onehot_of(cs_row, row_w - region(ee), base0 + w * win)
            acc_ref[...] += jnp.dot(oh, xwin_ref[...], preferred_element_type=f32)
            return c2

        lax.fori_loop(1, (end - base0 + win - 1) // win, extra, 0)
        return carry

    lax.fori_loop(0, ne, extra_expert, 0)

    gate2 = mod_ref[:, 5 * d:6 * d]
    o_ref[...] = x1_ref[...] + gate2 * acc_ref[...]


def _combine(cbk, cs, x1, mod, ye, cap, tbc):
    b, l, d = x1.shape
    n = b * l
    e = cs.shape[0]
    nb = n // tbc
    win = min(128, cap)
    per_b = l // tbc
    kern = functools.partial(_combine_kernel, cap=cap, tbc=tbc, win=win, ne=e,
                             total_rows=ye.shape[0])
    return pl.pallas_call(
        kern,
        grid_spec=pltpu.PrefetchScalarGridSpec(
            num_scalar_prefetch=1, grid=(nb,),
            in_specs=[pl.BlockSpec((e, tbc), lambda bk, c: (0, bk)),
                      pl.BlockSpec((tbc, d), lambda bk, c: (bk, 0)),
                      pl.BlockSpec((None, 1, mod.shape[-1]), lambda bk, c: (bk // per_b, 0, 0)),
                      pl.BlockSpec(memory_space=pl.ANY)],
            out_specs=pl.BlockSpec((tbc, d), lambda bk, c: (bk, 0)),
            scratch_shapes=[pltpu.VMEM((2, e * win, d), bf16), pltpu.VMEM((win, d), bf16),
                            pltpu.SemaphoreType.DMA((2,)), pltpu.SemaphoreType.DMA,
                            pltpu.VMEM((tbc, d), f32)]),
        out_shape=jax.ShapeDtypeStruct((n, d), f32),
        compiler_params=_params("arbitrary"),
        name="combine",
    )(cbk.reshape(-1), cs, x1.reshape(n, d), mod, ye).reshape(b, l, d)


TBF = 512
TM_EXPERT = 2048
FC_EXPERT = 256


def kernel(x_prompt, x_sample, c_prompt, c_sample, ada_w, ada_b, norm1_g, w_in, q_norm_g, k_norm_g, rpb,
           conv_w, conv_b, conv_ln_g, conv_ln_b, w_out, norm2_g, w_router, w_gate, w_up, w_down):
    xs = (x_prompt, x_sample)
    d = x_prompt.shape[-1]
    ne = w_router.shape[1]
    row = lambda a: a.reshape(1, -1).astype(f32)

    nb0, nb1 = c_prompt.shape[0], c_sample.shape[0]
    pad = (-(nb0 + nb1)) % 8
    c_all = jnp.concatenate([c_prompt, c_sample, jnp.zeros((pad, d), f32)], axis=0)
    mod_all = _ada(c_all, ada_w, ada_b)
    mods = (mod_all[:nb0].reshape(nb0, 1, -1), mod_all[nb0:nb0 + nb1].reshape(nb1, 1, -1))

    seg_i = jnp.arange(ATTN_W) // HEAD_DIM
    seg = (seg_i[:, None] == seg_i[None, :]).astype(bf16)
    gq_t = jnp.tile(q_norm_g.astype(f32), N_HEADS).reshape(1, ATTN_W)
    gk_t = jnp.tile(k_norm_g.astype(f32), N_HEADS).reshape(1, ATTN_W)
    bias_tab = _bias_table(rpb)
    w_in_b = w_in.astype(bf16)
    w_out_b = w_out.astype(bf16)

    staged = []
    for x, mod in zip(xs, mods):
        b, l, _ = x.shape
        n = b * l
        cap = max(1, EC_FACTOR * n // ne)
        q, k, v, u = _inproj(x, mod, row(norm1_g), w_in_b, seg, gq_t, gk_t)
        attn = _attention(q, k, v, bias_tab)
        conv = _conv(u, conv_w.astype(f32), row(conv_b), row(conv_ln_g), row(conv_ln_b), seg)
        x1, h2, aff_t = _outproj(attn, conv, x, mod, w_out_b, row(norm2_g), w_router)
        aff3 = aff_t.reshape(ne, n // LANES, LANES)
        cs, cbk = _route(aff3, cap, TBF)
        xe, g = _sc_dispatch(cs.reshape(-1), aff3.reshape(-1), h2, ne, cap)
        staged.append((x1, mod, xe, g.reshape(ne, cap), cs.reshape(ne, n), cbk, cap))

    outs = []
    for x1, mod, xe, g, cs, cbk, cap in staged:
        ye = _experts(xe, g, w_gate, w_up, w_down, min(TM_EXPERT, cap), min(FC_EXPERT, w_gate.shape[2]))
        outs.append(_combine(cbk, cs, x1, mod, ye, cap, TBF))
    return tuple(outs)
```

```python
import functools

import jax
import jax.numpy as jnp
import numpy as np
from jax import lax
from jax.experimental import pallas as pl
from jax.experimental.pallas import tpu as pltpu
from jax.experimental.pallas import tpu_sc as plsc

f32 = jnp.float32
bf16 = jnp.bfloat16
i32 = jnp.int32

GRID_W = 64
N_HEADS = 8
HEAD_DIM = 64
ATTN_W = N_HEADS * HEAD_DIM
CONV_W = 512
CONV_K = 31
CONV_GROUP = 64
WIN_ROWS = 8
WIN_COLS = 16
EC_FACTOR = 2
EPS = 1e-6
NEG = -1e30

LANES = 128
BF16_ROWS = 16
VMEM_LIMIT = 56 * 1024 * 1024

NT_DIMS = (((1,), (1,)), ((), ()))


def _params(*sem):
    return pltpu.CompilerParams(dimension_semantics=sem, vmem_limit_bytes=VMEM_LIMIT)


def _split_bf16(x):
    hi = x.astype(bf16)
    lo = (x - hi.astype(f32)).astype(bf16)
    return hi, lo


def _ada_kernel(c_ref, w_ref, b_ref, o_ref):
    c = c_ref[...]
    s = c * jax.nn.sigmoid(c)
    s_hi, s_lo = _split_bf16(s)
    w_hi, w_lo = _split_bf16(w_ref[...])
    acc = jnp.dot(s_hi, w_hi, preferred_element_type=f32)
    acc += jnp.dot(s_hi, w_lo, preferred_element_type=f32)
    acc += jnp.dot(s_lo, w_hi, preferred_element_type=f32)
    o_ref[...] = acc + b_ref[...]


def _ada(c_all, ada_w, ada_b):
    rows, d = c_all.shape
    ncols = ada_w.shape[1]
    tn = next(t for t in (1024, 512, 256, 128) if ncols % t == 0)
    return pl.pallas_call(
        _ada_kernel,
        grid=(ncols // tn,),
        in_specs=[pl.BlockSpec((rows, d), lambda j: (0, 0)),
                  pl.BlockSpec((d, tn), lambda j: (0, j)),
                  pl.BlockSpec((1, tn), lambda j: (0, j))],
        out_specs=pl.BlockSpec((rows, tn), lambda j: (0, j)),
        out_shape=jax.ShapeDtypeStruct((rows, ncols), f32),
        compiler_params=_params("arbitrary"),
        name="ada",
    )(c_all, ada_w, ada_b.reshape(1, ncols))


def _inproj_kernel(x_ref, mod_ref, g1_ref, w_ref, seg_ref, gq_ref, gk_ref,
                   q_ref, k_ref, v_ref, u_ref, *, nparts):
    tm, d = x_ref.shape
    shift = mod_ref[:, 0:d]
    scale = mod_ref[:, d:2 * d]
    seg = seg_ref[...]
    qs = gq_ref[...] * gk_ref[...] * (HEAD_DIM ** -0.5)
    parts = [slice(p * (tm // nparts), (p + 1) * (tm // nparts)) for p in range(nparts)]
    hs = []
    for rows in parts:
        x = x_ref[rows, :]
        ms = jnp.mean(x * x, axis=-1, keepdims=True)
        h = x * lax.rsqrt(ms + EPS) * g1_ref[...]
        hs.append((h * (1.0 + scale) + shift).astype(bf16))
    zs = [jnp.dot(h, w_ref[...], preferred_element_type=f32) for h in hs]
    for rows, z in zip(parts, zs):
        zq = z[:, 0:ATTN_W]
        zk = z[:, ATTN_W:2 * ATTN_W]
        ssq_q = jnp.dot((zq * zq).astype(bf16), seg, preferred_element_type=f32)
        ssq_k = jnp.dot((zk * zk).astype(bf16), seg, preferred_element_type=f32)
        q_ref[rows, :] = (zq * lax.rsqrt(ssq_q * (1.0 / HEAD_DIM) + EPS) * qs).astype(bf16)
        k_ref[rows, :] = (zk * lax.rsqrt(ssq_k * (1.0 / HEAD_DIM) + EPS)).astype(bf16)
        v_ref[rows, :] = z[:, 2 * ATTN_W:3 * ATTN_W].astype(bf16)
        a = z[:, 3 * ATTN_W:3 * ATTN_W + CONV_W]
        gt = z[:, 3 * ATTN_W + CONV_W:]
        u_ref[rows, :] = a * jax.nn.sigmoid(gt)


def _inproj(x, mod, norm1_g, w_in_b, seg, gq_t, gk_t):
    b, l, d = x.shape
    tm = min(1024, l)
    ncols = w_in_b.shape[1]
    tok = lambda w: pl.BlockSpec((None, tm, w), lambda bi, i: (bi, i, 0))
    full = lambda a: pl.BlockSpec(a.shape, lambda bi, i: (0,) * a.ndim)
    return pl.pallas_call(
        functools.partial(_inproj_kernel, nparts=2),
        grid=(b, l // tm),
        in_specs=[tok(d),
                  pl.BlockSpec((None, 1, mod.shape[-1]), lambda bi, i: (bi, 0, 0)),
                  full(norm1_g), full(w_in_b), full(seg), full(gq_t), full(gk_t)],
        out_specs=[tok(ATTN_W), tok(ATTN_W), tok(ATTN_W), tok(CONV_W)],
        out_shape=[jax.ShapeDtypeStruct((b, l, ATTN_W), bf16)] * 3
                  + [jax.ShapeDtypeStruct((b, l, CONV_W), f32)],
        compiler_params=_params("arbitrary", "arbitrary"),
        name="inproj",
    )(x, mod, norm1_g, w_in_b, seg, gq_t, gk_t)


def _attn_kernel(q_ref, k_ref, v_ref, bias_ref, o_ref, *, rows, rg, wr, nr):
    g = pl.program_id(1)
    lane = lax.broadcasted_iota(i32, (GRID_W, LANES), 1)
    first_head = lane < HEAD_DIM
    nkeys = wr * GRID_W

    pairs = [(rr, p) for rr in range(nr) for p in range(N_HEADS // 2)]

    def rows_body(it, carry):
        geo = []
        for rr in range(nr):
            lr = it * nr + rr
            r = g * rg + lr
            rs = jnp.clip(r - wr // 2, 0, rows - wr)
            geo.append((pl.multiple_of(lr * GRID_W, GRID_W),
                        pl.multiple_of(rs * GRID_W, GRID_W),
                        rs - r + (WIN_ROWS - 1)))
        scores = []
        for rr, p in pairs:
            q0, tok0, rb = geo[rr]
            cols = slice(p * LANES, (p + 1) * LANES)
            qp = q_ref[pl.ds(q0, GRID_W), cols]
            kp = k_ref[pl.ds(tok0, nkeys), cols]
            zero = jnp.zeros_like(qp)
            q2 = jnp.concatenate([jnp.where(first_head, qp, zero),
                                  jnp.where(first_head, zero, qp)], axis=0)
            s = lax.dot_general(q2, kp, NT_DIMS, preferred_element_type=f32)
            bias = jnp.concatenate(
                [jnp.concatenate([bias_ref[2 * p + j, rb + 2 * m] for m in range(nkeys // LANES)],
                                 axis=-1) for j in range(2)], axis=0)
            scores.append(s + bias)
        probs = []
        for s in scores:
            mx = jnp.max(s, axis=-1, keepdims=True)
            e = jnp.exp(s - mx)
            probs.append((e.astype(bf16), jnp.sum(e, axis=-1, keepdims=True)))
        for (rr, p), (e, den) in zip(pairs, probs):
            q0, tok0, rb = geo[rr]
            cols = slice(p * LANES, (p + 1) * LANES)
            vp = v_ref[pl.ds(tok0, nkeys), cols]
            o = jnp.dot(e, vp, preferred_element_type=f32) / den
            o_ref[pl.ds(q0, GRID_W), cols] = jnp.where(
                first_head, o[0:GRID_W], o[GRID_W:]).astype(bf16)
        return carry

    lax.fori_loop(0, rg // nr, rows_body, 0)


def _attention(q, k, v, bias_tab):
    b, l, _ = q.shape
    rows = l // GRID_W
    wr = min(WIN_ROWS, rows)
    rg = min(16, rows)
    once = pl.Buffered(1)
    kern = functools.partial(_attn_kernel, rows=rows, rg=rg, wr=wr, nr=8)
    return pl.pallas_call(
        kern,
        grid=(b, rows // rg),
        in_specs=[pl.BlockSpec((None, rg * GRID_W, ATTN_W), lambda bi, g: (bi, g, 0)),
                  pl.BlockSpec((None, l, ATTN_W), lambda bi, g: (bi, 0, 0), pipeline_mode=once),
                  pl.BlockSpec((None, l, ATTN_W), lambda bi, g: (bi, 0, 0), pipeline_mode=once),
                  pl.BlockSpec(bias_tab.shape, lambda bi, g: (0, 0, 0, 0), pipeline_mode=once)],
        out_specs=pl.BlockSpec((None, rg * GRID_W, ATTN_W), lambda bi, g: (bi, g, 0)),
        out_shape=jax.ShapeDtypeStruct((b, l, ATTN_W), bf16),
        compiler_params=_params("arbitrary", "arbitrary"),
        name="attention",
    )(q, k, v, bias_tab)


def _bias_table(rpb):
    cols = np.arange(GRID_W)
    col_start = np.clip(cols - WIN_COLS // 2, 0, GRID_W - WIN_COLS)
    kc = np.arange(GRID_W)[None, :]
    valid = (kc >= col_start[:, None]) & (kc < col_start[:, None] + WIN_COLS)
    off = kc - cols[:, None] + (WIN_COLS - 1)
    pick = (off[None] == np.arange(2 * WIN_COLS - 1)[:, None, None]) & valid[None]
    tab = jnp.einsum('hrk,kcd->hrcd', rpb.astype(f32), jnp.asarray(pick, f32),
                     precision=lax.Precision.HIGHEST)
    tab = jnp.where(jnp.asarray(valid)[None, None], tab, NEG)
    return jnp.concatenate([tab[:, :-1], tab[:, 1:]], axis=-1)


def _conv_kernel(u_ref, prev_ref, next_ref, w_ref, cb_ref, lg_ref, lb_ref, seg_ref, o_ref,
                 win_ref, sh_ref, *, tl, chunk):
    i = pl.program_id(1)
    n_i = pl.num_programs(1)
    halo = prev_ref.shape[0]
    sub = 8
    win_ref[0:halo, :] = jnp.where(i > 0, prev_ref[...], 0.0)
    win_ref[halo:halo + tl, :] = u_ref[...]
    win_ref[halo + tl:, :] = jnp.where(i < n_i - 1, next_ref[...], 0.0)
    sh_rows = sh_ref.shape[1]
    for r in range(1, sub):
        for k0 in range(0, sh_rows, chunk):
            k1 = min(k0 + chunk, sh_rows)
            sh_ref[r - 1, k0:k1, :] = win_ref[r + k0:r + k1, :]
    seg = seg_ref[...]
    pad = CONV_K // 2
    for c in range(tl // chunk):
        acc = jnp.zeros((chunk, CONV_W), f32)
        for j in range(CONV_K):
            start = halo - pad + c * chunk + j
            r = start % sub
            src = win_ref if r == 0 else sh_ref.at[r - 1]
            acc = acc + src[start - r:start - r + chunk, :] * w_ref[j:j + 1, :]
        y = acc + cb_ref[...]
        y_hi, y_lo = _split_bf16(y)
        mu = (jnp.dot(y_hi, seg, preferred_element_type=f32)
              + jnp.dot(y_lo, seg, preferred_element_type=f32)) * (1.0 / CONV_GROUP)
        dlt = y - mu
        var = jnp.dot((dlt * dlt).astype(bf16), seg, preferred_element_type=f32) * (1.0 / CONV_GROUP)
        un = dlt * lax.rsqrt(var + EPS) * lg_ref[...] + lb_ref[...]
        o_ref[c * chunk:(c + 1) * chunk, :] = (un * jax.nn.sigmoid(un)).astype(bf16)


def _conv(u, conv_w, conv_b, ln_g, ln_b, seg):
    b, l, _ = u.shape
    tl = min(1024, l)
    halo = 16
    nh = l // halo
    kern = functools.partial(_conv_kernel, tl=tl, chunk=min(128, tl))
    full = lambda a: pl.BlockSpec(a.shape, lambda bi, i: (0,) * a.ndim)
    return pl.pallas_call(
        kern,
        grid=(b, l // tl),
        in_specs=[pl.BlockSpec((None, tl, CONV_W), lambda bi, i: (bi, i, 0)),
                  pl.BlockSpec((None, halo, CONV_W),
                               lambda bi, i: (bi, jnp.maximum(i * (tl // halo) - 1, 0), 0)),
                  pl.BlockSpec((None, halo, CONV_W),
                               lambda bi, i: (bi, jnp.minimum((i + 1) * (tl // halo), nh - 1), 0)),
                  full(conv_w), full(conv_b), full(ln_g), full(ln_b), full(seg)],
        out_specs=pl.BlockSpec((None, tl, CONV_W), lambda bi, i: (bi, i, 0)),
        out_shape=jax.ShapeDtypeStruct((b, l, CONV_W), bf16),
        scratch_shapes=[pltpu.VMEM((tl + 2 * halo, CONV_W), f32),
                        pltpu.VMEM((7, tl + 2 * halo - 8, CONV_W), f32)],
        compiler_params=_params("arbitrary", "arbitrary"),
        name="conv",
    )(u, u, u, conv_w, conv_b, ln_g, ln_b, seg)


def _outproj_kernel(at_ref, cv_ref, x_ref, mod_ref, wo_ref, g2_ref, wr_ref,
                    x1_ref, h2_ref, aff_ref, *, nparts):
    tm, d = x_ref.shape
    gate1 = mod_ref[:, 2 * d:3 * d]
    shift2 = mod_ref[:, 3 * d:4 * d]
    scale2 = mod_ref[:, 4 * d:5 * d]
    w_hi, w_lo = _split_bf16(wr_ref[...])
    parts = [slice(p * (tm // nparts), (p + 1) * (tm // nparts)) for p in range(nparts)]
    mixes = [jnp.dot(at_ref[rows, :], wo_ref[0:ATTN_W, :], preferred_element_type=f32)
             + jnp.dot(cv_ref[rows, :], wo_ref[ATTN_W:, :], preferred_element_type=f32)
             for rows in parts]
    splits = []
    for rows, mix in zip(parts, mixes):
        x1 = x_ref[rows, :] + gate1 * mix
        x1_ref[rows, :] = x1
        ms = jnp.mean(x1 * x1, axis=-1, keepdims=True)
        h2 = x1 * lax.rsqrt(ms + EPS) * g2_ref[...]
        h2 = h2 * (1.0 + scale2) + shift2
        h_hi, h_lo = _split_bf16(h2)
        h2_ref[rows, :] = h_hi.astype(f32)
        splits.append((h_hi, h_lo))
    lgs = [jnp.dot(h_hi, w_hi, preferred_element_type=f32)
           + jnp.dot(h_lo, w_hi, preferred_element_type=f32)
           + jnp.dot(h_hi, w_lo, preferred_element_type=f32) for h_hi, h_lo in splits]
    for rows, lg in zip(parts, lgs):
        logits = lg.T[0:aff_ref.shape[0], :]
        mx = jnp.max(logits, axis=0, keepdims=True)
        ex = jnp.exp(logits - mx)
        aff_ref[:, rows] = ex / jnp.sum(ex, axis=0, keepdims=True)


def _outproj(attn, conv, x, mod, w_out_b, norm2_g, w_router):
    b, l, d = x.shape
    tm = min(1024, l)
    nt = l // tm
    e = w_router.shape[1]
    w_router_t = jnp.pad(w_router.astype(f32), ((0, 0), (0, LANES - e)))
    tok = lambda w: pl.BlockSpec((None, tm, w), lambda bi, i: (bi, i, 0))
    full = lambda a: pl.BlockSpec(a.shape, lambda bi, i: (0,) * a.ndim)
    return pl.pallas_call(
        functools.partial(_outproj_kernel, nparts=4),
        grid=(b, nt),
        in_specs=[tok(ATTN_W), tok(CONV_W), tok(d),
                  pl.BlockSpec((None, 1, mod.shape[-1]), lambda bi, i: (bi, 0, 0)),
                  full(w_out_b), full(norm2_g), full(w_router_t)],
        out_specs=[tok(d),
                   pl.BlockSpec((tm, d), lambda bi, i: (bi * nt + i, 0)),
                   pl.BlockSpec((e, tm), lambda bi, i: (0, bi * nt + i))],
        out_shape=[jax.ShapeDtypeStruct((b, l, d), f32),
                   jax.ShapeDtypeStruct((b * l, d), f32),
                   jax.ShapeDtypeStruct((e, b * l), f32)],
        compiler_params=_params("arbitrary", "arbitrary"),
        name="outproj",
    )(attn, conv, x, mod, w_out_b, norm2_g, w_router_t)


def _route_kernel(aff_ref, cs_ref, cbk_ref, *, cap, tbc):
    a = aff_ref[...]
    ne, nc, _ = a.shape

    def enough(v):
        return jnp.sum((a >= v).astype(f32), axis=(1, 2), keepdims=True) >= cap

    def pow2(k):
        return lax.bitcast_convert_type(jnp.left_shift(k + 127, 23), f32)

    def exp_body(_, st):
        klo, khi = st
        kmid = jnp.right_shift(klo + khi, 1)
        ok = enough(pow2(kmid))
        return jnp.where(ok, kmid, klo), jnp.where(ok, khi, kmid)

    klo, khi = lax.fori_loop(0, 7, exp_body,
                             (jnp.full((ne, 1, 1), -127, i32), jnp.full((ne, 1, 1), 1, i32)))

    def lin_body(_, st):
        lo, hi = st
        mid = lo + (hi - lo) * 0.5
        ok = enough(mid)
        return jnp.where(ok, mid, lo), jnp.where(ok, hi, mid)

    lo, _ = lax.fori_loop(0, 48, lin_body, (pow2(klo), pow2(khi)))
    thr = jnp.min(jnp.where(a >= lo, a, jnp.inf), axis=(1, 2), keepdims=True)

    r_i = lax.broadcasted_iota(i32, (LANES, LANES), 0)
    c_i = lax.broadcasted_iota(i32, (LANES, LANES), 1)
    upper = (r_i <= c_i).astype(bf16)
    ones = jnp.ones((LANES, LANES), bf16)
    rr = lax.broadcasted_iota(i32, (nc, nc), 0)
    cc = lax.broadcasted_iota(i32, (nc, nc), 1)
    lower = (cc < rr).astype(bf16)
    blk_r = lax.broadcasted_iota(i32, (nc, LANES), 0)
    blk_c = lax.broadcasted_iota(i32, (nc, LANES), 1)
    before_blk = (blk_r < blk_c * (tbc // LANES)).astype(bf16)
    ones8 = jnp.ones((8, LANES), bf16)

    def cumsum(m):
        within = jnp.dot(m, upper, preferred_element_type=f32)
        prev = jnp.dot(lower, m, preferred_element_type=f32)
        offs = jnp.dot(prev.astype(bf16), ones, preferred_element_type=f32)
        return within + offs

    for e in range(ne):
        a_e = a[e]
        thr_e = thr[e]
        gt_e = a_e > thr_e
        eq_e = (a_e == thr_e).astype(f32)
        need = cap - jnp.sum(gt_e.astype(f32), keepdims=True)
        rank_eq = cumsum(eq_e.astype(bf16)) - eq_e
        sel = jnp.logical_or(gt_e, jnp.logical_and(eq_e > 0.0, rank_eq < need))
        sel_b = sel.astype(bf16)
        c_incl = cumsum(sel_b)
        cs_ref[e] = jnp.where(sel, c_incl, 0.0).astype(i32)
        per_chunk = lax.dot_general(ones8, sel_b, NT_DIMS, preferred_element_type=f32)
        cbk = jnp.dot(per_chunk.astype(bf16), before_blk, preferred_element_type=f32)
        cbk_ref[e:e + 1, :] = cbk[0:1].astype(i32)


def _route(aff3, cap, tbc):
    e, nc, _ = aff3.shape
    kern = functools.partial(_route_kernel, cap=cap, tbc=tbc)
    return pl.pallas_call(
        kern,
        out_shape=[jax.ShapeDtypeStruct((e, nc, LANES), i32),
                   jax.ShapeDtypeStruct((e, LANES), i32)],
        compiler_params=pltpu.CompilerParams(vmem_limit_bytes=VMEM_LIMIT),
        name="route",
    )(aff3)


SC_CORES = 2
SC_SUBCORES = 16
SC_LANES = 16


SC_CHUNK = 32


def _sc_dispatch(cs_flat, aff_flat, h_tiles, ne, cap):
    n = cs_flat.shape[0] // ne
    workers = SC_CORES * SC_SUBCORES
    per_e = workers // ne
    per_w = cap // per_e
    assert per_e * ne == workers and per_w * per_e == cap
    assert n % SC_LANES == 0 and per_w % SC_CHUNK == 0
    table = h_tiles
    d_model = table.shape[1]
    mesh = plsc.VectorSubcoreMesh(core_axis_name="c", subcore_axis_name="s",
                                  num_cores=SC_CORES, num_subcores=SC_SUBCORES)

    @functools.partial(
        pl.kernel, mesh=mesh,
        out_type=[jax.ShapeDtypeStruct((ne * cap, d_model), f32),
                  jax.ShapeDtypeStruct((ne * cap,), f32)],
        scratch_types=[pltpu.VMEM((n,), i32), pltpu.VMEM((n,), f32),
                       pltpu.VMEM((per_w,), i32), pltpu.VMEM((per_w,), f32),
                       pltpu.VMEM((SC_CHUNK, d_model), f32), pltpu.SemaphoreType.DMA],
        compiler_params=pltpu.CompilerParams(needs_layout_passes=False, use_tc_tiling_on_sc=True),
        name="sc_dispatch")
    def dispatch(cs_hbm, aff_hbm, table_hbm, xe_hbm, g_hbm, cs_v, aff_v, idx_v, g_v, rows_v, sem):
        wid = lax.axis_index("s") * SC_CORES + lax.axis_index("c")
        e = wid // per_e
        lo = (wid - e * per_e) * per_w
        base = e * cap + lo
        pltpu.sync_copy(cs_hbm.at[pl.ds(e * n, n)], cs_v)
        pltpu.sync_copy(aff_hbm.at[pl.ds(e * n, n)], aff_v)
        lane = lax.iota(i32, SC_LANES)

        @pl.loop(0, n // SC_LANES)
        def _(i):
            off = pl.multiple_of(i * SC_LANES, SC_LANES)
            local = cs_v[pl.ds(off, SC_LANES)] - 1 - lo
            mine = jnp.logical_and(local >= 0, local < per_w)
            plsc.store_scatter(idx_v, [local], off + lane, mask=mine)
            plsc.store_scatter(g_v, [local], aff_v[pl.ds(off, SC_LANES)], mask=mine)

        pltpu.sync_copy(g_v, g_hbm.at[pl.ds(base, per_w)])

        @pl.loop(0, per_w // SC_CHUNK)
        def _(c):
            off = pl.multiple_of(c * SC_CHUNK, SC_CHUNK)
            pltpu.async_copy(table_hbm.at[idx_v.at[pl.ds(off, SC_CHUNK)]], rows_v, sem).wait()
            pltpu.sync_copy(rows_v, xe_hbm.at[pl.ds(base + off, SC_CHUNK)])

    return dispatch(cs_flat, aff_flat, table)


def _expert_kernel(x_ref, g_ref, wg_ref, wu_ref, wd_ref, y_ref, xb_ref, acc_ref, gcol_ref):
    f = pl.program_id(2)

    @pl.when(f == 0)
    def _():
        xb_ref[...] = x_ref[...].astype(bf16)
        acc_ref[...] = jnp.zeros_like(acc_ref)
        gcol_ref[...] = jnp.broadcast_to(g_ref[...], (LANES, g_ref.shape[-1])).T

    xb = xb_ref[...]
    hg = jnp.dot(xb, wg_ref[...].astype(bf16), preferred_element_type=f32)
    hu = jnp.dot(xb, wu_ref[...].astype(bf16), preferred_element_type=f32)
    hid = (hg * jax.nn.sigmoid(hg) * hu).astype(bf16)
    acc_ref[...] += jnp.dot(hid, wd_ref[...].astype(bf16), preferred_element_type=f32)

    @pl.when(f == pl.num_programs(2) - 1)
    def _():
        gcol = gcol_ref[...]
        d = acc_ref.shape[-1]
        for c in range(d // LANES):
            cols = slice(c * LANES, (c + 1) * LANES)
            y_ref[:, cols] = (acc_ref[:, cols] * gcol).astype(bf16)


def _experts(xe, g_rows, wg_b, wu_b, wd_b, tm, fc):
    e, d, fdim = wg_b.shape
    rows = xe.shape[0]
    per_e = rows // e
    nt = per_e // tm
    return pl.pallas_call(
        _expert_kernel,
        grid=(e, nt, fdim // fc),
        in_specs=[pl.BlockSpec((tm, d), lambda ei, t, f: (ei * nt + t, 0)),
                  pl.BlockSpec((None, 1, tm), lambda ei, t, f: (ei * nt + t, 0, 0)),
                  pl.BlockSpec((None, d, fc), lambda ei, t, f: (ei, 0, f)),
                  pl.BlockSpec((None, d, fc), lambda ei, t, f: (ei, 0, f)),
                  pl.BlockSpec((None, fc, d), lambda ei, t, f: (ei, f, 0))],
        out_specs=pl.BlockSpec((tm, d), lambda ei, t, f: (ei * nt + t, 0)),
        out_shape=jax.ShapeDtypeStruct((rows, d), bf16),
        scratch_shapes=[pltpu.VMEM((tm, d), bf16), pltpu.VMEM((tm, d), f32),
                        pltpu.VMEM((tm, LANES), f32)],
        compiler_params=_params("arbitrary", "arbitrary", "arbitrary"),
        name="experts",
    )(xe, g_rows.reshape(rows // tm, 1, tm), wg_b, wu_b, wd_b)


def _combine_kernel(cbk_ref, cs_ref, x1_ref, mod_ref, ye_ref, o_ref, win_ref, xwin_ref, sem, xsem,
                    acc_ref, *, cap, tbc, win, ne, total_rows):
    blk = pl.program_id(0)
    nblk = pl.num_programs(0)
    d = acc_ref.shape[-1]
    slot = blk % 2

    def region(ee):
        return ee * cap

    def first_row(bb, ee):
        st = cbk_ref[ee * LANES + bb]
        row = region(ee) + jnp.left_shift(jnp.right_shift(st, 4), 4)
        return pl.multiple_of(jnp.minimum(row, total_rows - win), BF16_ROWS)

    def start_block(bb, sl):
        for ee in range(ne):
            pltpu.make_async_copy(ye_ref.at[pl.ds(first_row(bb, ee), win)],
                                  win_ref.at[sl, pl.ds(ee * win, win)], sem.at[sl]).start()

    @pl.when(blk == 0)
    def _():
        start_block(blk, slot)

    @pl.when(blk + 1 < nblk)
    def _():
        start_block(blk + 1, (blk + 1) % 2)

    pltpu.make_async_copy(ye_ref.at[pl.ds(0, ne * win)], win_ref.at[slot], sem.at[slot]).wait()

    w_iota = lax.broadcasted_iota(i32, (win, tbc), 0)

    def onehot_of(cs_row, base_slot, done_slots):
        hit = jnp.logical_and(cs_row == base_slot + 1 + w_iota, cs_row > done_slots)
        return hit.astype(f32).T.astype(bf16)

    onehot = jnp.concatenate(
        [onehot_of(cs_ref[ee:ee + 1, :], first_row(blk, ee) - region(ee), 0) for ee in range(ne)],
        axis=1)
    acc_ref[...] = jnp.dot(onehot, win_ref[slot], preferred_element_type=f32)

    def extra_expert(ee, carry):
        row0 = first_row(blk, ee)
        base0 = row0 - region(ee)
        end = cbk_ref[ee * LANES + blk + 1]
        cs_row = cs_ref[pl.ds(ee, 1), :]

        def extra(w, c2):
            row_w = pl.multiple_of(jnp.minimum(row0 + w * win, total_rows - win), BF16_ROWS)
            cp = pltpu.make_async_copy(ye_ref.at[pl.ds(row_w, win)], xwin_ref, xsem)
            cp.start()
            cp.wait()
            oh = onehot_of(cs_row, row_w - region(ee), base0 + w * win)
            acc_ref[...] += jnp.dot(oh, xwin_ref[...], preferred_element_type=f32)
            return c2

        lax.fori_loop(1, (end - base0 + win - 1) // win, extra, 0)
        return carry

    lax.fori_loop(0, ne, extra_expert, 0)

    gate2 = mod_ref[:, 5 * d:6 * d]
    o_ref[...] = x1_ref[...] + gate2 * acc_ref[...]


def _combine(cbk, cs, x1, mod, ye, cap, tbc):
    b, l, d = x1.shape
    n = b * l
    e = cs.shape[0]
    nb = n // tbc
    win = min(128, cap)
    per_b = l // tbc
    kern = functools.partial(_combine_kernel, cap=cap, tbc=tbc, win=win, ne=e,
                             total_rows=ye.shape[0])
    return pl.pallas_call(
        kern,
        grid_spec=pltpu.PrefetchScalarGridSpec(
            num_scalar_prefetch=1, grid=(nb,),
            in_specs=[pl.BlockSpec((e, tbc), lambda bk, c: (0, bk)),
                      pl.BlockSpec((tbc, d), lambda bk, c: (bk, 0)),
                      pl.BlockSpec((None, 1, mod.shape[-1]), lambda bk, c: (bk // per_b, 0, 0)),
                      pl.BlockSpec(memory_space=pl.ANY)],
            out_specs=pl.BlockSpec((tbc, d), lambda bk, c: (bk, 0)),
            scratch_shapes=[pltpu.VMEM((2, e * win, d), bf16), pltpu.VMEM((win, d), bf16),
                            pltpu.SemaphoreType.DMA((2,)), pltpu.SemaphoreType.DMA,
                            pltpu.VMEM((tbc, d), f32)]),
        out_shape=jax.ShapeDtypeStruct((n, d), f32),
        compiler_params=_params("arbitrary"),
        name="combine",
    )(cbk.reshape(-1), cs, x1.reshape(n, d), mod, ye).reshape(b, l, d)


TBF = 512
TM_EXPERT = 2048
FC_EXPERT = 256


def kernel(x_prompt, x_sample, c_prompt, c_sample, ada_w, ada_b, norm1_g, w_in, q_norm_g, k_norm_g, rpb,
           conv_w, conv_b, conv_ln_g, conv_ln_b, w_out, norm2_g, w_router, w_gate, w_up, w_down):
    xs = (x_prompt, x_sample)
    d = x_prompt.shape[-1]
    ne = w_router.shape[1]
    row = lambda a: a.reshape(1, -1).astype(f32)

    nb0, nb1 = c_prompt.shape[0], c_sample.shape[0]
    pad = (-(nb0 + nb1)) % 8
    c_all = jnp.concatenate([c_prompt, c_sample, jnp.zeros((pad, d), f32)], axis=0)
    mod_all = _ada(c_all, ada_w, ada_b)
    mods = (mod_all[:nb0].reshape(nb0, 1, -1), mod_all[nb0:nb0 + nb1].reshape(nb1, 1, -1))

    seg_i = jnp.arange(ATTN_W) // HEAD_DIM
    seg = (seg_i[:, None] == seg_i[None, :]).astype(bf16)
    gq_t = jnp.tile(q_norm_g.astype(f32), N_HEADS).reshape(1, ATTN_W)
    gk_t = jnp.tile(k_norm_g.astype(f32), N_HEADS).reshape(1, ATTN_W)
    bias_tab = _bias_table(rpb)
    w_in_b = w_in.astype(bf16)
    w_out_b = w_out.astype(bf16)

    staged = []
    for x, mod in zip(xs, mods):
        b, l, _ = x.shape
        n = b * l
        cap = max(1, EC_FACTOR * n // ne)
        q, k, v, u = _inproj(x, mod, row(norm1_g), w_in_b, seg, gq_t, gk_t)
        attn = _attention(q, k, v, bias_tab)
        conv = _conv(u, conv_w.astype(f32), row(conv_b), row(conv_ln_g), row(conv_ln_b), seg)
        x1, h2, aff_t = _outproj(attn, conv, x, mod, w_out_b, row(norm2_g), w_router)
        aff3 = aff_t.reshape(ne, n // LANES, LANES)
        cs, cbk = _route(aff3, cap, TBF)
        xe, g = _sc_dispatch(cs.reshape(-1), aff3.reshape(-1), h2, ne, cap)
        staged.append((x1, mod, xe, g.reshape(ne, cap), cs.reshape(ne, n), cbk, cap))

    outs = []
    for x1, mod, xe, g, cs, cbk, cap in staged:
        ye = _experts(xe, g, w_gate, w_up, w_down, min(TM_EXPERT, cap), min(FC_EXPERT, w_gate.shape[2]))
        outs.append(_combine(cbk, cs, x1, mod, ye, cap, TBF))
    return tuple(outs)
```

```python
import functools

import jax
import jax.numpy as jnp
import numpy as np
from jax import lax
from jax.experimental import pallas as pl
from jax.experimental.pallas import tpu as pltpu
from jax.experimental.pallas import tpu_sc as plsc

f32 = jnp.float32
bf16 = jnp.bfloat16
i32 = jnp.int32

GRID_W = 64
N_HEADS = 8
HEAD_DIM = 64
ATTN_W = N_HEADS * HEAD_DIM
CONV_W = 512
CONV_K = 31
CONV_GROUP = 64
WIN_ROWS = 8
WIN_COLS = 16
EC_FACTOR = 2
EPS = 1e-6
NEG = -1e30

LANES = 128
BF16_ROWS = 16
CBK_W = 256
VMEM_LIMIT = 56 * 1024 * 1024

NT_DIMS = (((1,), (1,)), ((), ()))


def _params(*sem):
    return pltpu.CompilerParams(dimension_semantics=sem, vmem_limit_bytes=VMEM_LIMIT)


def _split_bf16(x):
    hi = x.astype(bf16)
    lo = (x - hi.astype(f32)).astype(bf16)
    return hi, lo


def _ada_kernel(c_ref, w_ref, b_ref, o_ref):
    c = c_ref[...]
    s = c * jax.nn.sigmoid(c)
    s_hi, s_lo = _split_bf16(s)
    w_hi, w_lo = _split_bf16(w_ref[...])
    acc = jnp.dot(s_hi, w_hi, preferred_element_type=f32)
    acc += jnp.dot(s_hi, w_lo, preferred_element_type=f32)
    acc += jnp.dot(s_lo, w_hi, preferred_element_type=f32)
    o_ref[...] = acc + b_ref[...]


def _ada(c_all, ada_w, ada_b):
    rows, d = c_all.shape
    ncols = ada_w.shape[1]
    tn = next(t for t in (1024, 512, 256, 128) if ncols % t == 0)
    return pl.pallas_call(
        _ada_kernel,
        grid=(ncols // tn,),
        in_specs=[pl.BlockSpec((rows, d), lambda j: (0, 0)),
                  pl.BlockSpec((d, tn), lambda j: (0, j)),
                  pl.BlockSpec((1, tn), lambda j: (0, j))],
        out_specs=pl.BlockSpec((rows, tn), lambda j: (0, j)),
        out_shape=jax.ShapeDtypeStruct((rows, ncols), f32),
        compiler_params=_params("arbitrary"),
        name="ada",
    )(c_all, ada_w, ada_b.reshape(1, ncols))


def _inproj_kernel(x_ref, mod_ref, g1_ref, w_ref, seg_ref, gq_ref, gk_ref,
                   q_ref, k_ref, v_ref, u_ref, *, nparts):
    tm, d = x_ref.shape
    shift = mod_ref[:, 0:d]
    scale = mod_ref[:, d:2 * d]
    seg = seg_ref[...]
    qs = gq_ref[...] * gk_ref[...] * (HEAD_DIM ** -0.5)
    parts = [slice(p * (tm // nparts), (p + 1) * (tm // nparts)) for p in range(nparts)]
    hs = []
    for rows in parts:
        x = x_ref[rows, :]
        ms = jnp.mean(x * x, axis=-1, keepdims=True)
        h = x * lax.rsqrt(ms + EPS) * g1_ref[...]
        hs.append((h * (1.0 + scale) + shift).astype(bf16))
    zs = [jnp.dot(h, w_ref[...], preferred_element_type=f32) for h in hs]
    for rows, z in zip(parts, zs):
        zq = z[:, 0:ATTN_W]
        zk = z[:, ATTN_W:2 * ATTN_W]
        ssq_q = jnp.dot((zq * zq).astype(bf16), seg, preferred_element_type=f32)
        ssq_k = jnp.dot((zk * zk).astype(bf16), seg, preferred_element_type=f32)
        q_ref[rows, :] = (zq * lax.rsqrt(ssq_q * (1.0 / HEAD_DIM) + EPS) * qs).astype(bf16)
        k_ref[rows, :] = (zk * lax.rsqrt(ssq_k * (1.0 / HEAD_DIM) + EPS)).astype(bf16)
        v_ref[rows, :] = z[:, 2 * ATTN_W:3 * ATTN_W].astype(bf16)
        a = z[:, 3 * ATTN_W:3 * ATTN_W + CONV_W]
        gt = z[:, 3 * ATTN_W + CONV_W:]
        u_ref[rows, :] = a * jax.nn.sigmoid(gt)


def _inproj(x, mod, norm1_g, w_in_b, seg, gq_t, gk_t):
    b, l, d = x.shape
    tm = min(1024, l)
    ncols = w_in_b.shape[1]
    tok = lambda w: pl.BlockSpec((None, tm, w), lambda bi, i: (bi, i, 0))
    full = lambda a: pl.BlockSpec(a.shape, lambda bi, i: (0,) * a.ndim)
    return pl.pallas_call(
        functools.partial(_inproj_kernel, nparts=2),
        grid=(b, l // tm),
        in_specs=[tok(d),
                  pl.BlockSpec((None, 1, mod.shape[-1]), lambda bi, i: (bi, 0, 0)),
                  full(norm1_g), full(w_in_b), full(seg), full(gq_t), full(gk_t)],
        out_specs=[tok(ATTN_W), tok(ATTN_W), tok(ATTN_W), tok(CONV_W)],
        out_shape=[jax.ShapeDtypeStruct((b, l, ATTN_W), bf16)] * 3
                  + [jax.ShapeDtypeStruct((b, l, CONV_W), f32)],
        compiler_params=_params("arbitrary", "arbitrary"),
        name="inproj",
    )(x, mod, norm1_g, w_in_b, seg, gq_t, gk_t)


def _attn_kernel(q_ref, k_ref, v_ref, bias_ref, o_ref, *, rows, rg, wr, nr):
    g = pl.program_id(1)
    lane = lax.broadcasted_iota(i32, (GRID_W, LANES), 1)
    first_head = lane < HEAD_DIM
    nkeys = wr * GRID_W

    pairs = [(rr, p) for rr in range(nr) for p in range(N_HEADS // 2)]

    def rows_body(it, carry):
        geo = []
        for rr in range(nr):
            lr = it * nr + rr
            r = g * rg + lr
            rs = jnp.clip(r - wr // 2, 0, rows - wr)
            geo.append((pl.multiple_of(lr * GRID_W, GRID_W),
                        pl.multiple_of(rs * GRID_W, GRID_W),
                        rs - r + (WIN_ROWS - 1)))
        scores = []
        for rr, p in pairs:
            q0, tok0, rb = geo[rr]
            cols = slice(p * LANES, (p + 1) * LANES)
            qp = q_ref[pl.ds(q0, GRID_W), cols]
            kp = k_ref[pl.ds(tok0, nkeys), cols]
            zero = jnp.zeros_like(qp)
            q2 = jnp.concatenate([jnp.where(first_head, qp, zero),
                                  jnp.where(first_head, zero, qp)], axis=0)
            s = lax.dot_general(q2, kp, NT_DIMS, preferred_element_type=f32)
            bias = jnp.concatenate(
                [jnp.concatenate([bias_ref[2 * p + j, rb + 2 * m] for m in range(nkeys // LANES)],
                                 axis=-1) for j in range(2)], axis=0)
            scores.append(s + bias)
        probs = []
        for s in scores:
            mx = jnp.max(s, axis=-1, keepdims=True)
            e = jnp.exp(s - mx)
            probs.append((e.astype(bf16), jnp.sum(e, axis=-1, keepdims=True)))
        for (rr, p), (e, den) in zip(pairs, probs):
            q0, tok0, rb = geo[rr]
            cols = slice(p * LANES, (p + 1) * LANES)
            vp = v_ref[pl.ds(tok0, nkeys), cols]
            o = jnp.dot(e, vp, preferred_element_type=f32) / den
            o_ref[pl.ds(q0, GRID_W), cols] = jnp.where(
                first_head, o[0:GRID_W], o[GRID_W:]).astype(bf16)
        return carry

    lax.fori_loop(0, rg // nr, rows_body, 0)


def _attention(q, k, v, bias_tab):
    b, l, _ = q.shape
    rows = l // GRID_W
    wr = min(WIN_ROWS, rows)
    rg = min(16, rows)
    once = pl.Buffered(1)
    kern = functools.partial(_attn_kernel, rows=rows, rg=rg, wr=wr, nr=8)
    return pl.pallas_call(
        kern,
        grid=(b, rows // rg),
        in_specs=[pl.BlockSpec((None, rg * GRID_W, ATTN_W), lambda bi, g: (bi, g, 0)),
                  pl.BlockSpec((None, l, ATTN_W), lambda bi, g: (bi, 0, 0), pipeline_mode=once),
                  pl.BlockSpec((None, l, ATTN_W), lambda bi, g: (bi, 0, 0), pipeline_mode=once),
                  pl.BlockSpec(bias_tab.shape, lambda bi, g: (0, 0, 0, 0), pipeline_mode=once)],
        out_specs=pl.BlockSpec((None, rg * GRID_W, ATTN_W), lambda bi, g: (bi, g, 0)),
        out_shape=jax.ShapeDtypeStruct((b, l, ATTN_W), bf16),
        compiler_params=_params("arbitrary", "arbitrary"),
        name="attention",
    )(q, k, v, bias_tab)


def _bias_table(rpb):
    cols = np.arange(GRID_W)
    col_start = np.clip(cols - WIN_COLS // 2, 0, GRID_W - WIN_COLS)
    kc = np.arange(GRID_W)[None, :]
    valid = (kc >= col_start[:, None]) & (kc < col_start[:, None] + WIN_COLS)
    off = kc - cols[:, None] + (WIN_COLS - 1)
    pick = (off[None] == np.arange(2 * WIN_COLS - 1)[:, None, None]) & valid[None]
    tab = jnp.einsum('hrk,kcd->hrcd', rpb.astype(f32), jnp.asarray(pick, f32),
                     precision=lax.Precision.HIGHEST)
    tab = jnp.where(jnp.asarray(valid)[None, None], tab, NEG)
    return jnp.concatenate([tab[:, :-1], tab[:, 1:]], axis=-1)


def _conv_kernel(u_ref, prev_ref, next_ref, w_ref, cb_ref, lg_ref, lb_ref, seg_ref, o_ref,
                 win_ref, sh_ref, *, tl, chunk):
    i = pl.program_id(1)
    n_i = pl.num_programs(1)
    halo = prev_ref.shape[0]
    sub = 8
    win_ref[0:halo, :] = jnp.where(i > 0, prev_ref[...], 0.0)
    win_ref[halo:halo + tl, :] = u_ref[...]
    win_ref[halo + tl:, :] = jnp.where(i < n_i - 1, next_ref[...], 0.0)
    sh_rows = sh_ref.shape[1]
    for r in range(1, sub):
        for k0 in range(0, sh_rows, chunk):
            k1 = min(k0 + chunk, sh_rows)
            sh_ref[r - 1, k0:k1, :] = win_ref[r + k0:r + k1, :]
    seg = seg_ref[...]
    pad = CONV_K // 2
    for c in range(tl // chunk):
        acc = jnp.zeros((chunk, CONV_W), f32)
        for j in range(CONV_K):
            start = halo - pad + c * chunk + j
            r = start % sub
            src = win_ref if r == 0 else sh_ref.at[r - 1]
            acc = acc + src[start - r:start - r + chunk, :] * w_ref[j:j + 1, :]
        y = acc + cb_ref[...]
        y_hi, y_lo = _split_bf16(y)
        mu = (jnp.dot(y_hi, seg, preferred_element_type=f32)
              + jnp.dot(y_lo, seg, preferred_element_type=f32)) * (1.0 / CONV_GROUP)
        dlt = y - mu
        var = jnp.dot((dlt * dlt).astype(bf16), seg, preferred_element_type=f32) * (1.0 / CONV_GROUP)
        un = dlt * lax.rsqrt(var + EPS) * lg_ref[...] + lb_ref[...]
        o_ref[c * chunk:(c + 1) * chunk, :] = (un * jax.nn.sigmoid(un)).astype(bf16)


def _conv(u, conv_w, conv_b, ln_g, ln_b, seg):
    b, l, _ = u.shape
    tl = min(1024, l)
    halo = 16
    nh = l // halo
    kern = functools.partial(_conv_kernel, tl=tl, chunk=min(128, tl))
    full = lambda a: pl.BlockSpec(a.shape, lambda bi, i: (0,) * a.ndim)
    return pl.pallas_call(
        kern,
        grid=(b, l // tl),
        in_specs=[pl.BlockSpec((None, tl, CONV_W), lambda bi, i: (bi, i, 0)),
                  pl.BlockSpec((None, halo, CONV_W),
                               lambda bi, i: (bi, jnp.maximum(i * (tl // halo) - 1, 0), 0)),
                  pl.BlockSpec((None, halo, CONV_W),
                               lambda bi, i: (bi, jnp.minimum((i + 1) * (tl // halo), nh - 1), 0)),
                  full(conv_w), full(conv_b), full(ln_g), full(ln_b), full(seg)],
        out_specs=pl.BlockSpec((None, tl, CONV_W), lambda bi, i: (bi, i, 0)),
        out_shape=jax.ShapeDtypeStruct((b, l, CONV_W), bf16),
        scratch_shapes=[pltpu.VMEM((tl + 2 * halo, CONV_W), f32),
                        pltpu.VMEM((7, tl + 2 * halo - 8, CONV_W), f32)],
        compiler_params=_params("arbitrary", "arbitrary"),
        name="conv",
    )(u, u, u, conv_w, conv_b, ln_g, ln_b, seg)


def _outproj_kernel(at_ref, cv_ref, x_ref, mod_ref, wo_ref, g2_ref, wr_ref,
                    x1_ref, h2_ref, aff_ref, *, nparts):
    tm, d = x_ref.shape
    gate1 = mod_ref[:, 2 * d:3 * d]
    shift2 = mod_ref[:, 3 * d:4 * d]
    scale2 = mod_ref[:, 4 * d:5 * d]
    w_hi, w_lo = _split_bf16(wr_ref[...])
    parts = [slice(p * (tm // nparts), (p + 1) * (tm // nparts)) for p in range(nparts)]
    mixes = [jnp.dot(at_ref[rows, :], wo_ref[0:ATTN_W, :], preferred_element_type=f32)
             + jnp.dot(cv_ref[rows, :], wo_ref[ATTN_W:, :], preferred_element_type=f32)
             for rows in parts]
    splits = []
    for rows, mix in zip(parts, mixes):
        x1 = x_ref[rows, :] + gate1 * mix
        x1_ref[rows, :] = x1
        ms = jnp.mean(x1 * x1, axis=-1, keepdims=True)
        h2 = x1 * lax.rsqrt(ms + EPS) * g2_ref[...]
        h2 = h2 * (1.0 + scale2) + shift2
        h_hi, h_lo = _split_bf16(h2)
        h2_ref[rows, :] = h_hi.astype(f32)
        splits.append((h_hi, h_lo))
    lgs = [jnp.dot(h_hi, w_hi, preferred_element_type=f32)
           + jnp.dot(h_lo, w_hi, preferred_element_type=f32)
           + jnp.dot(h_hi, w_lo, preferred_element_type=f32) for h_hi, h_lo in splits]
    for rows, lg in zip(parts, lgs):
        logits = lg.T[0:aff_ref.shape[0], :]
        mx = jnp.max(logits, axis=0, keepdims=True)
        ex = jnp.exp(logits - mx)
        aff_ref[:, rows] = ex / jnp.sum(ex, axis=0, keepdims=True)


def _outproj(attn, conv, x, mod, w_out_b, norm2_g, w_router):
    b, l, d = x.shape
    tm = min(1024, l)
    nt = l // tm
    e = w_router.shape[1]
    w_router_t = jnp.pad(w_router.astype(f32), ((0, 0), (0, LANES - e)))
    tok = lambda w: pl.BlockSpec((None, tm, w), lambda bi, i: (bi, i, 0))
    full = lambda a: pl.BlockSpec(a.shape, lambda bi, i: (0,) * a.ndim)
    return pl.pallas_call(
        functools.partial(_outproj_kernel, nparts=4),
        grid=(b, nt),
        in_specs=[tok(ATTN_W), tok(CONV_W), tok(d),
                  pl.BlockSpec((None, 1, mod.shape[-1]), lambda bi, i: (bi, 0, 0)),
                  full(w_out_b), full(norm2_g), full(w_router_t)],
        out_specs=[tok(d),
                   pl.BlockSpec((tm, d), lambda bi, i: (bi * nt + i, 0)),
                   pl.BlockSpec((e, tm), lambda bi, i: (0, bi * nt + i))],
        out_shape=[jax.ShapeDtypeStruct((b, l, d), f32),
                   jax.ShapeDtypeStruct((b * l, d), f32),
                   jax.ShapeDtypeStruct((e, b * l), f32)],
        compiler_params=_params("arbitrary", "arbitrary"),
        name="outproj",
    )(attn, conv, x, mod, w_out_b, norm2_g, w_router_t)


def _route_kernel(aff_ref, cs_ref, cbk_ref, *, cap, tbc):
    a = aff_ref[...]
    ne, nc, _ = a.shape

    def enough(v):
        return jnp.sum((a >= v).astype(f32), axis=(1, 2), keepdims=True) >= cap

    def pow2(k):
        return lax.bitcast_convert_type(jnp.left_shift(k + 127, 23), f32)

    def exp_body(_, st):
        klo, khi = st
        kmid = jnp.right_shift(klo + khi, 1)
        ok = enough(pow2(kmid))
        return jnp.where(ok, kmid, klo), jnp.where(ok, khi, kmid)

    klo, khi = lax.fori_loop(0, 7, exp_body,
                             (jnp.full((ne, 1, 1), -127, i32), jnp.full((ne, 1, 1), 1, i32)))

    def lin_body(_, st):
        lo, hi = st
        mid = lo + (hi - lo) * 0.5
        ok = enough(mid)
        return jnp.where(ok, mid, lo), jnp.where(ok, hi, mid)

    lo, _ = lax.fori_loop(0, 48, lin_body, (pow2(klo), pow2(khi)))
    thr = jnp.min(jnp.where(a >= lo, a, jnp.inf), axis=(1, 2), keepdims=True)

    r_i = lax.broadcasted_iota(i32, (LANES, LANES), 0)
    c_i = lax.broadcasted_iota(i32, (LANES, LANES), 1)
    upper = (r_i <= c_i).astype(bf16)
    ones = jnp.ones((LANES, LANES), bf16)
    rr = lax.broadcasted_iota(i32, (nc, nc), 0)
    cc = lax.broadcasted_iota(i32, (nc, nc), 1)
    lower = (cc < rr).astype(bf16)
    blk_r = lax.broadcasted_iota(i32, (nc, CBK_W), 0)
    blk_c = lax.broadcasted_iota(i32, (nc, CBK_W), 1)
    before_blk = (blk_r < blk_c * (tbc // LANES)).astype(bf16)
    ones8 = jnp.ones((8, LANES), bf16)

    def cumsum(m):
        within = jnp.dot(m, upper, preferred_element_type=f32)
        prev = jnp.dot(lower, m, preferred_element_type=f32)
        offs = jnp.dot(prev.astype(bf16), ones, preferred_element_type=f32)
        return within + offs

    for e in range(ne):
        a_e = a[e]
        thr_e = thr[e]
        gt_e = a_e > thr_e
        eq_e = (a_e == thr_e).astype(f32)
        need = cap - jnp.sum(gt_e.astype(f32), keepdims=True)
        rank_eq = cumsum(eq_e.astype(bf16)) - eq_e
        sel = jnp.logical_or(gt_e, jnp.logical_and(eq_e > 0.0, rank_eq < need))
        sel_b = sel.astype(bf16)
        c_incl = cumsum(sel_b)
        cs_ref[e] = jnp.where(sel, c_incl, 0.0).astype(i32)
        per_chunk = lax.dot_general(ones8, sel_b, NT_DIMS, preferred_element_type=f32)
        cbk = jnp.dot(per_chunk.astype(bf16), before_blk, preferred_element_type=f32)
        cbk_ref[e:e + 1, :] = cbk[0:1].astype(i32)


def _route(aff3, cap, tbc):
    e, nc, _ = aff3.shape
    kern = functools.partial(_route_kernel, cap=cap, tbc=tbc)
    return pl.pallas_call(
        kern,
        out_shape=[jax.ShapeDtypeStruct((e, nc, LANES), i32),
                   jax.ShapeDtypeStruct((e, CBK_W), i32)],
        compiler_params=pltpu.CompilerParams(vmem_limit_bytes=VMEM_LIMIT),
        name="route",
    )(aff3)


SC_CORES = 2
SC_SUBCORES = 16
SC_LANES = 16


SC_CHUNK = 32


def _sc_dispatch(cs_flat, aff_flat, h_tiles, ne, cap):
    n = cs_flat.shape[0] // ne
    workers = SC_CORES * SC_SUBCORES
    per_e = workers // ne
    per_w = cap // per_e
    assert per_e * ne == workers and per_w * per_e == cap
    assert n % SC_LANES == 0 and per_w % SC_CHUNK == 0
    table = h_tiles
    d_model = table.shape[1]
    mesh = plsc.VectorSubcoreMesh(core_axis_name="c", subcore_axis_name="s",
                                  num_cores=SC_CORES, num_subcores=SC_SUBCORES)

    @functools.partial(
        pl.kernel, mesh=mesh,
        out_type=[jax.ShapeDtypeStruct((ne * cap, d_model), f32),
                  jax.ShapeDtypeStruct((ne * cap,), f32)],
        scratch_types=[pltpu.VMEM((n,), i32), pltpu.VMEM((n,), f32),
                       pltpu.VMEM((per_w,), i32), pltpu.VMEM((per_w,), f32),
                       pltpu.VMEM((SC_CHUNK, d_model), f32), pltpu.SemaphoreType.DMA],
        compiler_params=pltpu.CompilerParams(needs_layout_passes=False, use_tc_tiling_on_sc=True),
        name="sc_dispatch")
    def dispatch(cs_hbm, aff_hbm, table_hbm, xe_hbm, g_hbm, cs_v, aff_v, idx_v, g_v, rows_v, sem):
        wid = lax.axis_index("s") * SC_CORES + lax.axis_index("c")
        e = wid // per_e
        lo = (wid - e * per_e) * per_w
        base = e * cap + lo
        pltpu.sync_copy(cs_hbm.at[pl.ds(e * n, n)], cs_v)
        pltpu.sync_copy(aff_hbm.at[pl.ds(e * n, n)], aff_v)
        lane = lax.iota(i32, SC_LANES)

        @pl.loop(0, n // SC_LANES)
        def _(i):
            off = pl.multiple_of(i * SC_LANES, SC_LANES)
            local = cs_v[pl.ds(off, SC_LANES)] - 1 - lo
            mine = jnp.logical_and(local >= 0, local < per_w)
            plsc.store_scatter(idx_v, [local], off + lane, mask=mine)
            plsc.store_scatter(g_v, [local], aff_v[pl.ds(off, SC_LANES)], mask=mine)

        pltpu.sync_copy(g_v, g_hbm.at[pl.ds(base, per_w)])

        @pl.loop(0, per_w // SC_CHUNK)
        def _(c):
            off = pl.multiple_of(c * SC_CHUNK, SC_CHUNK)
            pltpu.async_copy(table_hbm.at[idx_v.at[pl.ds(off, SC_CHUNK)]], rows_v, sem).wait()
            pltpu.sync_copy(rows_v, xe_hbm.at[pl.ds(base + off, SC_CHUNK)])

    return dispatch(cs_flat, aff_flat, table)


def _expert_kernel(x_ref, g_ref, wg_ref, wu_ref, wd_ref, y_ref, xb_ref, acc_ref, gcol_ref):
    f = pl.program_id(2)

    @pl.when(f == 0)
    def _():
        xb_ref[...] = x_ref[...].astype(bf16)
        acc_ref[...] = jnp.zeros_like(acc_ref)
        gcol_ref[...] = jnp.broadcast_to(g_ref[...], (LANES, g_ref.shape[-1])).T

    xb = xb_ref[...]
    hg = jnp.dot(xb, wg_ref[...].astype(bf16), preferred_element_type=f32)
    hu = jnp.dot(xb, wu_ref[...].astype(bf16), preferred_element_type=f32)
    hid = (hg * jax.nn.sigmoid(hg) * hu).astype(bf16)
    acc_ref[...] += jnp.dot(hid, wd_ref[...].astype(bf16), preferred_element_type=f32)

    @pl.when(f == pl.num_programs(2) - 1)
    def _():
        gcol = gcol_ref[...]
        d = acc_ref.shape[-1]
        for c in range(d // LANES):
            cols = slice(c * LANES, (c + 1) * LANES)
            y_ref[:, cols] = (acc_ref[:, cols] * gcol).astype(bf16)


def _experts(xe, g_rows, wg_b, wu_b, wd_b, tm, fc):
    e, d, fdim = wg_b.shape
    rows = xe.shape[0]
    per_e = rows // e
    nt = per_e // tm
    return pl.pallas_call(
        _expert_kernel,
        grid=(e, nt, fdim // fc),
        in_specs=[pl.BlockSpec((tm, d), lambda ei, t, f: (ei * nt + t, 0)),
                  pl.BlockSpec((None, 1, tm), lambda ei, t, f: (ei * nt + t, 0, 0)),
                  pl.BlockSpec((None, d, fc), lambda ei, t, f: (ei, 0, f)),
                  pl.BlockSpec((None, d, fc), lambda ei, t, f: (ei, 0, f)),
                  pl.BlockSpec((None, fc, d), lambda ei, t, f: (ei, f, 0))],
        out_specs=pl.BlockSpec((tm, d), lambda ei, t, f: (ei * nt + t, 0)),
        out_shape=jax.ShapeDtypeStruct((rows, d), bf16),
        scratch_shapes=[pltpu.VMEM((tm, d), bf16), pltpu.VMEM((tm, d), f32),
                        pltpu.VMEM((tm, LANES), f32)],
        compiler_params=_params("arbitrary", "arbitrary", "arbitrary"),
        name="experts",
    )(xe, g_rows.reshape(rows // tm, 1, tm), wg_b, wu_b, wd_b)


def _combine_kernel(cbk_ref, cs_ref, x1_ref, mod_ref, ye_ref, o_ref, win_ref, xwin_ref, sem, xsem,
                    acc_ref, *, cap, tbc, win, ne, total_rows):
    blk = pl.program_id(0)
    nblk = pl.num_programs(0)
    d = acc_ref.shape[-1]
    slot = blk % 2

    def region(ee):
        return ee * cap

    def first_row(bb, ee):
        st = cbk_ref[ee * CBK_W + bb]
        row = region(ee) + jnp.left_shift(jnp.right_shift(st, 4), 4)
        return pl.multiple_of(jnp.minimum(row, total_rows - win), BF16_ROWS)

    def start_block(bb, sl):
        for ee in range(ne):
            pltpu.make_async_copy(ye_ref.at[pl.ds(first_row(bb, ee), win)],
                                  win_ref.at[sl, pl.ds(ee * win, win)], sem.at[sl]).start()

    @pl.when(blk == 0)
    def _():
        start_block(blk, slot)

    @pl.when(blk + 1 < nblk)
    def _():
        start_block(blk + 1, (blk + 1) % 2)

    pltpu.make_async_copy(ye_ref.at[pl.ds(0, ne * win)], win_ref.at[slot], sem.at[slot]).wait()

    w_iota = lax.broadcasted_iota(i32, (win, tbc), 0)

    def onehot_of(cs_row, base_slot, done_slots):
        hit = jnp.logical_and(cs_row == base_slot + 1 + w_iota, cs_row > done_slots)
        return hit.astype(f32).T.astype(bf16)

    onehot = jnp.concatenate(
        [onehot_of(cs_ref[ee:ee + 1, :], first_row(blk, ee) - region(ee), 0) for ee in range(ne)],
        axis=1)
    acc_ref[...] = jnp.dot(onehot, win_ref[slot], preferred_element_type=f32)

    def extra_expert(ee, carry):
        row0 = first_row(blk, ee)
        base0 = row0 - region(ee)
        end = cbk_ref[ee * CBK_W + blk + 1]
        cs_row = cs_ref[pl.ds(ee, 1), :]

        def extra(w, c2):
            row_w = pl.multiple_of(jnp.minimum(row0 + w * win, total_rows - win), BF16_ROWS)
            cp = pltpu.make_async_copy(ye_ref.at[pl.ds(row_w, win)], xwin_ref, xsem)
            cp.start()
            cp.wait()
            oh = onehot_of(cs_row, row_w - region(ee), base0 + w * win)
            acc_ref[...] += jnp.dot(oh, xwin_ref[...], preferred_element_type=f32)
            return c2

        lax.fori_loop(1, (end - base0 + win - 1) // win, extra, 0)
        return carry

    lax.fori_loop(0, ne, extra_expert, 0)

    gate2 = mod_ref[:, 5 * d:6 * d]
    o_ref[...] = x1_ref[...] + gate2 * acc_ref[...]


def _combine(cbk, cs, x1, mod, ye, cap, tbc):
    b, l, d = x1.shape
    n = b * l
    e = cs.shape[0]
    nb = n // tbc
    win = min(tbc // 4, cap)
    assert nb + 1 <= CBK_W
    per_b = l // tbc
    kern = functools.partial(_combine_kernel, cap=cap, tbc=tbc, win=win, ne=e,
                             total_rows=ye.shape[0])
    return pl.pallas_call(
        kern,
        grid_spec=pltpu.PrefetchScalarGridSpec(
            num_scalar_prefetch=1, grid=(nb,),
            in_specs=[pl.BlockSpec((e, tbc), lambda bk, c: (0, bk)),
                      pl.BlockSpec((tbc, d), lambda bk, c: (bk, 0)),
                      pl.BlockSpec((None, 1, mod.shape[-1]), lambda bk, c: (bk // per_b, 0, 0)),
                      pl.BlockSpec(memory_space=pl.ANY)],
            out_specs=pl.BlockSpec((tbc, d), lambda bk, c: (bk, 0)),
            scratch_shapes=[pltpu.VMEM((2, e * win, d), bf16), pltpu.VMEM((win, d), bf16),
                            pltpu.SemaphoreType.DMA((2,)), pltpu.SemaphoreType.DMA,
                            pltpu.VMEM((tbc, d), f32)]),
        out_shape=jax.ShapeDtypeStruct((n, d), f32),
        compiler_params=_params("arbitrary"),
        name="combine",
    )(cbk.reshape(-1), cs, x1.reshape(n, d), mod, ye).reshape(b, l, d)


TBF = 256
TM_EXPERT = 2048
FC_EXPERT = 256


def kernel(x_prompt, x_sample, c_prompt, c_sample, ada_w, ada_b, norm1_g, w_in, q_norm_g, k_norm_g, rpb,
           conv_w, conv_b, conv_ln_g, conv_ln_b, w_out, norm2_g, w_router, w_gate, w_up, w_down):
    xs = (x_prompt, x_sample)
    d = x_prompt.shape[-1]
    ne = w_router.shape[1]
    row = lambda a: a.reshape(1, -1).astype(f32)

    nb0, nb1 = c_prompt.shape[0], c_sample.shape[0]
    pad = (-(nb0 + nb1)) % 8
    c_all = jnp.concatenate([c_prompt, c_sample, jnp.zeros((pad, d), f32)], axis=0)
    mod_all = _ada(c_all, ada_w, ada_b)
    mods = (mod_all[:nb0].reshape(nb0, 1, -1), mod_all[nb0:nb0 + nb1].reshape(nb1, 1, -1))

    seg_i = jnp.arange(ATTN_W) // HEAD_DIM
    seg = (seg_i[:, None] == seg_i[None, :]).astype(bf16)
    gq_t = jnp.tile(q_norm_g.astype(f32), N_HEADS).reshape(1, ATTN_W)
    gk_t = jnp.tile(k_norm_g.astype(f32), N_HEADS).reshape(1, ATTN_W)
    bias_tab = _bias_table(rpb)
    w_in_b = w_in.astype(bf16)
    w_out_b = w_out.astype(bf16)

    staged = []
    for x, mod in zip(xs, mods):
        b, l, _ = x.shape
        n = b * l
        cap = max(1, EC_FACTOR * n // ne)
        q, k, v, u = _inproj(x, mod, row(norm1_g), w_in_b, seg, gq_t, gk_t)
        attn = _attention(q, k, v, bias_tab)
        conv = _conv(u, conv_w.astype(f32), row(conv_b), row(conv_ln_g), row(conv_ln_b), seg)
        x1, h2, aff_t = _outproj(attn, conv, x, mod, w_out_b, row(norm2_g), w_router)
        aff3 = aff_t.reshape(ne, n // LANES, LANES)
        cs, cbk = _route(aff3, cap, TBF)
        xe, g = _sc_dispatch(cs.reshape(-1), aff3.reshape(-1), h2, ne, cap)
        staged.append((x1, mod, xe, g.reshape(ne, cap), cs.reshape(ne, n), cbk, cap))

    outs = []
    for x1, mod, xe, g, cs, cbk, cap in staged:
        ye = _experts(xe, g, w_gate, w_up, w_down, min(TM_EXPERT, cap), min(FC_EXPERT, w_gate.shape[2]))
        outs.append(_combine(cbk, cs, x1, mod, ye, cap, TBF))
    return tuple(outs)
```
